```python
import jax
import jax.numpy as jnp
from jax import lax
import numpy as np

D_MODEL = 1024
BATCH = 2
SEQ = 8192
DEPTH = 4
DEC_BATCH = 8
DEC_SEQ = 64
PAST_LEN = 4096

CHUNK = 64
QBLOCK = 128
N_BRANCH = 3
BRANCH_W = D_MODEL // 2
H_RET = 4
DV_RET = BRANCH_W // H_RET
DK_RET = DV_RET // 2
ROPE_BASE = 10000.0
H_FOX = 8
HD_FOX = BRANCH_W // H_FOX
H_GDN = 4
DK_GDN = BRANCH_W // H_GDN
DV_GDN = BRANCH_W // H_GDN
CONV_W = 4
CONV_CH = 2 * H_GDN * DK_GDN + H_GDN * DV_GDN
NORM_EPS = 1e-6
IN_SPLITS = (H_RET * DK_RET, H_RET * DK_RET, H_RET * DV_RET, BRANCH_W,
             H_FOX * HD_FOX, H_FOX * HD_FOX, H_FOX * HD_FOX, H_FOX, BRANCH_W,
             H_GDN * DK_GDN, H_GDN * DK_GDN, H_GDN * DV_GDN, H_GDN, H_GDN, BRANCH_W,
             N_BRANCH * D_MODEL)
IN_W = sum(IN_SPLITS)

kernel_name = 'hybrid_stream_encoder_step'


def rmsnorm(x, w):
    xf = x.astype(jnp.float32)
    xf = xf * lax.rsqrt(jnp.mean(xf * xf, axis=-1, keepdims=True) + NORM_EPS)
    return (xf * w.astype(jnp.float32)).astype(x.dtype)


def l2norm(x):
    return x * lax.rsqrt(jnp.sum(x * x, axis=-1, keepdims=True) + NORM_EPS)


def rotary(x, pos):
    half = x.shape[-1] // 2
    inv_freq = ROPE_BASE ** (-jnp.arange(half, dtype=jnp.float32) / half)
    ang = pos.astype(jnp.float32)[:, None] * inv_freq[None, :]
    cos = jnp.cos(ang)[None, :, None, :]
    sin = jnp.sin(ang)[None, :, None, :]
    xf = x.astype(jnp.float32)
    x1, x2 = xf[..., :half], xf[..., half:]
    return jnp.concatenate([x1 * cos - x2 * sin, x2 * cos + x1 * sin], axis=-1)


def retention(q, k, v, s0, chunk):
    B, S, H, DK = q.shape
    n = S // chunk
    log_g = jnp.log1p(-jnp.exp2(-5.0 - jnp.arange(H, dtype=jnp.float32)))
    i = jnp.arange(chunk, dtype=jnp.float32)
    d_intra = jnp.exp(jnp.abs(i[:, None] - i[None, :])[None] * log_g[:, None, None])
    d_q = jnp.exp((i[:, None] + 1.0) * log_g[None, :])
    d_k = jnp.exp((chunk - 1.0 - i)[:, None] * log_g[None, :])
    d_c = jnp.exp(chunk * log_g)

    def split(t):
        return t.reshape(B, n, chunk, H, t.shape[-1]).swapaxes(0, 1)

    def step(s, blk):
        qb, kb, vb = blk
        a = jnp.einsum('bihd,bjhd->bhij', qb, kb) * d_intra
        o = (jnp.einsum('bhij,bjhe->bihe', a, vb)
             + jnp.einsum('bihd,bhde->bihe', qb * d_q[None, :, :, None], s))
        s = s * d_c[:, None, None] + jnp.einsum('bjhd,bjhe->bhde', kb * d_k[None, :, :, None], vb)
        return s, o

    s, o = lax.scan(step, s0, (split(q), split(k), split(v)))
    return o.swapaxes(0, 1).reshape(B, S, H, v.shape[-1]), s


def fox_block(q, k, v, fq, fk, qpos, kpos):
    logits = jnp.einsum('bhqd,bhkd->bhqk', q, k).astype(jnp.float32) * (q.shape[-1] ** -0.5)
    logits = logits + fq[..., :, None] - fk[..., None, :]
    logits = jnp.where(qpos[:, None] >= kpos[None, :], logits, -jnp.inf)
    p = jax.nn.softmax(logits, axis=-1)
    return jnp.einsum('bhqk,bhkd->bhqd', p.astype(v.dtype), v)


def fox_prompt(q, k, v, fcum):
    B, H, S, Dh = q.shape
    nb = S // QBLOCK
    pos = jnp.arange(S)
    qb = q.reshape(B, H, nb, QBLOCK, Dh).transpose(2, 0, 1, 3, 4)
    fb = fcum.reshape(B, H, nb, QBLOCK).transpose(2, 0, 1, 3)
    pb = pos.reshape(nb, QBLOCK)
    o = lax.map(lambda blk: fox_block(blk[0], k, v, blk[1], fcum, blk[2], pos), (qb, fb, pb))
    return o.transpose(1, 2, 0, 3, 4).reshape(B, H, S, Dh)


def causal_conv_silu(u, buf, w):
    S = u.shape[1]
    full = jnp.concatenate([buf.astype(u.dtype), u], axis=1)
    y = full[:, 0:S] * w[0]
    for j in range(1, CONV_W):
        y = y + full[:, j:j + S] * w[j]
    return jax.nn.silu(y), full[:, -(CONV_W - 1):]


def gated_delta_rule(q, k, v, g, beta, s0, chunk):
    B, S, H, DK = q.shape
    n = S // chunk

    def split(t):
        return t.reshape(B, n, chunk, H, t.shape[-1]).transpose(1, 0, 3, 2, 4)

    gc = split(g[..., None])[..., 0]
    bc = split(beta[..., None])[..., 0]
    tri = jnp.tril(jnp.ones((chunk, chunk), bool))
    strict = jnp.tril(jnp.ones((chunk, chunk), bool), -1)
    eye = jnp.eye(chunk, dtype=jnp.float32)

    def step(s, blk):
        qb, kb, vb, gb, bb = blk
        G = jnp.cumsum(gb, axis=-1)
        decay = jnp.exp(jnp.where(tri, G[..., :, None] - G[..., None, :], -jnp.inf))
        a = jnp.where(strict, jnp.einsum('bhik,bhjk->bhij', kb, kb) * decay, 0.0) * bb[..., :, None]
        lower = eye + a
        u = lax.linalg.triangular_solve(lower, vb * bb[..., None], left_side=True,
                                        lower=True, unit_diagonal=True)
        w = lax.linalg.triangular_solve(lower, kb * (bb * jnp.exp(G))[..., None], left_side=True,
                                        lower=True, unit_diagonal=True)
        u = u - jnp.einsum('bhck,bhkv->bhcv', w, s)
        o = (jnp.einsum('bhck,bhkv->bhcv', qb * jnp.exp(G)[..., None], s)
             + jnp.einsum('bhij,bhjv->bhiv', jnp.einsum('bhik,bhjk->bhij', qb, kb) * decay, u))
        g_last = G[..., -1:]
        s = s * jnp.exp(g_last)[..., None] + jnp.einsum(
            'bhck,bhcv->bhkv', kb * jnp.exp(g_last - G)[..., None], u)
        return s, o

    s, o = lax.scan(step, s0, (split(q), split(k), split(v), gc, bc))
    return o.transpose(1, 0, 3, 2, 4).reshape(B, S, H, v.shape[-1]), s


def trunk_layer(x, c, pos, hist, norm_w, ada_w, ada_b, w_in, fox_f_bias, gdn_a_log,
                gdn_dt_bias, gdn_conv_w, ret_norm_w, gdn_norm_w, w_branch, w_out):
    B, S, _ = x.shape
    f32 = jnp.float32
    mod = jax.nn.silu(c) @ ada_w + ada_b
    shift, scale, gate = jnp.split(mod[:, None, :], 3, axis=-1)
    h = rmsnorm(x, norm_w) * (1.0 + scale) + shift
    proj = h @ w_in
    (rq, rk, rv, rz, fq, fk, fv, ff, fz, gq, gk, gv, ga, gb, gz, mg) = jnp.split(
        proj, np.cumsum(IN_SPLITS)[:-1].tolist(), axis=-1)

    if hist is None:
        ret_s0 = jnp.zeros((B, H_RET, DK_RET, DV_RET), f32)
        gdn_s0 = jnp.zeros((B, H_GDN, DK_GDN, DV_GDN), f32)
        conv_buf = jnp.zeros((B, CONV_W - 1, CONV_CH), proj.dtype)
        chunk = CHUNK
    else:
        fox_k_c, fox_v_c, fox_lf_c, ret_s0, gdn_s0, conv_buf = hist
        chunk = S

    q_r = rotary(rq.reshape(B, S, H_RET, DK_RET), pos)
    k_r = rotary(rk.reshape(B, S, H_RET, DK_RET), pos) * (DK_RET ** -0.5)
    v_r = rv.reshape(B, S, H_RET, DV_RET).astype(f32)
    o_r, ret_s = retention(q_r, k_r, v_r, ret_s0.astype(f32), chunk)
    o_ret = rmsnorm(o_r, ret_norm_w.reshape(H_RET, DV_RET)).reshape(B, S, BRANCH_W)
    o_ret = o_ret.astype(x.dtype) * jax.nn.silu(rz)

    q_f = fq.reshape(B, S, H_FOX, HD_FOX).transpose(0, 2, 1, 3)
    k_f = fk.reshape(B, S, H_FOX, HD_FOX).transpose(0, 2, 1, 3)
    v_f = fv.reshape(B, S, H_FOX, HD_FOX).transpose(0, 2, 1, 3)
    logf = jax.nn.log_sigmoid(ff.astype(f32) + fox_f_bias.astype(f32)).transpose(0, 2, 1)
    if hist is None:
        o_f = fox_prompt(q_f, k_f, v_f, jnp.cumsum(logf, axis=-1))
    else:
        past = fox_k_c.shape[2]
        k_all = jnp.concatenate([fox_k_c.astype(k_f.dtype), k_f], axis=2)
        v_all = jnp.concatenate([fox_v_c.astype(v_f.dtype), v_f], axis=2)
        fcum = jnp.cumsum(jnp.concatenate([fox_lf_c.astype(f32), logf], axis=-1), axis=-1)
        o_f = fox_block(q_f, k_all, v_all, fcum[..., past:], fcum, pos, jnp.arange(past + S))
    o_fox = o_f.transpose(0, 2, 1, 3).reshape(B, S, BRANCH_W) * jax.nn.silu(fz)

    qkv, conv_new = causal_conv_silu(jnp.concatenate([gq, gk, gv], axis=-1), conv_buf, gdn_conv_w)
    q_g, k_g, v_g = jnp.split(qkv.astype(f32), [H_GDN * DK_GDN, 2 * H_GDN * DK_GDN], axis=-1)
    q_g = l2norm(q_g.reshape(B, S, H_GDN, DK_GDN)) * (DK_GDN ** -0.5)
    k_g = l2norm(k_g.reshape(B, S, H_GDN, DK_GDN))
    v_g = v_g.reshape(B, S, H_GDN, DV_GDN)
    g = -jnp.exp(gdn_a_log.astype(f32)) * jax.nn.softplus(ga.astype(f32) + gdn_dt_bias.astype(f32))
    beta = jax.nn.sigmoid(gb.astype(f32))
    o_g, gdn_s = gated_delta_rule(q_g, k_g, v_g, g, beta, gdn_s0.astype(f32), chunk)
    o_gdn = rmsnorm(o_g, gdn_norm_w).reshape(B, S, BRANCH_W).astype(x.dtype) * jax.nn.silu(gz)

    br = jnp.einsum('bsnc,ncd->bsnd', jnp.stack([o_ret, o_fox, o_gdn], axis=2), w_branch)
    mg = jax.nn.sigmoid(mg.reshape(B, S, N_BRANCH, D_MODEL).astype(f32)).astype(x.dtype)
    out = jnp.sum(br * mg, axis=2) @ w_out
    x = x + gate * out
    new_state = (k_f, v_f, logf.astype(x.dtype), ret_s.astype(x.dtype),
                 gdn_s.astype(x.dtype), conv_new)
    return x, new_state


def setup_inputs(seed: int = 0) -> dict:
    key = jax.random.key(seed)
    ks = jax.random.split(key, 24)
    f32 = jnp.float32

    def nrm(k, shape, s):
        return jax.random.normal(k, shape, f32) * s

    dt = jnp.exp(jax.random.uniform(ks[15], (DEPTH, H_GDN), f32, np.log(1e-3), np.log(1e-1)))
    return {
        'x_prompt': nrm(ks[0], (BATCH, SEQ, D_MODEL), 1.0),
        'x_sample': nrm(ks[1], (DEC_BATCH, DEC_SEQ, D_MODEL), 1.0),
        'c_prompt': nrm(ks[2], (BATCH, D_MODEL), 1.0),
        'c_sample': nrm(ks[3], (DEC_BATCH, D_MODEL), 1.0),
        'cache_fox_k': nrm(ks[4], (DEPTH, DEC_BATCH, H_FOX, PAST_LEN, HD_FOX), 1.0),
        'cache_fox_v': nrm(ks[5], (DEPTH, DEC_BATCH, H_FOX, PAST_LEN, HD_FOX), 1.0),
        'cache_fox_logf': jax.nn.log_sigmoid(3.0 + nrm(ks[6], (DEPTH, DEC_BATCH, H_FOX, PAST_LEN), 1.0)),
        'state_ret': nrm(ks[7], (DEPTH, DEC_BATCH, H_RET, DK_RET, DV_RET), 1.0),
        'state_gdn': nrm(ks[8], (DEPTH, DEC_BATCH, H_GDN, DK_GDN, DV_GDN), 0.1),
        'state_gdn_conv': nrm(ks[9], (DEPTH, DEC_BATCH, CONV_W - 1, CONV_CH), 1.0),
        'norm_w': 1.0 + nrm(ks[10], (DEPTH, D_MODEL), 0.1),
        'ada_w': nrm(ks[11], (DEPTH, D_MODEL, 3 * D_MODEL), 0.5 * D_MODEL ** -0.5),
        'ada_b': nrm(ks[12], (DEPTH, 3 * D_MODEL), 0.02),
        'w_in': nrm(ks[13], (DEPTH, D_MODEL, IN_W), D_MODEL ** -0.5),
        'fox_f_bias': jax.random.uniform(ks[14], (DEPTH, H_FOX), f32, 1.0, 5.0),
        'gdn_a_log': jnp.log(jax.random.uniform(ks[16], (DEPTH, H_GDN), f32, 1.0, 16.0)),
        'gdn_dt_bias': dt + jnp.log(-jnp.expm1(-dt)),
        'gdn_conv_w': nrm(ks[17], (DEPTH, CONV_W, CONV_CH), CONV_W ** -0.5),
        'ret_norm_w': 1.0 + nrm(ks[18], (DEPTH, BRANCH_W), 0.1),
        'gdn_norm_w': 1.0 + nrm(ks[19], (DEPTH, DV_GDN), 0.1),
        'w_branch': nrm(ks[20], (DEPTH, N_BRANCH, BRANCH_W, D_MODEL), BRANCH_W ** -0.5),
        'w_out': nrm(ks[21], (DEPTH, D_MODEL, D_MODEL), D_MODEL ** -0.5),
        'final_norm_w': 1.0 + nrm(ks[22], (D_MODEL,), 0.1),
    }


def reference(x_prompt, x_sample, c_prompt, c_sample, cache_fox_k, cache_fox_v, cache_fox_logf,
              state_ret, state_gdn, state_gdn_conv, norm_w, ada_w, ada_b, w_in, fox_f_bias,
              gdn_a_log, gdn_dt_bias, gdn_conv_w, ret_norm_w, gdn_norm_w, w_branch, w_out,
              final_norm_w):
    past = cache_fox_k.shape[3]
    pos_p = jnp.arange(x_prompt.shape[1])
    pos_s = past + jnp.arange(x_sample.shape[1])
    xp, xs = x_prompt, x_sample
    p_states, s_states = [], []
    for l in range(DEPTH):
        wl = (norm_w[l], ada_w[l], ada_b[l], w_in[l], fox_f_bias[l], gdn_a_log[l], gdn_dt_bias[l],
              gdn_conv_w[l], ret_norm_w[l], gdn_norm_w[l], w_branch[l], w_out[l])
        xp, st_p = trunk_layer(xp, c_prompt, pos_p, None, *wl)
        p_states.append(st_p)
        hist = (cache_fox_k[l], cache_fox_v[l], cache_fox_logf[l], state_ret[l], state_gdn[l],
                state_gdn_conv[l])
        xs, st_s = trunk_layer(xs, c_sample, pos_s, hist, *wl)
        s_states.append(st_s)
    y_prompt = rmsnorm(xp, final_norm_w)
    y_sample = rmsnorm(xs, final_norm_w)

    def stacked(states, i):
        return jnp.stack([st[i] for st in states])

    return (y_prompt, y_sample,
            stacked(p_states, 0), stacked(p_states, 1), stacked(p_states, 2),
            stacked(p_states, 3), stacked(p_states, 4), stacked(p_states, 5),
            stacked(s_states, 0), stacked(s_states, 1), stacked(s_states, 2),
            stacked(s_states, 3), stacked(s_states, 4), stacked(s_states, 5))
```

```python
import functools

import numpy as np
import jax
import jax.numpy as jnp
from jax import lax
from jax.experimental import pallas as pl
from jax.experimental.pallas import tpu as pltpu

F32 = jnp.float32
BF16 = jnp.bfloat16

N_BRANCH = 3
H_RET = 4
H_FOX = 8
H_GDN = 4
CONV_W = 4
CHUNK = 64
NORM_EPS = 1e-6
ROPE_BASE = 10000.0
LANES = 128
GDN_CHUNK = 128
VMEM_LIMIT = 56 * 1024 * 1024


def _cparams(sem):
    return pltpu.CompilerParams(dimension_semantics=sem, vmem_limit_bytes=VMEM_LIMIT)


def _sigmoid(x):
    return 1.0 / (1.0 + jnp.exp(-x))


def _silu(x):
    return x * _sigmoid(x)


def _softplus(x):
    return jnp.maximum(x, 0.0) + jnp.log(1.0 + jnp.exp(-jnp.abs(x)))


def _dot(a, b):
    return jnp.dot(a.astype(BF16), b.astype(BF16), preferred_element_type=F32)


def _dot_nt(a, b):
    return lax.dot_general(a.astype(BF16), b.astype(BF16), (((1,), (1,)), ((), ())),
                           preferred_element_type=F32)


def _split2(x):
    hi = x.astype(BF16)
    lo = (x - hi.astype(F32)).astype(BF16)
    return hi, lo


def _split3(x):
    hi = x.astype(BF16)
    r = x - hi.astype(F32)
    mid = r.astype(BF16)
    lo = (r - mid.astype(F32)).astype(BF16)
    return hi, mid, lo


def _dot_hi(a, b):
    ah, al = _split2(a)
    bh, bl = _split2(b)
    d = lambda x, y: jnp.dot(x, y, preferred_element_type=F32)
    return d(ah, bh) + (d(ah, bl) + d(al, bh))


def _dot_sel_l(m, x):
    hi, mid, lo = _split3(x)
    d = lambda y: jnp.dot(m, y, preferred_element_type=F32)
    return d(hi) + (d(mid) + d(lo))


def _dot_sel_r(x, m):
    hi, mid, lo = _split3(x)
    d = lambda y: jnp.dot(y, m, preferred_element_type=F32)
    return d(hi) + (d(mid) + d(lo))


def _iota2(shape, dim):
    return lax.broadcasted_iota(jnp.int32, shape, dim)


def _adaln_kernel(c_ref, w_ref, b_ref, o_ref):
    s = _silu(c_ref[...])
    o_ref[...] = _dot(s, w_ref[...]) + b_ref[...]


def _adaln(c_all, ada_w, ada_b):
    L, D, _ = ada_w.shape
    R = c_all.shape[0]
    return pl.pallas_call(
        _adaln_kernel,
        grid=(L, 3),
        in_specs=[pl.BlockSpec((R, D), lambda l, k: (0, 0)),
                  pl.BlockSpec((None, D, D), lambda l, k: (l, 0, k)),
                  pl.BlockSpec((None, None, 1, D), lambda l, k: (l, k, 0, 0))],
        out_specs=pl.BlockSpec((None, None, R, D), lambda l, k: (l, k, 0, 0)),
        out_shape=jax.ShapeDtypeStruct((L, 3, R, D), F32),
        compiler_params=_cparams(("parallel", "parallel")),
        name="adaln",
    )(c_all, ada_w, ada_b.reshape(L, 3, 1, D))


PROJ_TN = 512


def _inproj_kernel(x_ref, nw_ref, sc_ref, sh_ref, w_ref, ws_ref, o_ref, os_ref, h_ref):
    bb, tm, D = x_ref.shape

    @pl.when(pl.program_id(2) == 0)
    def _():
        x = x_ref[...]
        ms = jnp.mean(x * x, axis=-1, keepdims=True)
        xn = (x * lax.rsqrt(ms + NORM_EPS)) * nw_ref[...]
        h = xn * (1.0 + sc_ref[...]) + sh_ref[...]
        hb = h.astype(BF16).reshape(bb * tm, D)
        h_ref[...] = hb
        os_ref[...] = jnp.dot(hb, ws_ref[...], preferred_element_type=F32).reshape(bb, tm, LANES)

    o_ref[...] = jnp.dot(h_ref[...], w_ref[...],
                         preferred_element_type=F32).reshape(bb, tm, PROJ_TN)


def _inproj(x, mod, l, norm_w, w_main, w_small, bb, tm):
    B, S, D = x.shape
    NP = w_main.shape[-1]
    nj = NP // PROJ_TN
    return pl.pallas_call(
        _inproj_kernel,
        grid=(B // bb, S // tm, nj),
        in_specs=[pl.BlockSpec((bb, tm, D), lambda b, i, j: (b, i, 0)),
                  pl.BlockSpec((None, 1, D), lambda b, i, j: (l, 0, 0)),
                  pl.BlockSpec((None, None, bb, 1, D), lambda b, i, j: (l, 1, b, 0, 0)),
                  pl.BlockSpec((None, None, bb, 1, D), lambda b, i, j: (l, 0, b, 0, 0)),
                  pl.BlockSpec((None, D, PROJ_TN), lambda b, i, j: (l, 0, j)),
                  pl.BlockSpec((None, D, LANES), lambda b, i, j: (l, 0, 0))],
        out_specs=[pl.BlockSpec((bb, tm, PROJ_TN), lambda b, i, j: (b, i, j)),
                   pl.BlockSpec((bb, tm, LANES), lambda b, i, j: (b, i, 0))],
        out_shape=[jax.ShapeDtypeStruct((B, S, NP), F32),
                   jax.ShapeDtypeStruct((B, S, LANES), F32)],
        scratch_shapes=[pltpu.VMEM((bb * tm, D), BF16)],
        compiler_params=_cparams(("parallel", "parallel", "arbitrary")),
        name="inproj",
    )(x, norm_w, mod, mod, w_main, w_small)


def _merge_kernel(x_ref, g_ref, o1_ref, o2_ref, o3_ref, mg_ref, wb_ref, wo_ref, fw_ref, y_ref,
                  *, final):
    bb, tm, D = x_ref.shape
    M = bb * tm
    acc = None
    for n, o_ref in enumerate((o1_ref, o2_ref, o3_ref)):
        br = jnp.dot(o_ref[...].reshape(M, o_ref.shape[-1]).astype(BF16), wb_ref[n],
                     preferred_element_type=F32)
        gate = _sigmoid(mg_ref[:, :, n * D:(n + 1) * D].reshape(M, D))
        term = br * gate
        acc = term if acc is None else acc + term
    out = jnp.dot(acc.astype(BF16), wo_ref[...], preferred_element_type=F32)
    xn = x_ref[...] + g_ref[...] * out.reshape(bb, tm, D)
    if final:
        ms = jnp.mean(xn * xn, axis=-1, keepdims=True)
        xn = (xn * lax.rsqrt(ms + NORM_EPS)) * fw_ref[...]
    y_ref[...] = xn


def _merge(x, mod, l, o_ret, o_fox, o_gdn, proj, w_branch, w_out, final_w, bb, tm, final):
    B, S, D = x.shape
    W = o_ret.shape[-1]
    return pl.pallas_call(
        functools.partial(_merge_kernel, final=final),
        grid=(B // bb, S // tm),
        in_specs=[pl.BlockSpec((bb, tm, D), lambda b, i: (b, i, 0)),
                  pl.BlockSpec((None, None, bb, 1, D), lambda b, i: (l, 2, b, 0, 0)),
                  pl.BlockSpec((bb, tm, W), lambda b, i: (b, i, 0)),
                  pl.BlockSpec((bb, tm, W), lambda b, i: (b, i, 0)),
                  pl.BlockSpec((bb, tm, W), lambda b, i: (b, i, 0)),
                  pl.BlockSpec((bb, tm, N_BRANCH * D), lambda b, i: (b, i, 0)),
                  pl.BlockSpec((None, N_BRANCH, W, D), lambda b, i: (l, 0, 0, 0)),
                  pl.BlockSpec((None, D, D), lambda b, i: (l, 0, 0)),
                  pl.BlockSpec((1, D), lambda b, i: (0, 0))],
        out_specs=pl.BlockSpec((bb, tm, D), lambda b, i: (b, i, 0)),
        out_shape=jax.ShapeDtypeStruct((B, S, D), F32),
        compiler_params=_cparams(("parallel", "parallel")),
        name="merge",
    )(x, mod, o_ret, o_fox, o_gdn, proj, w_branch, w_out, final_w)


def _ret_kernel(qk_ref, v_ref, z_ref, cos_ref, sin_ref, m_ref, dq_ref, dk_ref, dc_ref, nw_ref,
                s0_ref, o_ref, so_ref, s_ref):
    T = qk_ref.shape[0]
    H = s_ref.shape[0]
    DK = s0_ref.shape[1]
    t = pl.program_id(1)

    @pl.when(t == 0)
    def _():
        zero = jnp.zeros((DK, LANES), F32)
        for h in range(H):
            s0 = s0_ref[h]
            s_ref[h] = jnp.concatenate([s0, zero] if h % 2 == 0 else [zero, s0], axis=0)

    cos = cos_ref[...]
    sin = sin_ref[...]
    lane = _iota2((T, LANES), 1)
    first = (lane % DK) < (DK // 2)
    low = lane < DK

    def rot(x):
        sw = jnp.where(first, pltpu.roll(x, LANES - DK // 2, 1), pltpu.roll(x, DK // 2, 1))
        return x * cos + sw * sin

    nkq = H * DK
    for p in range(H // 2):
        qc = rot(qk_ref[:, p * LANES:(p + 1) * LANES])
        kc = rot(qk_ref[:, nkq + p * LANES:nkq + (p + 1) * LANES]) * (DK ** -0.5)
        kcb = kc.astype(BF16)
        for half in range(2):
            h = 2 * p + half
            sel = low if half == 0 else jnp.logical_not(low)
            qm = jnp.where(sel, qc, 0.0)
            a = _dot_nt(qm, kcb) * m_ref[h]
            vh = v_ref[:, h * LANES:(h + 1) * LANES].astype(BF16)
            s = s_ref[h]
            o = _dot(a, vh) + _dot(qm * dq_ref[h], s)
            km = jnp.where(sel, kc, 0.0) * dk_ref[h]
            s_ref[h] = s * dc_ref[h] + _dot(km.T, vh)
            ms = jnp.mean(o * o, axis=-1, keepdims=True)
            on = (o * lax.rsqrt(ms + NORM_EPS)) * nw_ref[h]
            z = z_ref[:, h * LANES:(h + 1) * LANES]
            o_ref[:, h * LANES:(h + 1) * LANES] = (on * _silu(z)).astype(o_ref.dtype)

    @pl.when(t == pl.num_programs(1) - 1)
    def _():
        for h in range(H):
            so_ref[h] = s_ref[h][(h % 2) * DK:(h % 2 + 1) * DK, :]


def _ret_tables(T, pos0, S, DK):
    half = DK // 2
    inv_freq = ROPE_BASE ** (-jnp.arange(half, dtype=F32) / half)
    pos = pos0 + jnp.arange(S)
    ang = pos.astype(F32)[:, None] * inv_freq[None, :]
    cos, sin = jnp.cos(ang), jnp.sin(ang)
    reps = LANES // DK
    cos_t = jnp.tile(jnp.concatenate([cos, cos], axis=1), (1, reps))
    sin_t = jnp.tile(jnp.concatenate([-sin, sin], axis=1), (1, reps))
    log_g = jnp.log1p(-jnp.exp2(-5.0 - jnp.arange(H_RET, dtype=F32)))
    i = jnp.arange(T)
    dist = jnp.abs(i[:, None] - i[None, :]).astype(F32)
    allowed = (i[None, :] // CHUNK) <= (i[:, None] // CHUNK)
    m = jnp.where(allowed[None], jnp.exp(dist[None] * log_g[:, None, None]), 0.0)
    fi = i.astype(F32)
    dq = jnp.exp((fi[None, :] + 1.0) * log_g[:, None])
    dk = jnp.exp((T - 1.0 - fi)[None, :] * log_g[:, None])
    dc = jnp.exp(T * log_g)
    bc = lambda a: jnp.broadcast_to(a[..., None], a.shape + (LANES,))
    return cos_t, sin_t, m, bc(dq), bc(dk), bc(dc[:, None])


def _ret(proj, cols, tabs, ret_nw, s0, T, out_dtype):
    B, S, _ = proj.shape
    H, DK, DV = s0.shape[1:]
    W = H * DV
    cqk, cv, cz = cols
    cos_t, sin_t, m, dq, dk, dc = tabs
    return pl.pallas_call(
        _ret_kernel,
        grid=(B, S // T),
        in_specs=[pl.BlockSpec((None, T, W), lambda b, t: (b, t, cqk)),
                  pl.BlockSpec((None, T, W), lambda b, t: (b, t, cv)),
                  pl.BlockSpec((None, T, W), lambda b, t: (b, t, cz)),
                  pl.BlockSpec((T, LANES), lambda b, t: (t, 0)),
                  pl.BlockSpec((T, LANES), lambda b, t: (t, 0)),
                  pl.BlockSpec((H, T, T), lambda b, t: (0, 0, 0)),
                  pl.BlockSpec((H, T, LANES), lambda b, t: (0, 0, 0)),
                  pl.BlockSpec((H, T, LANES), lambda b, t: (0, 0, 0)),
                  pl.BlockSpec((H, 1, LANES), lambda b, t: (0, 0, 0)),
                  pl.BlockSpec((H, 1, DV), lambda b, t: (0, 0, 0)),
                  pl.BlockSpec((None, H, DK, DV), lambda b, t: (b, 0, 0, 0))],
        out_specs=[pl.BlockSpec((None, T, W), lambda b, t: (b, t, 0)),
                   pl.BlockSpec((None, H, DK, DV), lambda b, t: (b, 0, 0, 0))],
        out_shape=[jax.ShapeDtypeStruct((B, S, W), out_dtype),
                   jax.ShapeDtypeStruct((B, H, DK, DV), F32)],
        scratch_shapes=[pltpu.VMEM((H, LANES, DV), F32)],
        compiler_params=_cparams(("parallel", "arbitrary")),
        name="ret",
    )(proj, proj, proj, cos_t, sin_t, m, dq, dk, dc, ret_nw, s0)


INV_BASE = 8


def _unit_lower_inverse(a, eye, ri, ci):
    n = a.shape[0]
    b = INV_BASE
    ad = jnp.where((ri // b) == (ci // b), a, 0.0)
    x = eye - ad
    p = ad
    k = 2
    while k < b:
        p = _dot_hi(p, p)
        x = x + _dot_hi(x, p)
        k *= 2
    while b < n:
        off = ((ri // (2 * b)) == (ci // (2 * b))) & ((ri // b) != (ci // b))
        x = x - _dot_hi(x, _dot_hi(jnp.where(off, a, 0.0), x))
        b *= 2
    return x


def _gdn_kernel(u_ref, z_ref, sm_ref, cw_ref, cb_ref, al_ref, dt_ref, nw_ref, s0_ref,
                o_ref, so_ref, co_ref, ext_ref, s_ref):
    T, C3 = u_ref.shape
    H = s_ref.shape[0]
    DK = s_ref.shape[1]
    W = H * DK
    C = GDN_CHUNK
    TP = -(-T // C) * C
    t = pl.program_id(1)
    last = pl.num_programs(1) - 1
    base = 8
    lo = base - (CONV_W - 1)

    @pl.when(t == 0)
    def _():
        s_ref[...] = s0_ref[...]
        ext_ref[lo:base, :] = cb_ref[...]

    ext_ref[base:base + T, :] = u_ref[...]
    y = ext_ref[lo:lo + T, :] * cw_ref[0:1, :]
    for j in range(1, CONV_W):
        y = y + ext_ref[lo + j:lo + j + T, :] * cw_ref[j:j + 1, :]
    tail = ext_ref[lo + T:base + T, :]
    ext_ref[lo:base, :] = tail

    @pl.when(t == last)
    def _():
        co_ref[...] = tail

    qkv = _silu(y)
    sm = sm_ref[...]
    g_all = -jnp.exp(al_ref[...]) * _softplus(sm + dt_ref[...])
    b_all = _sigmoid(sm)
    z_all = z_ref[...]
    if TP > T:
        pad = lambda a: jnp.concatenate([a, jnp.zeros((TP - T, a.shape[1]), F32)], axis=0)
        qkv, g_all, b_all, z_all = pad(qkv), pad(g_all), pad(b_all), pad(z_all)

    ri = _iota2((C, C), 0)
    ci = _iota2((C, C), 1)
    tri = ri >= ci
    strict = ri > ci
    ltri = jnp.where(tri, 1.0, 0.0).astype(BF16)
    eye = jnp.where(ri == ci, 1.0, 0.0)

    for c in range(TP // C):
        r0 = c * C
        gcum = _dot_sel_l(ltri, g_all[r0:r0 + C, :])
        for h in range(H):
            q = qkv[r0:r0 + C, h * DK:(h + 1) * DK]
            k = qkv[r0:r0 + C, W + h * DK:W + (h + 1) * DK]
            v = qkv[r0:r0 + C, 2 * W + h * DK:2 * W + (h + 1) * DK]
            q = q * lax.rsqrt(jnp.sum(q * q, axis=-1, keepdims=True) + NORM_EPS) * (DK ** -0.5)
            k = k * lax.rsqrt(jnp.sum(k * k, axis=-1, keepdims=True) + NORM_EPS)
            gc = gcum[:, 8 + h:9 + h]
            bc = b_all[r0:r0 + C, 12 + h:13 + h]
            gm = jnp.broadcast_to(gc, (C, C))
            dec = jnp.exp(jnp.where(tri, gm - gm.T, -jnp.inf))
            kb = k.astype(BF16)
            a = jnp.where(strict, _dot_nt(kb, kb) * dec, 0.0) * bc
            tinv = _unit_lower_inverse(a, eye, ri, ci)
            eg = jnp.exp(gc)
            glast = gcum[C - 1:C, 8 + h:9 + h]
            rhs = jnp.concatenate([k * (bc * eg), v * bc], axis=1)
            wu = _dot_hi(tinv, rhs)
            s = s_ref[h]
            sb = s.astype(BF16)
            u = wu[:, DK:] - _dot(wu[:, :DK], sb)
            qk = _dot_nt(q, kb) * dec
            o = _dot(q * eg, sb) + _dot(qk, u)
            kd = k * jnp.exp(glast - gc)
            s_ref[h] = s * jnp.exp(glast) + _dot(kd.T, u)
            ms = jnp.mean(o * o, axis=-1, keepdims=True)
            on = (o * lax.rsqrt(ms + NORM_EPS)) * nw_ref[...]
            z = z_all[r0:r0 + C, h * DK:(h + 1) * DK]
            res = (on * _silu(z)).astype(o_ref.dtype)
            if TP > T:
                o_ref[:, h * DK:(h + 1) * DK] = res[:T, :]
            else:
                o_ref[r0:r0 + C, h * DK:(h + 1) * DK] = res

    @pl.when(t == last)
    def _():
        so_ref[...] = s_ref[...]


def _gdn(proj, small, cols, conv_w, conv_buf, a_row, dt_row, gdn_nw, s0, T, out_dtype):
    B, S, _ = proj.shape
    H, DK, DV = s0.shape[1:]
    W = H * DK
    C3 = conv_w.shape[-1]
    cu, cz = cols
    return pl.pallas_call(
        _gdn_kernel,
        grid=(B, S // T),
        in_specs=[pl.BlockSpec((None, T, C3), lambda b, t: (b, t, cu)),
                  pl.BlockSpec((None, T, W), lambda b, t: (b, t, cz)),
                  pl.BlockSpec((None, T, LANES), lambda b, t: (b, t, 0)),
                  pl.BlockSpec((CONV_W, C3), lambda b, t: (0, 0)),
                  pl.BlockSpec((None, CONV_W - 1, C3), lambda b, t: (b, 0, 0)),
                  pl.BlockSpec((1, LANES), lambda b, t: (0, 0)),
                  pl.BlockSpec((1, LANES), lambda b, t: (0, 0)),
                  pl.BlockSpec((1, DV), lambda b, t: (0, 0)),
                  pl.BlockSpec((None, H, DK, DV), lambda b, t: (b, 0, 0, 0))],
        out_specs=[pl.BlockSpec((None, T, W), lambda b, t: (b, t, 0)),
                   pl.BlockSpec((None, H, DK, DV), lambda b, t: (b, 0, 0, 0)),
                   pl.BlockSpec((None, CONV_W - 1, C3), lambda b, t: (b, 0, 0))],
        out_shape=[jax.ShapeDtypeStruct((B, S, W), out_dtype),
                   jax.ShapeDtypeStruct((B, H, DK, DV), F32),
                   jax.ShapeDtypeStruct((B, CONV_W - 1, C3), F32)],
        scratch_shapes=[pltpu.VMEM((T + 8, C3), F32),
                        pltpu.VMEM((H, DK, DV), F32)],
        compiler_params=_cparams(("parallel", "arbitrary")),
        name="gdn",
    )(proj, proj, small, conv_w, conv_buf, a_row, dt_row, gdn_nw, s0)


def _logf_from_small(sm, fb):
    lane = _iota2(sm.shape, 1)
    return jnp.where(lane < H_FOX, -_softplus(-(sm + fb)), 0.0)


def _foxprep_kernel(qkv_ref, sm_ref, fb_ref, qa_ref, ka_ref, va_ref, ko_ref, vo_ref, lf_ref,
                    carry_ref):
    T = sm_ref.shape[0]
    H = qa_ref.shape[0]
    HD = ko_ref.shape[-1]
    W = H * HD
    t = pl.program_id(1)

    @pl.when(t == 0)
    def _():
        carry_ref[...] = jnp.zeros_like(carry_ref)

    logf = _logf_from_small(sm_ref[...], fb_ref[...])
    lf_ref[...] = logf
    ltri = jnp.where(_iota2((T, T), 0) >= _iota2((T, T), 1), 1.0, 0.0).astype(BF16)
    fc = _dot_sel_l(ltri, logf) + carry_ref[0:1, :]
    carry_ref[0:1, :] = fc[T - 1:T, :]
    f_hi, f_mid, f_lo = [p.astype(F32) for p in _split3(fc)]

    lane = _iota2((T, LANES), 1)
    low = lane < HD
    one_q = jnp.where((lane >= HD + 3) & (lane < HD + 6), 1.0, 0.0)
    one_k = jnp.where((lane >= HD) & (lane < HD + 3), 1.0, 0.0)
    one_v = jnp.where(lane == HD, 1.0, 0.0)
    scale = HD ** -0.5
    for p in range(H // 2):
        qc = qkv_ref[:, p * LANES:(p + 1) * LANES] * scale
        kc = qkv_ref[:, W + p * LANES:W + (p + 1) * LANES]
        vc = qkv_ref[:, 2 * W + p * LANES:2 * W + (p + 1) * LANES]
        for half in range(2):
            h = 2 * p + half
            if half == 1:
                qh, kh, vh = [pltpu.roll(a, HD, 1) for a in (qc, kc, vc)]
            else:
                qh, kh, vh = qc, kc, vc
            c_hi, c_mid, c_lo = [a[:, h:h + 1] for a in (f_hi, f_mid, f_lo)]
            ex_q = jnp.where(lane == HD, c_hi,
                             jnp.where(lane == HD + 1, c_mid,
                                       jnp.where(lane == HD + 2, c_lo, one_q)))
            ex_k = jnp.where(lane == HD + 3, -c_hi,
                             jnp.where(lane == HD + 4, -c_mid,
                                       jnp.where(lane == HD + 5, -c_lo, one_k)))
            qa_ref[h] = jnp.where(low, qh, ex_q).astype(BF16)
            ka_ref[h] = jnp.where(low, kh, ex_k).astype(BF16)
            va_ref[h] = jnp.where(low, vh, one_v).astype(BF16)
            ko_ref[h] = kh[:, :HD]
            vo_ref[h] = vh[:, :HD]


def _foxprep(proj, small, col, fb_row, l, k_prev, v_prev, L, T):
    B, S, _ = proj.shape
    H = H_FOX
    W = proj.shape[-1]
    HD = 512 // H
    aug = jax.ShapeDtypeStruct((B, H, S, LANES), BF16)
    kv = jax.ShapeDtypeStruct((L, B, H, S, HD), F32)
    aug_spec = pl.BlockSpec((None, H, T, LANES), lambda b, t: (b, 0, t, 0))
    kv_spec = pl.BlockSpec((None, None, H, T, HD), lambda b, t: (l, b, 0, t, 0))
    in_specs = [pl.BlockSpec((None, T, 3 * H * HD), lambda b, t: (b, t, col)),
                pl.BlockSpec((None, T, LANES), lambda b, t: (b, t, 0)),
                pl.BlockSpec((1, LANES), lambda b, t: (0, 0))]
    args = [proj, small, fb_row]
    aliases = {}
    kernel = _foxprep_kernel
    if k_prev is not None:
        in_specs += [pl.BlockSpec(memory_space=pl.ANY), pl.BlockSpec(memory_space=pl.ANY)]
        args += [k_prev, v_prev]
        aliases = {3: 3, 4: 4}
        kernel = lambda a, b, c, _k, _v, *rest: _foxprep_kernel(a, b, c, *rest)
    return pl.pallas_call(
        kernel,
        grid=(B, S // T),
        in_specs=in_specs,
        out_specs=[aug_spec, aug_spec, aug_spec, kv_spec, kv_spec,
                   pl.BlockSpec((None, T, LANES), lambda b, t: (b, t, 0))],
        out_shape=[aug, aug, aug, kv, kv, jax.ShapeDtypeStruct((B, S, LANES), F32)],
        scratch_shapes=[pltpu.VMEM((8, LANES), F32)],
        input_output_aliases=aliases,
        compiler_params=_cparams(("parallel", "arbitrary")),
        name="foxprep",
    )(*args)


def _flash_kernel(qa_ref, ka_ref, va_ref, z_ref, o_ref, m_ref, acc_ref):
    tq = qa_ref.shape[1]
    tk = ka_ref.shape[1]
    HD = LANES // 2
    i = pl.program_id(2)
    j = pl.program_id(3)

    @pl.when(j == 0)
    def _():
        m_ref[...] = jnp.full(m_ref.shape, -jnp.inf, F32)
        acc_ref[...] = jnp.zeros_like(acc_ref)

    @pl.when(j <= i)
    def _():
        keep = (_iota2((tq, tk), 0) >= _iota2((tq, tk), 1)) | (j < i)
        for hh in range(2):
            s = lax.dot_general(qa_ref[hh], ka_ref[hh], (((1,), (1,)), ((), ())),
                                preferred_element_type=F32)
            s = jnp.where(keep, s, -jnp.inf)
            m_prev = m_ref[hh]
            m_new = jnp.maximum(m_prev, jnp.max(s, axis=1, keepdims=True))
            alpha = jnp.exp(m_prev - m_new)
            p = jnp.exp(s - m_new[:, 0:1])
            acc_ref[hh] = acc_ref[hh] * alpha + jnp.dot(p.astype(BF16), va_ref[hh],
                                                        preferred_element_type=F32)
            m_ref[hh] = m_new

    @pl.when(j == pl.num_programs(3) - 1)
    def _():
        lane = _iota2((tq, LANES), 1)
        a0 = acc_ref[0]
        a1 = acc_ref[1]
        o0 = a0 / a0[:, HD:HD + 1]
        o1 = a1 / a1[:, HD:HD + 1]
        o = jnp.where(lane < HD, o0, pltpu.roll(o1, HD, 1))
        o_ref[...] = (o * _silu(z_ref[...])).astype(o_ref.dtype)


def _flash(qa, ka, va, proj, zcol, tq):
    B, H, S, _ = qa.shape
    n = S // tq
    return pl.pallas_call(
        _flash_kernel,
        grid=(B, H // 2, n, n),
        in_specs=[pl.BlockSpec((None, 2, tq, LANES), lambda b, p, i, j: (b, p, i, 0)),
                  pl.BlockSpec((None, 2, tq, LANES), lambda b, p, i, j: (b, p, jnp.minimum(i, j), 0)),
                  pl.BlockSpec((None, 2, tq, LANES), lambda b, p, i, j: (b, p, jnp.minimum(i, j), 0)),
                  pl.BlockSpec((None, tq, LANES), lambda b, p, i, j: (b, i, zcol + p))],
        out_specs=pl.BlockSpec((None, tq, LANES), lambda b, p, i, j: (b, i, p)),
        out_shape=jax.ShapeDtypeStruct((B, S, H * LANES // 2), BF16),
        scratch_shapes=[pltpu.VMEM((2, tq, LANES), F32), pltpu.VMEM((2, tq, LANES), F32)],
        compiler_params=_cparams(("parallel", "parallel", "parallel", "arbitrary")),
        name="flash",
    )(qa, ka, va, proj)


def _cumsum_kernel(x_ref, o_ref, *, group):
    R = x_ref.shape[0]
    x = x_ref[...]
    up = jnp.where(_iota2((LANES, LANES), 0) <= _iota2((LANES, LANES), 1), 1.0, 0.0).astype(BF16)
    w = _dot_sel_r(x, up)
    tot = jnp.broadcast_to(w[:, LANES - 1:LANES], (R, LANES))
    ri = _iota2((R, R), 0)
    ci = _iota2((R, R), 1)
    prev = jnp.where((ci < ri) & ((ci // group) == (ri // group)), 1.0, 0.0).astype(BF16)
    o_ref[...] = w + _dot_sel_l(prev, tot)


def _cumsum_rows(x, group):
    R = x.shape[0]
    rb = min(R, 256)
    return pl.pallas_call(
        functools.partial(_cumsum_kernel, group=group),
        grid=(R // rb,),
        in_specs=[pl.BlockSpec((rb, LANES), lambda r: (r, 0))],
        out_specs=pl.BlockSpec((rb, LANES), lambda r: (r, 0)),
        out_shape=jax.ShapeDtypeStruct((R, LANES), F32),
        compiler_params=_cparams(("parallel",)),
        name="cumsum",
    )(x)


def _foxsamp_kernel(q_ref, k_ref, v_ref, z_ref, sm_ref, fb_ref, tot_ref, fcc_ref, ck_ref, cv_ref,
                    o_ref, ko_ref, vo_ref, lf_ref):
    S = q_ref.shape[0]
    HD = ck_ref.shape[-1]
    p = pl.program_id(1)
    logf = _logf_from_small(sm_ref[...], fb_ref[...])
    lf_ref[...] = logf
    ltri = jnp.where(_iota2((S, S), 0) >= _iota2((S, S), 1), 1.0, 0.0).astype(BF16)
    fc = _dot_sel_l(ltri, logf) + tot_ref[...]
    fct = jnp.concatenate([fc, jnp.zeros((LANES - S, LANES), F32)], axis=0).T
    lane = _iota2((S, LANES), 1)
    row = _iota2((LANES, LANES), 0)
    causal = _iota2((S, S), 0) >= _iota2((S, S), 1)
    scale = HD ** -0.5
    for hh in range(2):
        h = 2 * p + hh
        fq = jnp.sum(jnp.where(lane == h, fc, 0.0), axis=1, keepdims=True)
        fkn = jnp.sum(jnp.where(row == h, fct, 0.0), axis=0, keepdims=True)[:, :S]
        q = q_ref[:, hh * HD:(hh + 1) * HD] * scale
        kn = k_ref[:, hh * HD:(hh + 1) * HD]
        vn = v_ref[:, hh * HD:(hh + 1) * HD]
        ko_ref[hh] = kn
        vo_ref[hh] = vn
        sc = _dot_nt(q, ck_ref[hh]) + fq - fcc_ref[hh]
        sn = jnp.where(causal, _dot_nt(q, kn) + fq - fkn, -jnp.inf)
        m = jnp.maximum(jnp.max(sc, axis=1, keepdims=True), jnp.max(sn, axis=1, keepdims=True))
        pc = jnp.exp(sc - m)
        pn = jnp.exp(sn - m)
        l = jnp.sum(pc, axis=1, keepdims=True) + jnp.sum(pn, axis=1, keepdims=True)
        o = (_dot(pc, cv_ref[hh]) + _dot(pn, vn)) / l
        o_ref[:, hh * HD:(hh + 1) * HD] = o * _silu(z_ref[:, hh * HD:(hh + 1) * HD])


def _foxsamp(proj, small, cols, fb_row, tot, fcc, cache_k, cache_v, l):
    B, S, _ = proj.shape
    _, _, H, P, HD = cache_k.shape
    cq, ck, cv, cz = cols
    pair = lambda c: pl.BlockSpec((None, S, LANES), lambda b, p: (b, 0, c + p))
    kv_out = jax.ShapeDtypeStruct((B, H, S, HD), F32)
    return pl.pallas_call(
        _foxsamp_kernel,
        grid=(B, H // 2),
        in_specs=[pair(cq), pair(ck), pair(cv), pair(cz),
                  pl.BlockSpec((None, S, LANES), lambda b, p: (b, 0, 0)),
                  pl.BlockSpec((1, LANES), lambda b, p: (0, 0)),
                  pl.BlockSpec((None, 1, LANES), lambda b, p: (b, 0, 0)),
                  pl.BlockSpec((None, 2, 1, P), lambda b, p: (b, p, 0, 0)),
                  pl.BlockSpec((None, None, 2, P, HD), lambda b, p: (l, b, p, 0, 0)),
                  pl.BlockSpec((None, None, 2, P, HD), lambda b, p: (l, b, p, 0, 0))],
        out_specs=[pl.BlockSpec((None, S, LANES), lambda b, p: (b, 0, p)),
                   pl.BlockSpec((None, 2, S, HD), lambda b, p: (b, p, 0, 0)),
                   pl.BlockSpec((None, 2, S, HD), lambda b, p: (b, p, 0, 0)),
                   pl.BlockSpec((None, S, LANES), lambda b, p: (b, 0, 0))],
        out_shape=[jax.ShapeDtypeStruct((B, S, H * HD), F32), kv_out, kv_out,
                   jax.ShapeDtypeStruct((B, S, LANES), F32)],
        compiler_params=_cparams(("parallel", "arbitrary")),
        name="foxsamp",
    )(proj, proj, proj, proj, small, fb_row, tot, fcc, cache_k, cache_v)


COL_MG = 0
COL_GQKV = 6
COL_FQKV = 9
COL_RQK = 12
COL_RV = 13
COL_RZ = 14
COL_FZ = 15
COL_GZ = 16


def _prep_w_in(w_in, D):
    BW = D // 2
    hk = BW // 2
    splits = (hk, hk, BW, BW, BW, BW, BW, H_FOX, BW, BW, BW, BW, H_GDN, H_GDN, BW, N_BRANCH * D)
    offs = np.cumsum((0,) + splits)
    seg = lambda i: w_in[..., offs[i]:offs[i + 1]]
    rq, rk, rv, rz, fq, fk, fv, ff, fz, gq, gk, gv, ga, gb, gz, mg = [seg(i) for i in range(16)]
    main = jnp.concatenate([mg, gq, gk, gv, fq, fk, fv, rq, rk, rv, rz, fz, gz], axis=-1)
    pad = jnp.zeros(w_in.shape[:-1] + (LANES - H_FOX - 2 * H_GDN,), w_in.dtype)
    small = jnp.concatenate([ff, ga, gb, pad], axis=-1)
    return main.astype(BF16), small.astype(BF16)


def _lane_row(vals, off):
    L, n = vals.shape
    return jnp.zeros((L, 1, LANES), F32).at[:, 0, off:off + n].set(vals)


def kernel(x_prompt, x_sample, c_prompt, c_sample, cache_fox_k, cache_fox_v, cache_fox_logf,
           state_ret, state_gdn, state_gdn_conv, norm_w, ada_w, ada_b, w_in, fox_f_bias,
           gdn_a_log, gdn_dt_bias, gdn_conv_w, ret_norm_w, gdn_norm_w, w_branch, w_out,
           final_norm_w):
    B, S, D = x_prompt.shape
    BS, SS, _ = x_sample.shape
    L = ada_w.shape[0]
    P = cache_fox_k.shape[3]
    DK_RET, DV_RET = state_ret.shape[-2:]
    DK_GDN, DV_GDN = state_gdn.shape[-2:]

    rows = -(-(B + BS) // 8) * 8
    c_all = jnp.concatenate([c_prompt, c_sample, jnp.zeros((rows - B - BS, D), F32)], axis=0)
    mod = _adaln(c_all, ada_w, ada_b)
    mod_p = mod[:, :, :B].reshape(L, 3, B, 1, D)
    mod_s = mod[:, :, B:B + BS].reshape(L, 3, BS, 1, D)

    w_main, w_small = _prep_w_in(w_in, D)
    w_branch_b = w_branch.astype(BF16)
    w_out_b = w_out.astype(BF16)
    norm_w3 = norm_w.reshape(L, 1, D)
    final_w = final_norm_w.reshape(1, D)
    ret_nw = ret_norm_w.reshape(L, H_RET, 1, DV_RET)
    gdn_nw = gdn_norm_w.reshape(L, 1, DV_GDN)
    fb_rows = _lane_row(fox_f_bias, 0)
    a_rows = _lane_row(gdn_a_log, H_FOX)
    dt_rows = _lane_row(gdn_dt_bias, H_FOX)

    tm_p = min(S, 1024)
    tmm_p = min(S, 512)
    t_ret = min(S, 256)
    t_gdn = min(S, 256)
    t_prep = min(S, 256)
    t_flash = min(S, 512)

    tabs_p = _ret_tables(t_ret, 0, S, DK_RET)
    tabs_s = _ret_tables(SS, P, SS, DK_RET)

    fcc = _cumsum_rows(cache_fox_logf.reshape(-1, LANES), P // LANES).reshape(L, BS, H_FOX, 1, P)
    tot = jnp.zeros((L, BS, 1, LANES), F32).at[:, :, 0, :H_FOX].set(fcc[:, :, :, 0, P - 1])

    zeros_ret = jnp.zeros((B, H_RET, DK_RET, DV_RET), F32)
    zeros_gdn = jnp.zeros((B, H_GDN, DK_GDN, DV_GDN), F32)
    zeros_conv = jnp.zeros((B, CONV_W - 1, gdn_conv_w.shape[-1]), F32)

    xp, xs = x_prompt, x_sample
    pk = pv = None
    p_lf, p_ret, p_gdn, p_conv = [], [], [], []
    s_k, s_v, s_lf, s_ret, s_gdn, s_conv = [], [], [], [], [], []
    for l in range(L):
        final = l == L - 1
        proj, small = _inproj(xp, mod_p, l, norm_w3, w_main, w_small, 1, tm_p)
        o_ret, st = _ret(proj, (COL_RQK, COL_RV, COL_RZ), tabs_p, ret_nw[l], zeros_ret, t_ret, BF16)
        p_ret.append(st)
        o_gdn, st, cv = _gdn(proj, small, (COL_GQKV // 3, COL_GZ), gdn_conv_w[l], zeros_conv,
                             a_rows[l], dt_rows[l], gdn_nw[l], zeros_gdn, t_gdn, BF16)
        p_gdn.append(st)
        p_conv.append(cv)
        qa, ka, va, pk, pv, lf = _foxprep(proj, small, COL_FQKV // 3, fb_rows[l], l, pk, pv, L, t_prep)
        p_lf.append(lf)
        o_fox = _flash(qa, ka, va, proj, COL_FZ * 4, t_flash)
        xp = _merge(xp, mod_p, l, o_ret, o_fox, o_gdn, proj, w_branch_b, w_out_b, final_w,
                    1, tmm_p, final)
        proj, small = _inproj(xs, mod_s, l, norm_w3, w_main, w_small, BS, SS)
        o_ret, st = _ret(proj, (COL_RQK, COL_RV, COL_RZ), tabs_s, ret_nw[l], state_ret[l], SS, F32)
        s_ret.append(st)
        o_gdn, st, cv = _gdn(proj, small, (COL_GQKV // 3, COL_GZ), gdn_conv_w[l], state_gdn_conv[l],
                             a_rows[l], dt_rows[l], gdn_nw[l], state_gdn[l], SS, F32)
        s_gdn.append(st)
        s_conv.append(cv)
        o_fox, kk, vv, lf = _foxsamp(proj, small,
                                     (COL_FQKV * 4, (COL_FQKV + 1) * 4, (COL_FQKV + 2) * 4, COL_FZ * 4),
                                     fb_rows[l], tot[l], fcc[l], cache_fox_k, cache_fox_v, l)
        s_k.append(kk)
        s_v.append(vv)
        s_lf.append(lf)
        xs = _merge(xs, mod_s, l, o_ret, o_fox, o_gdn, proj, w_branch_b, w_out_b, final_w,
                    BS, SS, final)

    logf_out = lambda lfs: jnp.stack(lfs)[..., :H_FOX].transpose(0, 1, 3, 2)
    return (xp, xs,
            pk, pv, logf_out(p_lf), jnp.stack(p_ret), jnp.stack(p_gdn), jnp.stack(p_conv),
            jnp.stack(s_k), jnp.stack(s_v), logf_out(s_lf), jnp.stack(s_ret), jnp.stack(s_gdn),
            jnp.stack(s_conv))
```

```python
import functools

import numpy as np
import jax
import jax.numpy as jnp
from jax import lax
from jax.experimental import pallas as pl
from jax.experimental.pallas import tpu as pltpu

F32 = jnp.float32
BF16 = jnp.bfloat16

N_BRANCH = 3
H_RET = 4
H_FOX = 8
H_GDN = 4
CONV_W = 4
CHUNK = 64
NORM_EPS = 1e-6
ROPE_BASE = 10000.0
LOG2E = 1.4426950408889634
LANES = 128
GDN_CHUNK = 128
VMEM_LIMIT = 56 * 1024 * 1024


def _cparams(sem):
    return pltpu.CompilerParams(dimension_semantics=sem, vmem_limit_bytes=VMEM_LIMIT)


def _sigmoid(x):
    return 1.0 / (1.0 + jnp.exp(-x))


def _silu(x):
    return x * _sigmoid(x)


def _softplus(x):
    return jnp.maximum(x, 0.0) + jnp.log(1.0 + jnp.exp(-jnp.abs(x)))


def _dot(a, b):
    return jnp.dot(a.astype(BF16), b.astype(BF16), preferred_element_type=F32)


def _dot_nt(a, b):
    return lax.dot_general(a.astype(BF16), b.astype(BF16), (((1,), (1,)), ((), ())),
                           preferred_element_type=F32)


def _split2(x):
    hi = x.astype(BF16)
    lo = (x - hi.astype(F32)).astype(BF16)
    return hi, lo


def _split3(x):
    hi = x.astype(BF16)
    r = x - hi.astype(F32)
    mid = r.astype(BF16)
    lo = (r - mid.astype(F32)).astype(BF16)
    return hi, mid, lo


def _dot_hi(a, b):
    ah, al = _split2(a)
    bh, bl = _split2(b)
    d = lambda x, y: jnp.dot(x, y, preferred_element_type=F32)
    return d(ah, bh) + (d(ah, bl) + d(al, bh))


def _dot_sel_l(m, x):
    hi, mid, lo = _split3(x)
    d = lambda y: jnp.dot(m, y, preferred_element_type=F32)
    return d(hi) + (d(mid) + d(lo))


def _dot_sel_r(x, m):
    hi, mid, lo = _split3(x)
    d = lambda y: jnp.dot(y, m, preferred_element_type=F32)
    return d(hi) + (d(mid) + d(lo))


def _iota2(shape, dim):
    return lax.broadcasted_iota(jnp.int32, shape, dim)


def _adaln_kernel(c_ref, w_ref, b_ref, o_ref):
    s = _silu(c_ref[...])
    o_ref[...] = _dot(s, w_ref[...]) + b_ref[...]


def _adaln(c_all, ada_w, ada_b):
    L, D, _ = ada_w.shape
    R = c_all.shape[0]
    return pl.pallas_call(
        _adaln_kernel,
        grid=(L, 3),
        in_specs=[pl.BlockSpec((R, D), lambda l, k: (0, 0)),
                  pl.BlockSpec((None, D, D), lambda l, k: (l, 0, k)),
                  pl.BlockSpec((None, None, 1, D), lambda l, k: (l, k, 0, 0))],
        out_specs=pl.BlockSpec((None, None, R, D), lambda l, k: (l, k, 0, 0)),
        out_shape=jax.ShapeDtypeStruct((L, 3, R, D), F32),
        compiler_params=_cparams(("parallel", "parallel")),
        name="adaln",
    )(c_all, ada_w, ada_b.reshape(L, 3, 1, D))


PROJ_TN = 512


def _inproj_kernel(x_ref, nw_ref, sc_ref, sh_ref, w_ref, ws_ref, o_ref, os_ref, h_ref):
    bb, tm, D = x_ref.shape

    @pl.when(pl.program_id(2) == 0)
    def _():
        x = x_ref[...]
        ms = jnp.mean(x * x, axis=-1, keepdims=True)
        xn = (x * lax.rsqrt(ms + NORM_EPS)) * nw_ref[...]
        h = xn * (1.0 + sc_ref[...]) + sh_ref[...]
        hb = h.astype(BF16).reshape(bb * tm, D)
        h_ref[...] = hb
        os_ref[...] = jnp.dot(hb, ws_ref[...], preferred_element_type=F32).reshape(bb, tm, LANES)

    o_ref[...] = jnp.dot(h_ref[...], w_ref[...],
                         preferred_element_type=F32).reshape(bb, tm, PROJ_TN)


def _inproj(x, mod, l, norm_w, w_main, w_small, bb, tm):
    B, S, D = x.shape
    NP = w_main.shape[-1]
    nj = NP // PROJ_TN
    return pl.pallas_call(
        _inproj_kernel,
        grid=(B // bb, S // tm, nj),
        in_specs=[pl.BlockSpec((bb, tm, D), lambda b, i, j: (b, i, 0)),
                  pl.BlockSpec((None, 1, D), lambda b, i, j: (l, 0, 0)),
                  pl.BlockSpec((None, None, bb, 1, D), lambda b, i, j: (l, 1, b, 0, 0)),
                  pl.BlockSpec((None, None, bb, 1, D), lambda b, i, j: (l, 0, b, 0, 0)),
                  pl.BlockSpec((None, D, PROJ_TN), lambda b, i, j: (l, 0, j)),
                  pl.BlockSpec((None, D, LANES), lambda b, i, j: (l, 0, 0))],
        out_specs=[pl.BlockSpec((bb, tm, PROJ_TN), lambda b, i, j: (b, i, j)),
                   pl.BlockSpec((bb, tm, LANES), lambda b, i, j: (b, i, 0))],
        out_shape=[jax.ShapeDtypeStruct((B, S, NP), F32),
                   jax.ShapeDtypeStruct((B, S, LANES), F32)],
        scratch_shapes=[pltpu.VMEM((bb * tm, D), BF16)],
        compiler_params=_cparams(("parallel", "parallel", "arbitrary")),
        name="inproj",
    )(x, norm_w, mod, mod, w_main, w_small)


def _merge_kernel(x_ref, g_ref, o1_ref, o2_ref, o3_ref, mg_ref, wb_ref, wo_ref, fw_ref, y_ref,
                  *, final):
    bb, tm, D = x_ref.shape
    M = bb * tm
    acc = None
    for n, o_ref in enumerate((o1_ref, o2_ref, o3_ref)):
        br = jnp.dot(o_ref[...].reshape(M, o_ref.shape[-1]).astype(BF16), wb_ref[n],
                     preferred_element_type=F32)
        gate = _sigmoid(mg_ref[:, :, n * D:(n + 1) * D].reshape(M, D))
        term = br * gate
        acc = term if acc is None else acc + term
    out = jnp.dot(acc.astype(BF16), wo_ref[...], preferred_element_type=F32)
    xn = x_ref[...] + g_ref[...] * out.reshape(bb, tm, D)
    if final:
        ms = jnp.mean(xn * xn, axis=-1, keepdims=True)
        xn = (xn * lax.rsqrt(ms + NORM_EPS)) * fw_ref[...]
    y_ref[...] = xn


def _merge(x, mod, l, o_ret, o_fox, o_gdn, proj, w_branch, w_out, final_w, bb, tm, final):
    B, S, D = x.shape
    W = o_ret.shape[-1]
    return pl.pallas_call(
        functools.partial(_merge_kernel, final=final),
        grid=(B // bb, S // tm),
        in_specs=[pl.BlockSpec((bb, tm, D), lambda b, i: (b, i, 0)),
                  pl.BlockSpec((None, None, bb, 1, D), lambda b, i: (l, 2, b, 0, 0)),
                  pl.BlockSpec((bb, tm, W), lambda b, i: (b, i, 0)),
                  pl.BlockSpec((bb, tm, W), lambda b, i: (b, i, 0)),
                  pl.BlockSpec((bb, tm, W), lambda b, i: (b, i, 0)),
                  pl.BlockSpec((bb, tm, N_BRANCH * D), lambda b, i: (b, i, 0)),
                  pl.BlockSpec((None, N_BRANCH, W, D), lambda b, i: (l, 0, 0, 0)),
                  pl.BlockSpec((None, D, D), lambda b, i: (l, 0, 0)),
                  pl.BlockSpec((1, D), lambda b, i: (0, 0))],
        out_specs=pl.BlockSpec((bb, tm, D), lambda b, i: (b, i, 0)),
        out_shape=jax.ShapeDtypeStruct((B, S, D), F32),
        compiler_params=_cparams(("parallel", "parallel")),
        name="merge",
    )(x, mod, o_ret, o_fox, o_gdn, proj, w_branch, w_out, final_w)


def _ret_kernel(qk_ref, v_ref, z_ref, cos_ref, sin_ref, m_ref, dq_ref, dk_ref, dc_ref, nw_ref,
                s0_ref, o_ref, so_ref, s_ref):
    T = qk_ref.shape[0]
    H = s_ref.shape[0]
    DK = s0_ref.shape[1]
    t = pl.program_id(1)

    @pl.when(t == 0)
    def _():
        zero = jnp.zeros((DK, LANES), F32)
        for h in range(H):
            s0 = s0_ref[h]
            s_ref[h] = jnp.concatenate([s0, zero] if h % 2 == 0 else [zero, s0], axis=0)

    cos = cos_ref[...]
    sin = sin_ref[...]
    lane = _iota2((T, LANES), 1)
    first = (lane % DK) < (DK // 2)
    low = lane < DK

    def rot(x):
        sw = jnp.where(first, pltpu.roll(x, LANES - DK // 2, 1), pltpu.roll(x, DK // 2, 1))
        return x * cos + sw * sin

    nkq = H * DK
    for p in range(H // 2):
        qc = rot(qk_ref[:, p * LANES:(p + 1) * LANES])
        kc = rot(qk_ref[:, nkq + p * LANES:nkq + (p + 1) * LANES]) * (DK ** -0.5)
        kcb = kc.astype(BF16)
        for half in range(2):
            h = 2 * p + half
            sel = low if half == 0 else jnp.logical_not(low)
            qm = jnp.where(sel, qc, 0.0)
            a = _dot_nt(qm, kcb) * m_ref[h]
            vh = v_ref[:, h * LANES:(h + 1) * LANES].astype(BF16)
            s = s_ref[h]
            o = _dot(a, vh) + _dot(qm * dq_ref[h], s)
            km = jnp.where(sel, kc, 0.0) * dk_ref[h]
            s_ref[h] = s * dc_ref[h] + _dot(km.T, vh)
            ms = jnp.mean(o * o, axis=-1, keepdims=True)
            on = (o * lax.rsqrt(ms + NORM_EPS)) * nw_ref[h]
            z = z_ref[:, h * LANES:(h + 1) * LANES]
            o_ref[:, h * LANES:(h + 1) * LANES] = (on * _silu(z)).astype(o_ref.dtype)

    @pl.when(t == pl.num_programs(1) - 1)
    def _():
        for h in range(H):
            so_ref[h] = s_ref[h][(h % 2) * DK:(h % 2 + 1) * DK, :]


def _ret_tables(T, pos0, S, DK):
    half = DK // 2
    inv_freq = ROPE_BASE ** (-jnp.arange(half, dtype=F32) / half)
    pos = pos0 + jnp.arange(S)
    ang = pos.astype(F32)[:, None] * inv_freq[None, :]
    cos, sin = jnp.cos(ang), jnp.sin(ang)
    reps = LANES // DK
    cos_t = jnp.tile(jnp.concatenate([cos, cos], axis=1), (1, reps))
    sin_t = jnp.tile(jnp.concatenate([-sin, sin], axis=1), (1, reps))
    log_g = jnp.log1p(-jnp.exp2(-5.0 - jnp.arange(H_RET, dtype=F32)))
    i = jnp.arange(T)
    dist = jnp.abs(i[:, None] - i[None, :]).astype(F32)
    allowed = (i[None, :] // CHUNK) <= (i[:, None] // CHUNK)
    m = jnp.where(allowed[None], jnp.exp(dist[None] * log_g[:, None, None]), 0.0)
    fi = i.astype(F32)
    dq = jnp.exp((fi[None, :] + 1.0) * log_g[:, None])
    dk = jnp.exp((T - 1.0 - fi)[None, :] * log_g[:, None])
    dc = jnp.exp(T * log_g)
    bc = lambda a: jnp.broadcast_to(a[..., None], a.shape + (LANES,))
    return cos_t, sin_t, m, bc(dq), bc(dk), bc(dc[:, None])


def _ret(proj, cols, tabs, ret_nw, s0, T, out_dtype):
    B, S, _ = proj.shape
    H, DK, DV = s0.shape[1:]
    W = H * DV
    cqk, cv, cz = cols
    cos_t, sin_t, m, dq, dk, dc = tabs
    return pl.pallas_call(
        _ret_kernel,
        grid=(B, S // T),
        in_specs=[pl.BlockSpec((None, T, W), lambda b, t: (b, t, cqk)),
                  pl.BlockSpec((None, T, W), lambda b, t: (b, t, cv)),
                  pl.BlockSpec((None, T, W), lambda b, t: (b, t, cz)),
                  pl.BlockSpec((T, LANES), lambda b, t: (t, 0)),
                  pl.BlockSpec((T, LANES), lambda b, t: (t, 0)),
                  pl.BlockSpec((H, T, T), lambda b, t: (0, 0, 0)),
                  pl.BlockSpec((H, T, LANES), lambda b, t: (0, 0, 0)),
                  pl.BlockSpec((H, T, LANES), lambda b, t: (0, 0, 0)),
                  pl.BlockSpec((H, 1, LANES), lambda b, t: (0, 0, 0)),
                  pl.BlockSpec((H, 1, DV), lambda b, t: (0, 0, 0)),
                  pl.BlockSpec((None, H, DK, DV), lambda b, t: (b, 0, 0, 0))],
        out_specs=[pl.BlockSpec((None, T, W), lambda b, t: (b, t, 0)),
                   pl.BlockSpec((None, H, DK, DV), lambda b, t: (b, 0, 0, 0))],
        out_shape=[jax.ShapeDtypeStruct((B, S, W), out_dtype),
                   jax.ShapeDtypeStruct((B, H, DK, DV), F32)],
        scratch_shapes=[pltpu.VMEM((H, LANES, DV), F32)],
        compiler_params=_cparams(("parallel", "arbitrary")),
        name="ret",
    )(proj, proj, proj, cos_t, sin_t, m, dq, dk, dc, ret_nw, s0)


INV_BASE = 8


def _mm3(a, b):
    d = lambda x, y: jnp.dot(x, y, preferred_element_type=F32)
    return d(a[0], b[0]) + (d(a[0], b[1]) + d(a[1], b[0]))


def _unit_lower_inverses(mats, eye, ri, ci):
    n = mats[0].shape[0]
    b = INV_BASE
    same = (ri // b) == (ci // b)
    ads = [jnp.where(same, a, 0.0) for a in mats]
    xs = [eye - ad for ad in ads]
    ps = [_split2(ad) for ad in ads]
    k = 2
    while k < b:
        pf = [_mm3(p, p) for p in ps]
        ps = [_split2(p) for p in pf]
        xs = [x + _mm3(_split2(x), p) for x, p in zip(xs, ps)]
        k *= 2
    while b < n:
        off = ((ri // (2 * b)) == (ci // (2 * b))) & ((ri // b) != (ci // b))
        xsp = [_split2(x) for x in xs]
        ts = [_mm3(_split2(jnp.where(off, a, 0.0)), x) for a, x in zip(mats, xsp)]
        xs = [x - _mm3(xp, _split2(t)) for x, xp, t in zip(xs, xsp, ts)]
        b *= 2
    return xs


def _gdn_kernel(u_ref, z_ref, sm_ref, cw_ref, cb_ref, al_ref, dt_ref, nw_ref, s0_ref,
                o_ref, so_ref, co_ref, ext_ref, s_ref):
    T, C3 = u_ref.shape
    H = s_ref.shape[0]
    DK = s_ref.shape[1]
    W = H * DK
    C = GDN_CHUNK
    TP = -(-T // C) * C
    t = pl.program_id(1)
    last = pl.num_programs(1) - 1
    base = 8
    lo = base - (CONV_W - 1)

    @pl.when(t == 0)
    def _():
        s_ref[...] = s0_ref[...]
        ext_ref[lo:base, :] = cb_ref[...]

    ext_ref[base:base + T, :] = u_ref[...]
    y = ext_ref[lo:lo + T, :] * cw_ref[0:1, :]
    for j in range(1, CONV_W):
        y = y + ext_ref[lo + j:lo + j + T, :] * cw_ref[j:j + 1, :]
    tail = ext_ref[lo + T:base + T, :]
    ext_ref[lo:base, :] = tail

    @pl.when(t == last)
    def _():
        co_ref[...] = tail

    qkv = _silu(y)
    sm = sm_ref[...]
    g_all = -jnp.exp(al_ref[...]) * _softplus(sm + dt_ref[...])
    b_all = _sigmoid(sm)
    z_all = z_ref[...]
    if TP > T:
        pad = lambda a: jnp.concatenate([a, jnp.zeros((TP - T, a.shape[1]), F32)], axis=0)
        qkv, g_all, b_all, z_all = pad(qkv), pad(g_all), pad(b_all), pad(z_all)

    ri = _iota2((C, C), 0)
    ci = _iota2((C, C), 1)
    tri = ri >= ci
    strict = ri > ci
    ltri = jnp.where(tri, 1.0, 0.0).astype(BF16)
    eye = jnp.where(ri == ci, 1.0, 0.0)

    nc = TP // C
    items = [(c, h) for c in range(nc) for h in range(H)]
    gcums = [_dot_sel_l(ltri, g_all[c * C:(c + 1) * C, :]) for c in range(nc)]

    qs, ks, vs, gcs, bcs, decs, kbs = [], [], [], [], [], [], []
    for c, h in items:
        r0 = c * C
        q = qkv[r0:r0 + C, h * DK:(h + 1) * DK]
        k = qkv[r0:r0 + C, W + h * DK:W + (h + 1) * DK]
        qs.append(q * lax.rsqrt(jnp.sum(q * q, axis=-1, keepdims=True) + NORM_EPS) * (DK ** -0.5))
        k = k * lax.rsqrt(jnp.sum(k * k, axis=-1, keepdims=True) + NORM_EPS)
        ks.append(k)
        kbs.append(k.astype(BF16))
        vs.append(qkv[r0:r0 + C, 2 * W + h * DK:2 * W + (h + 1) * DK])
        gc = gcums[c][:, 8 + h:9 + h]
        gcs.append(gc)
        bcs.append(b_all[r0:r0 + C, 12 + h:13 + h])
        gm = jnp.broadcast_to(gc, (C, C))
        decs.append(jnp.exp(jnp.where(tri, gm - gm.T, -jnp.inf)))
    kks = [_dot_nt(kb, kb) for kb in kbs]
    amats = [jnp.where(strict, kk * dec, 0.0) * bc for kk, dec, bc in zip(kks, decs, bcs)]
    tinvs = _unit_lower_inverses(amats, eye, ri, ci)
    egs = [jnp.exp(gc) for gc in gcs]
    rhss = [_split2(jnp.concatenate([k * (bc * eg), v * bc], axis=1))
            for k, v, bc, eg in zip(ks, vs, bcs, egs)]
    wus = [_mm3(_split2(tinv), rhs) for tinv, rhs in zip(tinvs, rhss)]
    qks = [(_dot_nt(q, kb) * dec).astype(BF16) for q, kb, dec in zip(qs, kbs, decs)]
    qes = [(q * eg).astype(BF16) for q, eg in zip(qs, egs)]
    glasts = [gcums[c][C - 1:C, 8 + h:9 + h] for c, h in items]
    kdts = [(k * jnp.exp(gl - gc)).T.astype(BF16) for k, gl, gc in zip(ks, glasts, gcs)]
    sdec = [jnp.exp(gl) for gl in glasts]

    for c in range(nc):
        idx = [c * H + h for h in range(H)]
        ss = [s_ref[h] for h in range(H)]
        sbs = [s.astype(BF16) for s in ss]
        us = [wus[i][:, DK:] - _dot(wus[i][:, :DK], sb) for i, sb in zip(idx, sbs)]
        ubs = [u.astype(BF16) for u in us]
        os_ = [jnp.dot(qes[i], sb, preferred_element_type=F32)
               + jnp.dot(qks[i], ub, preferred_element_type=F32)
               for i, sb, ub in zip(idx, sbs, ubs)]
        for h, i in enumerate(idx):
            s_ref[h] = ss[h] * sdec[i] + jnp.dot(kdts[i], ubs[h], preferred_element_type=F32)
        for h, o in enumerate(os_):
            ms = jnp.mean(o * o, axis=-1, keepdims=True)
            on = (o * lax.rsqrt(ms + NORM_EPS)) * nw_ref[...]
            z = z_all[c * C:(c + 1) * C, h * DK:(h + 1) * DK]
            res = (on * _silu(z)).astype(o_ref.dtype)
            if TP > T:
                o_ref[:, h * DK:(h + 1) * DK] = res[:T, :]
            else:
                o_ref[c * C:(c + 1) * C, h * DK:(h + 1) * DK] = res

    @pl.when(t == last)
    def _():
        so_ref[...] = s_ref[...]


def _gdn(proj, small, cols, conv_w, conv_buf, a_row, dt_row, gdn_nw, s0, T, out_dtype):
    B, S, _ = proj.shape
    H, DK, DV = s0.shape[1:]
    W = H * DK
    C3 = conv_w.shape[-1]
    cu, cz = cols
    return pl.pallas_call(
        _gdn_kernel,
        grid=(B, S // T),
        in_specs=[pl.BlockSpec((None, T, C3), lambda b, t: (b, t, cu)),
                  pl.BlockSpec((None, T, W), lambda b, t: (b, t, cz)),
                  pl.BlockSpec((None, T, LANES), lambda b, t: (b, t, 0)),
                  pl.BlockSpec((CONV_W, C3), lambda b, t: (0, 0)),
                  pl.BlockSpec((None, CONV_W - 1, C3), lambda b, t: (b, 0, 0)),
                  pl.BlockSpec((1, LANES), lambda b, t: (0, 0)),
                  pl.BlockSpec((1, LANES), lambda b, t: (0, 0)),
                  pl.BlockSpec((1, DV), lambda b, t: (0, 0)),
                  pl.BlockSpec((None, H, DK, DV), lambda b, t: (b, 0, 0, 0))],
        out_specs=[pl.BlockSpec((None, T, W), lambda b, t: (b, t, 0)),
                   pl.BlockSpec((None, H, DK, DV), lambda b, t: (b, 0, 0, 0)),
                   pl.BlockSpec((None, CONV_W - 1, C3), lambda b, t: (b, 0, 0))],
        out_shape=[jax.ShapeDtypeStruct((B, S, W), out_dtype),
                   jax.ShapeDtypeStruct((B, H, DK, DV), F32),
                   jax.ShapeDtypeStruct((B, CONV_W - 1, C3), F32)],
        scratch_shapes=[pltpu.VMEM((T + 8, C3), F32),
                        pltpu.VMEM((H, DK, DV), F32)],
        compiler_params=_cparams(("parallel", "arbitrary")),
        name="gdn",
    )(proj, proj, small, conv_w, conv_buf, a_row, dt_row, gdn_nw, s0)


def _logf_from_small(sm, fb):
    lane = _iota2(sm.shape, 1)
    return jnp.where(lane < H_FOX, -_softplus(-(sm + fb)), 0.0)


def _foxprep_kernel(qkv_ref, sm_ref, fb_ref, qa_ref, ka_ref, vt_ref, ko_ref, vo_ref, lf_ref,
                    carry_ref):
    T = sm_ref.shape[0]
    H = qa_ref.shape[0]
    HD = ko_ref.shape[-1]
    W = H * HD
    t = pl.program_id(1)

    @pl.when(t == 0)
    def _():
        carry_ref[...] = jnp.zeros_like(carry_ref)

    logf = _logf_from_small(sm_ref[...], fb_ref[...])
    lf_ref[...] = logf
    ltri = jnp.where(_iota2((T, T), 0) >= _iota2((T, T), 1), 1.0, 0.0).astype(BF16)
    fc = _dot_sel_l(ltri, logf) + carry_ref[0:1, :]
    carry_ref[0:1, :] = fc[T - 1:T, :]
    f_hi, f_mid, f_lo = [p.astype(F32) for p in _split3(fc * LOG2E)]

    lane = _iota2((T, LANES), 1)
    low = lane < HD
    one_q = jnp.where((lane >= HD + 3) & (lane < HD + 6), 1.0, 0.0)
    one_k = jnp.where((lane >= HD) & (lane < HD + 3), 1.0, 0.0)
    one_v = jnp.where(lane == HD, 1.0, 0.0)
    scale = (HD ** -0.5) * LOG2E
    for p in range(H // 2):
        qc = qkv_ref[:, p * LANES:(p + 1) * LANES] * scale
        kc = qkv_ref[:, W + p * LANES:W + (p + 1) * LANES]
        vc = qkv_ref[:, 2 * W + p * LANES:2 * W + (p + 1) * LANES]
        for half in range(2):
            h = 2 * p + half
            if half == 1:
                qh, kh, vh = [pltpu.roll(a, HD, 1) for a in (qc, kc, vc)]
            else:
                qh, kh, vh = qc, kc, vc
            c_hi, c_mid, c_lo = [a[:, h:h + 1] for a in (f_hi, f_mid, f_lo)]
            ex_q = jnp.where(lane == HD, c_hi,
                             jnp.where(lane == HD + 1, c_mid,
                                       jnp.where(lane == HD + 2, c_lo, one_q)))
            ex_k = jnp.where(lane == HD + 3, -c_hi,
                             jnp.where(lane == HD + 4, -c_mid,
                                       jnp.where(lane == HD + 5, -c_lo, one_k)))
            qa_ref[h] = jnp.where(low, qh, ex_q).astype(BF16)
            ka_ref[h] = jnp.where(low, kh, ex_k).astype(BF16)
            vt_ref[h] = jnp.where(low, vh, one_v).T.astype(BF16)
            ko_ref[h] = kh[:, :HD]
            vo_ref[h] = vh[:, :HD]


def _foxprep(proj, small, col, fb_row, l, k_prev, v_prev, L, T):
    B, S, _ = proj.shape
    H = H_FOX
    HD = 512 // H
    aug = jax.ShapeDtypeStruct((B, H, S, LANES), BF16)
    augt = jax.ShapeDtypeStruct((B, H, LANES, S), BF16)
    kv = jax.ShapeDtypeStruct((L, B, H, S, HD), F32)
    aug_spec = pl.BlockSpec((None, H, T, LANES), lambda b, t: (b, 0, t, 0))
    augt_spec = pl.BlockSpec((None, H, LANES, T), lambda b, t: (b, 0, 0, t))
    kv_spec = pl.BlockSpec((None, None, H, T, HD), lambda b, t: (l, b, 0, t, 0))
    in_specs = [pl.BlockSpec((None, T, 3 * H * HD), lambda b, t: (b, t, col)),
                pl.BlockSpec((None, T, LANES), lambda b, t: (b, t, 0)),
                pl.BlockSpec((1, LANES), lambda b, t: (0, 0))]
    args = [proj, small, fb_row]
    aliases = {}
    kernel = _foxprep_kernel
    if k_prev is not None:
        in_specs += [pl.BlockSpec(memory_space=pl.ANY), pl.BlockSpec(memory_space=pl.ANY)]
        args += [k_prev, v_prev]
        aliases = {3: 3, 4: 4}
        kernel = lambda a, b, c, _k, _v, *rest: _foxprep_kernel(a, b, c, *rest)
    return pl.pallas_call(
        kernel,
        grid=(B, S // T),
        in_specs=in_specs,
        out_specs=[aug_spec, aug_spec, augt_spec, kv_spec, kv_spec,
                   pl.BlockSpec((None, T, LANES), lambda b, t: (b, t, 0))],
        out_shape=[aug, aug, augt, kv, kv, jax.ShapeDtypeStruct((B, S, LANES), F32)],
        scratch_shapes=[pltpu.VMEM((8, LANES), F32)],
        input_output_aliases=aliases,
        compiler_params=_cparams(("parallel", "arbitrary")),
        name="foxprep",
    )(*args)


FLASH_QS = 256


def _flash_kernel(qi_ref, kj_ref, qa_ref, ka_ref, vt_ref, z_ref, o_ref,
                  s_scr, p_scr, m_scr, acc_scr):
    TQ = qa_ref.shape[1]
    TK = ka_ref.shape[1]
    QS = FLASH_QS
    HD = LANES // 2
    nqs = TQ // QS
    st = pl.program_id(2)
    i = qi_ref[st]
    j = kj_ref[st]
    units = [(qs, hh) for qs in range(nqs) for hh in range(2)]
    nt = (((1,), (1,)), ((), ()))

    @pl.when(j == 0)
    def _():
        m_scr[...] = jnp.full(m_scr.shape, -jnp.inf, F32)
        acc_scr[...] = jnp.zeros_like(acc_scr)

    def scores(n, nk, diag):
        qs, hh = units[n]
        slot = n % 2
        q = qa_ref[hh, qs * QS:(qs + 1) * QS, :]
        s = lax.dot_general(ka_ref[hh, 0:nk, :], q, nt, preferred_element_type=F32)
        if diag:
            keep = _iota2((QS, QS), 0) <= _iota2((QS, QS), 1)
            if nk > QS:
                s_scr[slot, 0:nk - QS, :] = s[0:nk - QS, :]
            s_scr[slot, nk - QS:nk, :] = jnp.where(keep, s[nk - QS:nk, :], -jnp.inf)
        else:
            s_scr[slot, 0:nk, :] = s

    def softmax_pv(n, nk):
        qs, hh = units[n]
        slot = n % 2
        s = s_scr[slot, 0:nk, :]
        m_prev = m_scr[hh, qs, 0:1, :]
        m_new = jnp.maximum(m_prev, jnp.max(s, axis=0, keepdims=True))
        alpha = jnp.exp2(m_prev - m_new)
        p_scr[slot, 0:nk, :] = jnp.exp2(s - m_new).astype(BF16)
        pv = jnp.dot(vt_ref[hh, :, 0:nk], p_scr[slot, 0:nk, :], preferred_element_type=F32)
        acc_scr[hh, qs] = acc_scr[hh, qs] * alpha + pv
        m_scr[hh, qs, 0:1, :] = m_new

    def sweep(nks, diag):
        scores(0, nks[0], diag)
        for n in range(len(units)):
            if n + 1 < len(units):
                scores(n + 1, nks[n + 1], diag)
            softmax_pv(n, nks[n])

    @pl.when(j < i)
    def _():
        sweep([TK] * len(units), False)

    @pl.when(j == i)
    def _():
        sweep([(qs + 1) * QS for qs, _ in units], True)
        lane = _iota2((QS, LANES), 1)
        for qs in range(nqs):
            a0 = acc_scr[0, qs]
            a1 = acc_scr[1, qs]
            o0 = (a0 / a0[HD:HD + 1, :]).T
            o1 = (a1 / a1[HD:HD + 1, :]).T
            o = jnp.where(lane < HD, o0, pltpu.roll(o1, HD, 1))
            z = z_ref[qs * QS:(qs + 1) * QS, :]
            o_ref[qs * QS:(qs + 1) * QS, :] = (o * _silu(z)).astype(o_ref.dtype)


def _flash(qa, ka, vt, proj, zcol, tq):
    B, H, S, _ = qa.shape
    n = S // tq
    steps = [(i, j) for i in range(n) for j in range(i + 1)]
    qi = jnp.asarray([s[0] for s in steps], jnp.int32)
    kj = jnp.asarray([s[1] for s in steps], jnp.int32)
    nqs = tq // FLASH_QS
    grid_spec = pltpu.PrefetchScalarGridSpec(
        num_scalar_prefetch=2,
        grid=(B, H // 2, len(steps)),
        in_specs=[pl.BlockSpec((None, 2, tq, LANES), lambda b, p, s, qi, kj: (b, p, qi[s], 0)),
                  pl.BlockSpec((None, 2, tq, LANES), lambda b, p, s, qi, kj: (b, p, kj[s], 0)),
                  pl.BlockSpec((None, 2, LANES, tq), lambda b, p, s, qi, kj: (b, p, 0, kj[s])),
                  pl.BlockSpec((None, tq, LANES), lambda b, p, s, qi, kj: (b, qi[s], zcol + p))],
        out_specs=pl.BlockSpec((None, tq, LANES), lambda b, p, s, qi, kj: (b, qi[s], p)),
        scratch_shapes=[pltpu.VMEM((2, tq, FLASH_QS), F32),
                        pltpu.VMEM((2, tq, FLASH_QS), BF16),
                        pltpu.VMEM((2, nqs, 8, FLASH_QS), F32),
                        pltpu.VMEM((2, nqs, LANES, FLASH_QS), F32)])
    return pl.pallas_call(
        _flash_kernel,
        grid_spec=grid_spec,
        out_shape=jax.ShapeDtypeStruct((B, S, H * LANES // 2), BF16),
        compiler_params=_cparams(("parallel", "parallel", "arbitrary")),
        name="flash",
    )(qi, kj, qa, ka, vt, proj)


def _cumsum_kernel(x_ref, o_ref, *, group):
    R = x_ref.shape[0]
    x = x_ref[...]
    up = jnp.where(_iota2((LANES, LANES), 0) <= _iota2((LANES, LANES), 1), 1.0, 0.0).astype(BF16)
    w = _dot_sel_r(x, up)
    tot = jnp.broadcast_to(w[:, LANES - 1:LANES], (R, LANES))
    ri = _iota2((R, R), 0)
    ci = _iota2((R, R), 1)
    prev = jnp.where((ci < ri) & ((ci // group) == (ri // group)), 1.0, 0.0).astype(BF16)
    o_ref[...] = w + _dot_sel_l(prev, tot)


def _cumsum_rows(x, group):
    R = x.shape[0]
    rb = min(R, 256)
    return pl.pallas_call(
        functools.partial(_cumsum_kernel, group=group),
        grid=(R // rb,),
        in_specs=[pl.BlockSpec((rb, LANES), lambda r: (r, 0))],
        out_specs=pl.BlockSpec((rb, LANES), lambda r: (r, 0)),
        out_shape=jax.ShapeDtypeStruct((R, LANES), F32),
        compiler_params=_cparams(("parallel",)),
        name="cumsum",
    )(x)


def _foxsamp_kernel(q_ref, k_ref, v_ref, z_ref, sm_ref, fb_ref, tot_ref, fcc_ref, ck_ref, cv_ref,
                    o_ref, ko_ref, vo_ref, lf_ref):
    S = q_ref.shape[0]
    HD = ck_ref.shape[-1]
    p = pl.program_id(1)
    logf = _logf_from_small(sm_ref[...], fb_ref[...])
    lf_ref[...] = logf
    ltri = jnp.where(_iota2((S, S), 0) >= _iota2((S, S), 1), 1.0, 0.0).astype(BF16)
    fc = _dot_sel_l(ltri, logf) + tot_ref[...]
    fct = jnp.concatenate([fc, jnp.zeros((LANES - S, LANES), F32)], axis=0).T
    lane = _iota2((S, LANES), 1)
    row = _iota2((LANES, LANES), 0)
    causal = _iota2((S, S), 0) >= _iota2((S, S), 1)
    scale = HD ** -0.5
    for hh in range(2):
        h = 2 * p + hh
        fq = jnp.sum(jnp.where(lane == h, fc, 0.0), axis=1, keepdims=True)
        fkn = jnp.sum(jnp.where(row == h, fct, 0.0), axis=0, keepdims=True)[:, :S]
        q = q_ref[:, hh * HD:(hh + 1) * HD] * scale
        kn = k_ref[:, hh * HD:(hh + 1) * HD]
        vn = v_ref[:, hh * HD:(hh + 1) * HD]
        ko_ref[hh] = kn
        vo_ref[hh] = vn
        sc = _dot_nt(q, ck_ref[hh]) + fq - fcc_ref[hh]
        sn = jnp.where(causal, _dot_nt(q, kn) + fq - fkn, -jnp.inf)
        m = jnp.maximum(jnp.max(sc, axis=1, keepdims=True), jnp.max(sn, axis=1, keepdims=True))
        pc = jnp.exp(sc - m)
        pn = jnp.exp(sn - m)
        l = jnp.sum(pc, axis=1, keepdims=True) + jnp.sum(pn, axis=1, keepdims=True)
        o = (_dot(pc, cv_ref[hh]) + _dot(pn, vn)) / l
        o_ref[:, hh * HD:(hh + 1) * HD] = o * _silu(z_ref[:, hh * HD:(hh + 1) * HD])


def _foxsamp(proj, small, cols, fb_row, tot, fcc, cache_k, cache_v, l):
    B, S, _ = proj.shape
    _, _, H, P, HD = cache_k.shape
    cq, ck, cv, cz = cols
    pair = lambda c: pl.BlockSpec((None, S, LANES), lambda b, p: (b, 0, c + p))
    kv_out = jax.ShapeDtypeStruct((B, H, S, HD), F32)
    return pl.pallas_call(
        _foxsamp_kernel,
        grid=(B, H // 2),
        in_specs=[pair(cq), pair(ck), pair(cv), pair(cz),
                  pl.BlockSpec((None, S, LANES), lambda b, p: (b, 0, 0)),
                  pl.BlockSpec((1, LANES), lambda b, p: (0, 0)),
                  pl.BlockSpec((None, 1, LANES), lambda b, p: (b, 0, 0)),
                  pl.BlockSpec((None, 2, 1, P), lambda b, p: (b, p, 0, 0)),
                  pl.BlockSpec((None, None, 2, P, HD), lambda b, p: (l, b, p, 0, 0)),
                  pl.BlockSpec((None, None, 2, P, HD), lambda b, p: (l, b, p, 0, 0))],
        out_specs=[pl.BlockSpec((None, S, LANES), lambda b, p: (b, 0, p)),
                   pl.BlockSpec((None, 2, S, HD), lambda b, p: (b, p, 0, 0)),
                   pl.BlockSpec((None, 2, S, HD), lambda b, p: (b, p, 0, 0)),
                   pl.BlockSpec((None, S, LANES), lambda b, p: (b, 0, 0))],
        out_shape=[jax.ShapeDtypeStruct((B, S, H * HD), F32), kv_out, kv_out,
                   jax.ShapeDtypeStruct((B, S, LANES), F32)],
        compiler_params=_cparams(("parallel", "arbitrary")),
        name="foxsamp",
    )(proj, proj, proj, proj, small, fb_row, tot, fcc, cache_k, cache_v)


COL_MG = 0
COL_GQKV = 6
COL_FQKV = 9
COL_RQK = 12
COL_RV = 13
COL_RZ = 14
COL_FZ = 15
COL_GZ = 16


def _prep_w_in(w_in, D):
    BW = D // 2
    hk = BW // 2
    splits = (hk, hk, BW, BW, BW, BW, BW, H_FOX, BW, BW, BW, BW, H_GDN, H_GDN, BW, N_BRANCH * D)
    offs = np.cumsum((0,) + splits)
    seg = lambda i: w_in[..., offs[i]:offs[i + 1]]
    rq, rk, rv, rz, fq, fk, fv, ff, fz, gq, gk, gv, ga, gb, gz, mg = [seg(i) for i in range(16)]
    main = jnp.concatenate([mg, gq, gk, gv, fq, fk, fv, rq, rk, rv, rz, fz, gz], axis=-1)
    pad = jnp.zeros(w_in.shape[:-1] + (LANES - H_FOX - 2 * H_GDN,), w_in.dtype)
    small = jnp.concatenate([ff, ga, gb, pad], axis=-1)
    return main.astype(BF16), small.astype(BF16)


def _lane_row(vals, off):
    L, n = vals.shape
    return jnp.zeros((L, 1, LANES), F32).at[:, 0, off:off + n].set(vals)


def kernel(x_prompt, x_sample, c_prompt, c_sample, cache_fox_k, cache_fox_v, cache_fox_logf,
           state_ret, state_gdn, state_gdn_conv, norm_w, ada_w, ada_b, w_in, fox_f_bias,
           gdn_a_log, gdn_dt_bias, gdn_conv_w, ret_norm_w, gdn_norm_w, w_branch, w_out,
           final_norm_w):
    B, S, D = x_prompt.shape
    BS, SS, _ = x_sample.shape
    L = ada_w.shape[0]
    P = cache_fox_k.shape[3]
    DK_RET, DV_RET = state_ret.shape[-2:]
    DK_GDN, DV_GDN = state_gdn.shape[-2:]

    rows = -(-(B + BS) // 8) * 8
    c_all = jnp.concatenate([c_prompt, c_sample, jnp.zeros((rows - B - BS, D), F32)], axis=0)
    mod = _adaln(c_all, ada_w, ada_b)
    mod_p = mod[:, :, :B].reshape(L, 3, B, 1, D)
    mod_s = mod[:, :, B:B + BS].reshape(L, 3, BS, 1, D)

    w_main, w_small = _prep_w_in(w_in, D)
    w_branch_b = w_branch.astype(BF16)
    w_out_b = w_out.astype(BF16)
    norm_w3 = norm_w.reshape(L, 1, D)
    final_w = final_norm_w.reshape(1, D)
    ret_nw = ret_norm_w.reshape(L, H_RET, 1, DV_RET)
    gdn_nw = gdn_norm_w.reshape(L, 1, DV_GDN)
    fb_rows = _lane_row(fox_f_bias, 0)
    a_rows = _lane_row(gdn_a_log, H_FOX)
    dt_rows = _lane_row(gdn_dt_bias, H_FOX)

    tm_p = min(S, 1024)
    tmm_p = min(S, 512)
    t_ret = min(S, 256)
    t_gdn = min(S, 256)
    t_prep = min(S, 256)
    t_flash = min(S, 1024)

    tabs_p = _ret_tables(t_ret, 0, S, DK_RET)
    tabs_s = _ret_tables(SS, P, SS, DK_RET)

    fcc = _cumsum_rows(cache_fox_logf.reshape(-1, LANES), P // LANES).reshape(L, BS, H_FOX, 1, P)
    tot = jnp.zeros((L, BS, 1, LANES), F32).at[:, :, 0, :H_FOX].set(fcc[:, :, :, 0, P - 1])

    zeros_ret = jnp.zeros((B, H_RET, DK_RET, DV_RET), F32)
    zeros_gdn = jnp.zeros((B, H_GDN, DK_GDN, DV_GDN), F32)
    zeros_conv = jnp.zeros((B, CONV_W - 1, gdn_conv_w.shape[-1]), F32)

    xp, xs = x_prompt, x_sample
    pk = pv = None
    p_lf, p_ret, p_gdn, p_conv = [], [], [], []
    s_k, s_v, s_lf, s_ret, s_gdn, s_conv = [], [], [], [], [], []
    for l in range(L):
        final = l == L - 1
        proj, small = _inproj(xp, mod_p, l, norm_w3, w_main, w_small, 1, tm_p)
        o_ret, st = _ret(proj, (COL_RQK, COL_RV, COL_RZ), tabs_p, ret_nw[l], zeros_ret, t_ret, BF16)
        p_ret.append(st)
        o_gdn, st, cv = _gdn(proj, small, (COL_GQKV // 3, COL_GZ), gdn_conv_w[l], zeros_conv,
                             a_rows[l], dt_rows[l], gdn_nw[l], zeros_gdn, t_gdn, BF16)
        p_gdn.append(st)
        p_conv.append(cv)
        qa, ka, va, pk, pv, lf = _foxprep(proj, small, COL_FQKV // 3, fb_rows[l], l, pk, pv, L, t_prep)
        p_lf.append(lf)
        o_fox = _flash(qa, ka, va, proj, COL_FZ * 4, t_flash)
        xp = _merge(xp, mod_p, l, o_ret, o_fox, o_gdn, proj, w_branch_b, w_out_b, final_w,
                    1, tmm_p, final)
        proj, small = _inproj(xs, mod_s, l, norm_w3, w_main, w_small, BS, SS)
        o_ret, st = _ret(proj, (COL_RQK, COL_RV, COL_RZ), tabs_s, ret_nw[l], state_ret[l], SS, F32)
        s_ret.append(st)
        o_gdn, st, cv = _gdn(proj, small, (COL_GQKV // 3, COL_GZ), gdn_conv_w[l], state_gdn_conv[l],
                             a_rows[l], dt_rows[l], gdn_nw[l], state_gdn[l], SS, F32)
        s_gdn.append(st)
        s_conv.append(cv)
        o_fox, kk, vv, lf = _foxsamp(proj, small,
                                     (COL_FQKV * 4, (COL_FQKV + 1) * 4, (COL_FQKV + 2) * 4, COL_FZ * 4),
                                     fb_rows[l], tot[l], fcc[l], cache_fox_k, cache_fox_v, l)
        s_k.append(kk)
        s_v.append(vv)
        s_lf.append(lf)
        xs = _merge(xs, mod_s, l, o_ret, o_fox, o_gdn, proj, w_branch_b, w_out_b, final_w,
                    BS, SS, final)

    logf_out = lambda lfs: jnp.stack(lfs)[..., :H_FOX].transpose(0, 1, 3, 2)
    return (xp, xs,
            pk, pv, logf_out(p_lf), jnp.stack(p_ret), jnp.stack(p_gdn), jnp.stack(p_conv),
            jnp.stack(s_k), jnp.stack(s_v), logf_out(s_lf), jnp.stack(s_ret), jnp.stack(s_gdn),
            jnp.stack(s_conv))
```

```python
import functools

import numpy as np
import jax
import jax.numpy as jnp
from jax import lax
from jax.experimental import pallas as pl
from jax.experimental.pallas import tpu as pltpu

F32 = jnp.float32
BF16 = jnp.bfloat16

N_BRANCH = 3
H_RET = 4
H_FOX = 8
H_GDN = 4
CONV_W = 4
CHUNK = 64
NORM_EPS = 1e-6
ROPE_BASE = 10000.0
LOG2E = 1.4426950408889634
LANES = 128
GDN_CHUNK = 128
VMEM_LIMIT = 56 * 1024 * 1024


def _cparams(sem):
    return pltpu.CompilerParams(dimension_semantics=sem, vmem_limit_bytes=VMEM_LIMIT)


def _sigmoid(x):
    return 1.0 / (1.0 + jnp.exp(-x))


def _silu(x):
    return x * _sigmoid(x)


def _softplus(x):
    return jnp.maximum(x, 0.0) + jnp.log(1.0 + jnp.exp(-jnp.abs(x)))


def _dot(a, b):
    return jnp.dot(a.astype(BF16), b.astype(BF16), preferred_element_type=F32)


def _dot_nt(a, b):
    return lax.dot_general(a.astype(BF16), b.astype(BF16), (((1,), (1,)), ((), ())),
                           preferred_element_type=F32)


def _split2(x):
    hi = x.astype(BF16)
    lo = (x - hi.astype(F32)).astype(BF16)
    return hi, lo


def _split3(x):
    hi = x.astype(BF16)
    r = x - hi.astype(F32)
    mid = r.astype(BF16)
    lo = (r - mid.astype(F32)).astype(BF16)
    return hi, mid, lo


def _dot_hi(a, b):
    ah, al = _split2(a)
    bh, bl = _split2(b)
    d = lambda x, y: jnp.dot(x, y, preferred_element_type=F32)
    return d(ah, bh) + (d(ah, bl) + d(al, bh))


def _dot_sel_l(m, x):
    hi, mid, lo = _split3(x)
    d = lambda y: jnp.dot(m, y, preferred_element_type=F32)
    return d(hi) + (d(mid) + d(lo))


def _dot_sel_r(x, m):
    hi, mid, lo = _split3(x)
    d = lambda y: jnp.dot(y, m, preferred_element_type=F32)
    return d(hi) + (d(mid) + d(lo))


def _iota2(shape, dim):
    return lax.broadcasted_iota(jnp.int32, shape, dim)


def _adaln_kernel(c_ref, w_ref, b_ref, o_ref):
    s = _silu(c_ref[...])
    o_ref[...] = _dot(s, w_ref[...]) + b_ref[...]


def _adaln(c_all, ada_w, ada_b):
    L, D, _ = ada_w.shape
    R = c_all.shape[0]
    return pl.pallas_call(
        _adaln_kernel,
        grid=(L, 3),
        in_specs=[pl.BlockSpec((R, D), lambda l, k: (0, 0)),
                  pl.BlockSpec((None, D, D), lambda l, k: (l, 0, k)),
                  pl.BlockSpec((None, None, 1, D), lambda l, k: (l, k, 0, 0))],
        out_specs=pl.BlockSpec((None, None, R, D), lambda l, k: (l, k, 0, 0)),
        out_shape=jax.ShapeDtypeStruct((L, 3, R, D), F32),
        compiler_params=_cparams(("parallel", "parallel")),
        name="adaln",
    )(c_all, ada_w, ada_b.reshape(L, 3, 1, D))


PROJ_TN = 512


def _inproj_kernel(x_ref, nw_ref, sc_ref, sh_ref, w_ref, ws_ref, o_ref, os_ref, h_ref):
    bb, tm, D = x_ref.shape

    @pl.when(pl.program_id(2) == 0)
    def _():
        x = x_ref[...]
        ms = jnp.mean(x * x, axis=-1, keepdims=True)
        xn = (x * lax.rsqrt(ms + NORM_EPS)) * nw_ref[...]
        h = xn * (1.0 + sc_ref[...]) + sh_ref[...]
        hb = h.astype(BF16).reshape(bb * tm, D)
        h_ref[...] = hb
        os_ref[...] = jnp.dot(hb, ws_ref[...], preferred_element_type=F32).reshape(bb, tm, LANES)

    o_ref[...] = jnp.dot(h_ref[...], w_ref[...],
                         preferred_element_type=F32).reshape(bb, tm, PROJ_TN)


def _inproj(x, mod, l, norm_w, w_main, w_small, bb, tm):
    B, S, D = x.shape
    NP = w_main.shape[-1]
    nj = NP // PROJ_TN
    return pl.pallas_call(
        _inproj_kernel,
        grid=(B // bb, S // tm, nj),
        in_specs=[pl.BlockSpec((bb, tm, D), lambda b, i, j: (b, i, 0)),
                  pl.BlockSpec((None, 1, D), lambda b, i, j: (l, 0, 0)),
                  pl.BlockSpec((None, None, bb, 1, D), lambda b, i, j: (l, 1, b, 0, 0)),
                  pl.BlockSpec((None, None, bb, 1, D), lambda b, i, j: (l, 0, b, 0, 0)),
                  pl.BlockSpec((None, D, PROJ_TN), lambda b, i, j: (l, 0, j)),
                  pl.BlockSpec((None, D, LANES), lambda b, i, j: (l, 0, 0))],
        out_specs=[pl.BlockSpec((bb, tm, PROJ_TN), lambda b, i, j: (b, i, j)),
                   pl.BlockSpec((bb, tm, LANES), lambda b, i, j: (b, i, 0))],
        out_shape=[jax.ShapeDtypeStruct((B, S, NP), F32),
                   jax.ShapeDtypeStruct((B, S, LANES), F32)],
        scratch_shapes=[pltpu.VMEM((bb * tm, D), BF16)],
        compiler_params=_cparams(("parallel", "parallel", "arbitrary")),
        name="inproj",
    )(x, norm_w, mod, mod, w_main, w_small)


def _merge_kernel(x_ref, g_ref, o1_ref, o2_ref, o3_ref, mg_ref, wb_ref, wo_ref, fw_ref, y_ref,
                  *, final):
    bb, tm, D = x_ref.shape
    M = bb * tm
    acc = None
    for n, o_ref in enumerate((o1_ref, o2_ref, o3_ref)):
        br = jnp.dot(o_ref[...].reshape(M, o_ref.shape[-1]).astype(BF16), wb_ref[n],
                     preferred_element_type=F32)
        gate = _sigmoid(mg_ref[:, :, n * D:(n + 1) * D].reshape(M, D))
        term = br * gate
        acc = term if acc is None else acc + term
    out = jnp.dot(acc.astype(BF16), wo_ref[...], preferred_element_type=F32)
    xn = x_ref[...] + g_ref[...] * out.reshape(bb, tm, D)
    if final:
        ms = jnp.mean(xn * xn, axis=-1, keepdims=True)
        xn = (xn * lax.rsqrt(ms + NORM_EPS)) * fw_ref[...]
    y_ref[...] = xn


def _merge(x, mod, l, o_ret, o_fox, o_gdn, proj, w_branch, w_out, final_w, bb, tm, final):
    B, S, D = x.shape
    W = o_ret.shape[-1]
    return pl.pallas_call(
        functools.partial(_merge_kernel, final=final),
        grid=(B // bb, S // tm),
        in_specs=[pl.BlockSpec((bb, tm, D), lambda b, i: (b, i, 0)),
                  pl.BlockSpec((None, None, bb, 1, D), lambda b, i: (l, 2, b, 0, 0)),
                  pl.BlockSpec((bb, tm, W), lambda b, i: (b, i, 0)),
                  pl.BlockSpec((bb, tm, W), lambda b, i: (b, i, 0)),
                  pl.BlockSpec((bb, tm, W), lambda b, i: (b, i, 0)),
                  pl.BlockSpec((bb, tm, N_BRANCH * D), lambda b, i: (b, i, 0)),
                  pl.BlockSpec((None, N_BRANCH, W, D), lambda b, i: (l, 0, 0, 0)),
                  pl.BlockSpec((None, D, D), lambda b, i: (l, 0, 0)),
                  pl.BlockSpec((1, D), lambda b, i: (0, 0))],
        out_specs=pl.BlockSpec((bb, tm, D), lambda b, i: (b, i, 0)),
        out_shape=jax.ShapeDtypeStruct((B, S, D), F32),
        compiler_params=_cparams(("parallel", "parallel")),
        name="merge",
    )(x, mod, o_ret, o_fox, o_gdn, proj, w_branch, w_out, final_w)


def _ret_kernel(qk_ref, v_ref, z_ref, cos_ref, sin_ref, m_ref, dq_ref, dk_ref, dc_ref, nw_ref,
                s0_ref, o_ref, so_ref, s_ref):
    T = qk_ref.shape[0]
    H = s_ref.shape[0]
    DK = s0_ref.shape[1]
    t = pl.program_id(1)

    @pl.when(t == 0)
    def _():
        zero = jnp.zeros((DK, LANES), F32)
        for h in range(H):
            s0 = s0_ref[h]
            s_ref[h] = jnp.concatenate([s0, zero] if h % 2 == 0 else [zero, s0], axis=0)

    cos = cos_ref[...]
    sin = sin_ref[...]
    lane = _iota2((T, LANES), 1)
    first = (lane % DK) < (DK // 2)
    low = lane < DK

    def rot(x):
        sw = jnp.where(first, pltpu.roll(x, LANES - DK // 2, 1), pltpu.roll(x, DK // 2, 1))
        return x * cos + sw * sin

    nkq = H * DK
    for p in range(H // 2):
        qc = rot(qk_ref[:, p * LANES:(p + 1) * LANES])
        kc = rot(qk_ref[:, nkq + p * LANES:nkq + (p + 1) * LANES]) * (DK ** -0.5)
        kcb = kc.astype(BF16)
        for half in range(2):
            h = 2 * p + half
            sel = low if half == 0 else jnp.logical_not(low)
            qm = jnp.where(sel, qc, 0.0)
            a = _dot_nt(qm, kcb) * m_ref[h]
            vh = v_ref[:, h * LANES:(h + 1) * LANES].astype(BF16)
            s = s_ref[h]
            o = _dot(a, vh) + _dot(qm * dq_ref[h], s)
            km = jnp.where(sel, kc, 0.0) * dk_ref[h]
            s_ref[h] = s * dc_ref[h] + _dot(km.T, vh)
            ms = jnp.mean(o * o, axis=-1, keepdims=True)
            on = (o * lax.rsqrt(ms + NORM_EPS)) * nw_ref[h]
            z = z_ref[:, h * LANES:(h + 1) * LANES]
            o_ref[:, h * LANES:(h + 1) * LANES] = (on * _silu(z)).astype(o_ref.dtype)

    @pl.when(t == pl.num_programs(1) - 1)
    def _():
        for h in range(H):
            so_ref[h] = s_ref[h][(h % 2) * DK:(h % 2 + 1) * DK, :]


def _ret_tables(T, pos0, S, DK):
    half = DK // 2
    inv_freq = ROPE_BASE ** (-jnp.arange(half, dtype=F32) / half)
    pos = pos0 + jnp.arange(S)
    ang = pos.astype(F32)[:, None] * inv_freq[None, :]
    cos, sin = jnp.cos(ang), jnp.sin(ang)
    reps = LANES // DK
    cos_t = jnp.tile(jnp.concatenate([cos, cos], axis=1), (1, reps))
    sin_t = jnp.tile(jnp.concatenate([-sin, sin], axis=1), (1, reps))
    log_g = jnp.log1p(-jnp.exp2(-5.0 - jnp.arange(H_RET, dtype=F32)))
    i = jnp.arange(T)
    dist = jnp.abs(i[:, None] - i[None, :]).astype(F32)
    allowed = (i[None, :] // CHUNK) <= (i[:, None] // CHUNK)
    m = jnp.where(allowed[None], jnp.exp(dist[None] * log_g[:, None, None]), 0.0)
    fi = i.astype(F32)
    dq = jnp.exp((fi[None, :] + 1.0) * log_g[:, None])
    dk = jnp.exp((T - 1.0 - fi)[None, :] * log_g[:, None])
    dc = jnp.exp(T * log_g)
    bc = lambda a: jnp.broadcast_to(a[..., None], a.shape + (LANES,))
    return cos_t, sin_t, m, bc(dq), bc(dk), bc(dc[:, None])


def _ret(proj, cols, tabs, ret_nw, s0, T, out_dtype):
    B, S, _ = proj.shape
    H, DK, DV = s0.shape[1:]
    W = H * DV
    cqk, cv, cz = cols
    cos_t, sin_t, m, dq, dk, dc = tabs
    return pl.pallas_call(
        _ret_kernel,
        grid=(B, S // T),
        in_specs=[pl.BlockSpec((None, T, W), lambda b, t: (b, t, cqk)),
                  pl.BlockSpec((None, T, W), lambda b, t: (b, t, cv)),
                  pl.BlockSpec((None, T, W), lambda b, t: (b, t, cz)),
                  pl.BlockSpec((T, LANES), lambda b, t: (t, 0)),
                  pl.BlockSpec((T, LANES), lambda b, t: (t, 0)),
                  pl.BlockSpec((H, T, T), lambda b, t: (0, 0, 0)),
                  pl.BlockSpec((H, T, LANES), lambda b, t: (0, 0, 0)),
                  pl.BlockSpec((H, T, LANES), lambda b, t: (0, 0, 0)),
                  pl.BlockSpec((H, 1, LANES), lambda b, t: (0, 0, 0)),
                  pl.BlockSpec((H, 1, DV), lambda b, t: (0, 0, 0)),
                  pl.BlockSpec((None, H, DK, DV), lambda b, t: (b, 0, 0, 0))],
        out_specs=[pl.BlockSpec((None, T, W), lambda b, t: (b, t, 0)),
                   pl.BlockSpec((None, H, DK, DV), lambda b, t: (b, 0, 0, 0))],
        out_shape=[jax.ShapeDtypeStruct((B, S, W), out_dtype),
                   jax.ShapeDtypeStruct((B, H, DK, DV), F32)],
        scratch_shapes=[pltpu.VMEM((H, LANES, DV), F32)],
        compiler_params=_cparams(("parallel", "arbitrary")),
        name="ret",
    )(proj, proj, proj, cos_t, sin_t, m, dq, dk, dc, ret_nw, s0)


INV_BASE = 8


def _mm3(a, b):
    d = lambda x, y: jnp.dot(x, y, preferred_element_type=F32)
    return d(a[0], b[0]) + (d(a[0], b[1]) + d(a[1], b[0]))


def _unit_lower_inverses(mats, eye, ri, ci):
    n = mats[0].shape[0]
    b = INV_BASE
    d = lambda x, y: jnp.dot(x, y, preferred_element_type=F32)
    same = (ri // b) == (ci // b)
    ads = [jnp.where(same, a, 0.0) for a in mats]
    xs = [eye - ad for ad in ads]
    ps = [ad.astype(BF16) for ad in ads]
    k = 2
    while k < b:
        ps = [d(p, p).astype(BF16) for p in ps]
        xs = [x + d(x.astype(BF16), p) for x, p in zip(xs, ps)]
        k *= 2
    while b < n:
        off = ((ri // (2 * b)) == (ci // (2 * b))) & ((ri // b) != (ci // b))
        xbs = [x.astype(BF16) for x in xs]
        ts = [d(jnp.where(off, a, 0.0).astype(BF16), xb) for a, xb in zip(mats, xbs)]
        xs = [x - d(xb, t.astype(BF16)) for x, xb, t in zip(xs, xbs, ts)]
        b *= 2
    return xs


def _gdn_kernel(u_ref, z_ref, sm_ref, cw_ref, cb_ref, al_ref, dt_ref, nw_ref, s0_ref,
                o_ref, so_ref, co_ref, ext_ref, s_ref):
    T, C3 = u_ref.shape
    H = s_ref.shape[0]
    DK = s_ref.shape[1]
    W = H * DK
    C = GDN_CHUNK
    TP = -(-T // C) * C
    t = pl.program_id(1)
    last = pl.num_programs(1) - 1
    base = 8
    lo = base - (CONV_W - 1)

    @pl.when(t == 0)
    def _():
        s_ref[...] = s0_ref[...]
        ext_ref[lo:base, :] = cb_ref[...]

    ext_ref[base:base + T, :] = u_ref[...]
    y = ext_ref[lo:lo + T, :] * cw_ref[0:1, :]
    for j in range(1, CONV_W):
        y = y + ext_ref[lo + j:lo + j + T, :] * cw_ref[j:j + 1, :]
    tail = ext_ref[lo + T:base + T, :]
    ext_ref[lo:base, :] = tail

    @pl.when(t == last)
    def _():
        co_ref[...] = tail

    qkv = _silu(y)
    sm = sm_ref[...]
    g_all = -jnp.exp(al_ref[...]) * _softplus(sm + dt_ref[...])
    b_all = _sigmoid(sm)
    z_all = z_ref[...]
    if TP > T:
        pad = lambda a: jnp.concatenate([a, jnp.zeros((TP - T, a.shape[1]), F32)], axis=0)
        qkv, g_all, b_all, z_all = pad(qkv), pad(g_all), pad(b_all), pad(z_all)

    ri = _iota2((C, C), 0)
    ci = _iota2((C, C), 1)
    tri = ri >= ci
    strict = ri > ci
    ltri = jnp.where(tri, 1.0, 0.0).astype(BF16)
    eye = jnp.where(ri == ci, 1.0, 0.0)

    nc = TP // C
    items = [(c, h) for c in range(nc) for h in range(H)]
    gcums = [_dot_sel_l(ltri, g_all[c * C:(c + 1) * C, :]) for c in range(nc)]

    qs, ks, vs, gcs, bcs, decs, kbs = [], [], [], [], [], [], []
    for c, h in items:
        r0 = c * C
        q = qkv[r0:r0 + C, h * DK:(h + 1) * DK]
        k = qkv[r0:r0 + C, W + h * DK:W + (h + 1) * DK]
        qs.append(q * lax.rsqrt(jnp.sum(q * q, axis=-1, keepdims=True) + NORM_EPS) * (DK ** -0.5))
        k = k * lax.rsqrt(jnp.sum(k * k, axis=-1, keepdims=True) + NORM_EPS)
        ks.append(k)
        kbs.append(k.astype(BF16))
        vs.append(qkv[r0:r0 + C, 2 * W + h * DK:2 * W + (h + 1) * DK])
        gc = gcums[c][:, 8 + h:9 + h]
        gcs.append(gc)
        bcs.append(b_all[r0:r0 + C, 12 + h:13 + h])
        gm = jnp.broadcast_to(gc, (C, C))
        decs.append(jnp.exp(jnp.where(tri, gm - gm.T, -jnp.inf)))
    kks = [_dot_nt(kb, kb) for kb in kbs]
    amats = [jnp.where(strict, kk * dec, 0.0) * bc for kk, dec, bc in zip(kks, decs, bcs)]
    tinvs = _unit_lower_inverses(amats, eye, ri, ci)
    egs = [jnp.exp(gc) for gc in gcs]
    rhss = [_split2(jnp.concatenate([k * (bc * eg), v * bc], axis=1))
            for k, v, bc, eg in zip(ks, vs, bcs, egs)]
    wus = [_mm3(_split2(tinv), rhs) for tinv, rhs in zip(tinvs, rhss)]
    qks = [(_dot_nt(q, kb) * dec).astype(BF16) for q, kb, dec in zip(qs, kbs, decs)]
    qes = [(q * eg).astype(BF16) for q, eg in zip(qs, egs)]
    glasts = [gcums[c][C - 1:C, 8 + h:9 + h] for c, h in items]
    kdts = [(k * jnp.exp(gl - gc)).T.astype(BF16) for k, gl, gc in zip(ks, glasts, gcs)]
    sdec = [jnp.exp(gl) for gl in glasts]

    for c in range(nc):
        idx = [c * H + h for h in range(H)]
        ss = [s_ref[h] for h in range(H)]
        sbs = [s.astype(BF16) for s in ss]
        us = [wus[i][:, DK:] - _dot(wus[i][:, :DK], sb) for i, sb in zip(idx, sbs)]
        ubs = [u.astype(BF16) for u in us]
        os_ = [jnp.dot(qes[i], sb, preferred_element_type=F32)
               + jnp.dot(qks[i], ub, preferred_element_type=F32)
               for i, sb, ub in zip(idx, sbs, ubs)]
        for h, i in enumerate(idx):
            s_ref[h] = ss[h] * sdec[i] + jnp.dot(kdts[i], ubs[h], preferred_element_type=F32)
        for h, o in enumerate(os_):
            ms = jnp.mean(o * o, axis=-1, keepdims=True)
            on = (o * lax.rsqrt(ms + NORM_EPS)) * nw_ref[...]
            z = z_all[c * C:(c + 1) * C, h * DK:(h + 1) * DK]
            res = (on * _silu(z)).astype(o_ref.dtype)
            if TP > T:
                o_ref[:, h * DK:(h + 1) * DK] = res[:T, :]
            else:
                o_ref[c * C:(c + 1) * C, h * DK:(h + 1) * DK] = res

    @pl.when(t == last)
    def _():
        so_ref[...] = s_ref[...]


def _gdn(proj, small, cols, conv_w, conv_buf, a_row, dt_row, gdn_nw, s0, T, out_dtype):
    B, S, _ = proj.shape
    H, DK, DV = s0.shape[1:]
    W = H * DK
    C3 = conv_w.shape[-1]
    cu, cz = cols
    return pl.pallas_call(
        _gdn_kernel,
        grid=(B, S // T),
        in_specs=[pl.BlockSpec((None, T, C3), lambda b, t: (b, t, cu)),
                  pl.BlockSpec((None, T, W), lambda b, t: (b, t, cz)),
                  pl.BlockSpec((None, T, LANES), lambda b, t: (b, t, 0)),
                  pl.BlockSpec((CONV_W, C3), lambda b, t: (0, 0)),
                  pl.BlockSpec((None, CONV_W - 1, C3), lambda b, t: (b, 0, 0)),
                  pl.BlockSpec((1, LANES), lambda b, t: (0, 0)),
                  pl.BlockSpec((1, LANES), lambda b, t: (0, 0)),
                  pl.BlockSpec((1, DV), lambda b, t: (0, 0)),
                  pl.BlockSpec((None, H, DK, DV), lambda b, t: (b, 0, 0, 0))],
        out_specs=[pl.BlockSpec((None, T, W), lambda b, t: (b, t, 0)),
                   pl.BlockSpec((None, H, DK, DV), lambda b, t: (b, 0, 0, 0)),
                   pl.BlockSpec((None, CONV_W - 1, C3), lambda b, t: (b, 0, 0))],
        out_shape=[jax.ShapeDtypeStruct((B, S, W), out_dtype),
                   jax.ShapeDtypeStruct((B, H, DK, DV), F32),
                   jax.ShapeDtypeStruct((B, CONV_W - 1, C3), F32)],
        scratch_shapes=[pltpu.VMEM((T + 8, C3), F32),
                        pltpu.VMEM((H, DK, DV), F32)],
        compiler_params=_cparams(("parallel", "arbitrary")),
        name="gdn",
    )(proj, proj, small, conv_w, conv_buf, a_row, dt_row, gdn_nw, s0)


def _logf_from_small(sm, fb):
    lane = _iota2(sm.shape, 1)
    return jnp.where(lane < H_FOX, -_softplus(-(sm + fb)), 0.0)


def _foxprep_kernel(qkv_ref, sm_ref, fb_ref, qa_ref, ka_ref, vt_ref, ko_ref, vo_ref, lf_ref,
                    carry_ref):
    T = sm_ref.shape[0]
    H = qa_ref.shape[0]
    HD = ko_ref.shape[1]
    W = H * HD
    t = pl.program_id(1)

    @pl.when(t == 0)
    def _():
        carry_ref[...] = jnp.zeros_like(carry_ref)

    logf = _logf_from_small(sm_ref[...], fb_ref[...])
    lf_ref[...] = logf
    ltri = jnp.where(_iota2((T, T), 0) >= _iota2((T, T), 1), 1.0, 0.0).astype(BF16)
    fc = _dot_sel_l(ltri, logf) + carry_ref[0:1, :]
    carry_ref[0:1, :] = fc[T - 1:T, :]
    f_hi, f_mid, f_lo = [p.astype(F32) for p in _split3(fc * LOG2E)]

    lane = _iota2((T, LANES), 1)
    low = lane < HD
    one_q = jnp.where((lane >= HD + 3) & (lane < HD + 6), 1.0, 0.0)
    one_k = jnp.where((lane >= HD) & (lane < HD + 3), 1.0, 0.0)
    ones_rows = jnp.where(_iota2((LANES - HD, T), 0) == 0, 1.0, 0.0)
    scale = (HD ** -0.5) * LOG2E
    for p in range(H // 2):
        qc = qkv_ref[:, p * LANES:(p + 1) * LANES] * scale
        kc = qkv_ref[:, W + p * LANES:W + (p + 1) * LANES]
        kct = kc.T
        vct = qkv_ref[:, 2 * W + p * LANES:2 * W + (p + 1) * LANES].T
        for half in range(2):
            h = 2 * p + half
            if half == 1:
                qh, kh = [pltpu.roll(a, HD, 1) for a in (qc, kc)]
            else:
                qh, kh = qc, kc
            vht = vct[half * HD:(half + 1) * HD, :]
            c_hi, c_mid, c_lo = [a[:, h:h + 1] for a in (f_hi, f_mid, f_lo)]
            ex_q = jnp.where(lane == HD, c_hi,
                             jnp.where(lane == HD + 1, c_mid,
                                       jnp.where(lane == HD + 2, c_lo, one_q)))
            ex_k = jnp.where(lane == HD + 3, -c_hi,
                             jnp.where(lane == HD + 4, -c_mid,
                                       jnp.where(lane == HD + 5, -c_lo, one_k)))
            qa_ref[h] = jnp.where(low, qh, ex_q).astype(BF16)
            ka_ref[h] = jnp.where(low, kh, ex_k).astype(BF16)
            vt_ref[h] = jnp.concatenate([vht, ones_rows], axis=0).astype(BF16)
            ko_ref[h] = kct[half * HD:(half + 1) * HD, :]
            vo_ref[h] = vht


def _foxprep(proj, small, col, fb_row, l, k_prev, v_prev, L, T):
    B, S, _ = proj.shape
    H = H_FOX
    HD = 512 // H
    aug = jax.ShapeDtypeStruct((B, H, S, LANES), BF16)
    augt = jax.ShapeDtypeStruct((B, H, LANES, S), BF16)
    kv = jax.ShapeDtypeStruct((L, B, H, HD, S), F32)
    aug_spec = pl.BlockSpec((None, H, T, LANES), lambda b, t: (b, 0, t, 0))
    augt_spec = pl.BlockSpec((None, H, LANES, T), lambda b, t: (b, 0, 0, t))
    kv_spec = pl.BlockSpec((None, None, H, HD, T), lambda b, t: (l, b, 0, 0, t))
    in_specs = [pl.BlockSpec((None, T, 3 * H * HD), lambda b, t: (b, t, col)),
                pl.BlockSpec((None, T, LANES), lambda b, t: (b, t, 0)),
                pl.BlockSpec((1, LANES), lambda b, t: (0, 0)),
                pl.BlockSpec(memory_space=pl.ANY), pl.BlockSpec(memory_space=pl.ANY)]
    args = [proj, small, fb_row, k_prev, v_prev]
    aliases = {3: 3, 4: 4}
    kernel = lambda a, b, c, _k, _v, *rest: _foxprep_kernel(a, b, c, *rest)
    return pl.pallas_call(
        kernel,
        grid=(B, S // T),
        in_specs=in_specs,
        out_specs=[aug_spec, aug_spec, augt_spec, kv_spec, kv_spec,
                   pl.BlockSpec((None, T, LANES), lambda b, t: (b, t, 0))],
        out_shape=[aug, aug, augt, kv, kv, jax.ShapeDtypeStruct((B, S, LANES), F32)],
        scratch_shapes=[pltpu.VMEM((8, LANES), F32)],
        input_output_aliases=aliases,
        compiler_params=_cparams(("parallel", "arbitrary")),
        name="foxprep",
    )(*args)


FLASH_QS = 256


def _flash_kernel(qi_ref, kj_ref, qa_ref, ka_ref, vt_ref, z_ref, o_ref,
                  s_scr, p_scr, m_scr, acc_scr):
    TQ = qa_ref.shape[1]
    TK = ka_ref.shape[1]
    QS = FLASH_QS
    HD = LANES // 2
    nqs = TQ // QS
    st = pl.program_id(2)
    i = qi_ref[st]
    j = kj_ref[st]
    units = [(qs, hh) for qs in range(nqs) for hh in range(2)]
    nt = (((1,), (1,)), ((), ()))

    @pl.when(j == 0)
    def _():
        m_scr[...] = jnp.full(m_scr.shape, -jnp.inf, F32)
        acc_scr[...] = jnp.zeros_like(acc_scr)

    def scores(n, nk, diag):
        qs, hh = units[n]
        slot = n % 2
        q = qa_ref[hh, qs * QS:(qs + 1) * QS, :]
        s = lax.dot_general(ka_ref[hh, 0:nk, :], q, nt, preferred_element_type=F32)
        if diag:
            keep = _iota2((QS, QS), 0) <= _iota2((QS, QS), 1)
            if nk > QS:
                s_scr[slot, 0:nk - QS, :] = s[0:nk - QS, :]
            s_scr[slot, nk - QS:nk, :] = jnp.where(keep, s[nk - QS:nk, :], -jnp.inf)
        else:
            s_scr[slot, 0:nk, :] = s

    def softmax_pv(n, nk):
        qs, hh = units[n]
        slot = n % 2
        s = s_scr[slot, 0:nk, :]
        m_prev = m_scr[hh, qs, 0:1, :]
        m_new = jnp.maximum(m_prev, jnp.max(s, axis=0, keepdims=True))
        alpha = jnp.exp2(m_prev - m_new)
        p_scr[slot, 0:nk, :] = jnp.exp2(s - m_new).astype(BF16)
        pv = jnp.dot(vt_ref[hh, :, 0:nk], p_scr[slot, 0:nk, :], preferred_element_type=F32)
        acc_scr[hh, qs] = acc_scr[hh, qs] * alpha + pv
        m_scr[hh, qs, 0:1, :] = m_new

    def sweep(nks, diag):
        scores(0, nks[0], diag)
        for n in range(len(units)):
            if n + 1 < len(units):
                scores(n + 1, nks[n + 1], diag)
            softmax_pv(n, nks[n])

    @pl.when(j < i)
    def _():
        sweep([TK] * len(units), False)

    @pl.when(j == i)
    def _():
        sweep([(qs + 1) * QS for qs, _ in units], True)
        lane = _iota2((QS, LANES), 1)
        for qs in range(nqs):
            a0 = acc_scr[0, qs]
            a1 = acc_scr[1, qs]
            o0 = (a0 / a0[HD:HD + 1, :]).T
            o1 = (a1 / a1[HD:HD + 1, :]).T
            o = jnp.where(lane < HD, o0, pltpu.roll(o1, HD, 1))
            z = z_ref[qs * QS:(qs + 1) * QS, :]
            o_ref[qs * QS:(qs + 1) * QS, :] = (o * _silu(z)).astype(o_ref.dtype)


def _flash(qa, ka, vt, proj, zcol, tq):
    B, H, S, _ = qa.shape
    n = S // tq
    steps = [(i, j) for i in range(n) for j in range(i + 1)]
    qi = jnp.asarray([s[0] for s in steps], jnp.int32)
    kj = jnp.asarray([s[1] for s in steps], jnp.int32)
    nqs = tq // FLASH_QS
    grid_spec = pltpu.PrefetchScalarGridSpec(
        num_scalar_prefetch=2,
        grid=(B, H // 2, len(steps)),
        in_specs=[pl.BlockSpec((None, 2, tq, LANES), lambda b, p, s, qi, kj: (b, p, qi[s], 0)),
                  pl.BlockSpec((None, 2, tq, LANES), lambda b, p, s, qi, kj: (b, p, kj[s], 0)),
                  pl.BlockSpec((None, 2, LANES, tq), lambda b, p, s, qi, kj: (b, p, 0, kj[s])),
                  pl.BlockSpec((None, tq, LANES), lambda b, p, s, qi, kj: (b, qi[s], zcol + p))],
        out_specs=pl.BlockSpec((None, tq, LANES), lambda b, p, s, qi, kj: (b, qi[s], p)),
        scratch_shapes=[pltpu.VMEM((2, tq, FLASH_QS), F32),
                        pltpu.VMEM((2, tq, FLASH_QS), BF16),
                        pltpu.VMEM((2, nqs, 8, FLASH_QS), F32),
                        pltpu.VMEM((2, nqs, LANES, FLASH_QS), F32)])
    return pl.pallas_call(
        _flash_kernel,
        grid_spec=grid_spec,
        out_shape=jax.ShapeDtypeStruct((B, S, H * LANES // 2), BF16),
        compiler_params=_cparams(("parallel", "parallel", "arbitrary")),
        name="flash",
    )(qi, kj, qa, ka, vt, proj)


def _cumsum_kernel(x_ref, o_ref, *, group):
    R = x_ref.shape[0]
    x = x_ref[...]
    up = jnp.where(_iota2((LANES, LANES), 0) <= _iota2((LANES, LANES), 1), 1.0, 0.0).astype(BF16)
    w = _dot_sel_r(x, up)
    tot = jnp.broadcast_to(w[:, LANES - 1:LANES], (R, LANES))
    ri = _iota2((R, R), 0)
    ci = _iota2((R, R), 1)
    prev = jnp.where((ci < ri) & ((ci // group) == (ri // group)), 1.0, 0.0).astype(BF16)
    o_ref[...] = w + _dot_sel_l(prev, tot)


def _cumsum_rows(x, group):
    R = x.shape[0]
    rb = min(R, 256)
    return pl.pallas_call(
        functools.partial(_cumsum_kernel, group=group),
        grid=(R // rb,),
        in_specs=[pl.BlockSpec((rb, LANES), lambda r: (r, 0))],
        out_specs=pl.BlockSpec((rb, LANES), lambda r: (r, 0)),
        out_shape=jax.ShapeDtypeStruct((R, LANES), F32),
        compiler_params=_cparams(("parallel",)),
        name="cumsum",
    )(x)


def _foxsamp_kernel(q_ref, k_ref, v_ref, z_ref, sm_ref, fb_ref, tot_ref, fcc_ref, ck_ref, cv_ref,
                    o_ref, ko_ref, vo_ref, lf_ref):
    S = q_ref.shape[0]
    HD = ck_ref.shape[1]
    p = pl.program_id(1)
    logf = _logf_from_small(sm_ref[...], fb_ref[...])
    lf_ref[...] = logf
    ltri = jnp.where(_iota2((S, S), 0) >= _iota2((S, S), 1), 1.0, 0.0).astype(BF16)
    fc = _dot_sel_l(ltri, logf) + tot_ref[...]
    fct = jnp.concatenate([fc, jnp.zeros((LANES - S, LANES), F32)], axis=0).T
    lane = _iota2((S, LANES), 1)
    row = _iota2((LANES, LANES), 0)
    causal = _iota2((S, S), 0) >= _iota2((S, S), 1)
    scale = HD ** -0.5
    for hh in range(2):
        h = 2 * p + hh
        fq = jnp.sum(jnp.where(lane == h, fc, 0.0), axis=1, keepdims=True)
        fkn = jnp.sum(jnp.where(row == h, fct, 0.0), axis=0, keepdims=True)[:, :S]
        q = q_ref[:, hh * HD:(hh + 1) * HD] * scale
        kn = k_ref[:, hh * HD:(hh + 1) * HD]
        vn = v_ref[:, hh * HD:(hh + 1) * HD]
        ko_ref[hh] = kn
        vo_ref[hh] = vn
        sc = _dot(q, ck_ref[hh]) + fq - fcc_ref[hh]
        sn = jnp.where(causal, _dot_nt(q, kn) + fq - fkn, -jnp.inf)
        m = jnp.maximum(jnp.max(sc, axis=1, keepdims=True), jnp.max(sn, axis=1, keepdims=True))
        pc = jnp.exp(sc - m)
        pn = jnp.exp(sn - m)
        l = jnp.sum(pc, axis=1, keepdims=True) + jnp.sum(pn, axis=1, keepdims=True)
        o = (_dot_nt(pc, cv_ref[hh]) + _dot(pn, vn)) / l
        o_ref[:, hh * HD:(hh + 1) * HD] = o * _silu(z_ref[:, hh * HD:(hh + 1) * HD])


def _foxsamp(proj, small, cols, fb_row, tot, fcc, cache_k, cache_v, l):
    B, S, _ = proj.shape
    _, _, H, HD, P = cache_k.shape
    cq, ck, cv, cz = cols
    pair = lambda c: pl.BlockSpec((None, S, LANES), lambda b, p: (b, 0, c + p))
    kv_out = jax.ShapeDtypeStruct((B, H, S, HD), F32)
    return pl.pallas_call(
        _foxsamp_kernel,
        grid=(B, H // 2),
        in_specs=[pair(cq), pair(ck), pair(cv), pair(cz),
                  pl.BlockSpec((None, S, LANES), lambda b, p: (b, 0, 0)),
                  pl.BlockSpec((1, LANES), lambda b, p: (0, 0)),
                  pl.BlockSpec((None, 1, LANES), lambda b, p: (b, 0, 0)),
                  pl.BlockSpec((None, 2, 1, P), lambda b, p: (b, p, 0, 0)),
                  pl.BlockSpec((None, None, 2, HD, P), lambda b, p: (l, b, p, 0, 0)),
                  pl.BlockSpec((None, None, 2, HD, P), lambda b, p: (l, b, p, 0, 0))],
        out_specs=[pl.BlockSpec((None, S, LANES), lambda b, p: (b, 0, p)),
                   pl.BlockSpec((None, 2, S, HD), lambda b, p: (b, p, 0, 0)),
                   pl.BlockSpec((None, 2, S, HD), lambda b, p: (b, p, 0, 0)),
                   pl.BlockSpec((None, S, LANES), lambda b, p: (b, 0, 0))],
        out_shape=[jax.ShapeDtypeStruct((B, S, H * HD), F32), kv_out, kv_out,
                   jax.ShapeDtypeStruct((B, S, LANES), F32)],
        compiler_params=_cparams(("parallel", "arbitrary")),
        name="foxsamp",
    )(proj, proj, proj, proj, small, fb_row, tot, fcc, cache_k, cache_v)


COL_MG = 0
COL_GQKV = 6
COL_FQKV = 9
COL_RQK = 12
COL_RV = 13
COL_RZ = 14
COL_FZ = 15
COL_GZ = 16


def _prep_w_in(w_in, D):
    BW = D // 2
    hk = BW // 2
    splits = (hk, hk, BW, BW, BW, BW, BW, H_FOX, BW, BW, BW, BW, H_GDN, H_GDN, BW, N_BRANCH * D)
    offs = np.cumsum((0,) + splits)
    seg = lambda i: w_in[..., offs[i]:offs[i + 1]]
    rq, rk, rv, rz, fq, fk, fv, ff, fz, gq, gk, gv, ga, gb, gz, mg = [seg(i) for i in range(16)]
    main = jnp.concatenate([mg, gq, gk, gv, fq, fk, fv, rq, rk, rv, rz, fz, gz], axis=-1)
    pad = jnp.zeros(w_in.shape[:-1] + (LANES - H_FOX - 2 * H_GDN,), w_in.dtype)
    small = jnp.concatenate([ff, ga, gb, pad], axis=-1)
    return main.astype(BF16), small.astype(BF16)


def _lane_row(vals, off):
    L, n = vals.shape
    return jnp.zeros((L, 1, LANES), F32).at[:, 0, off:off + n].set(vals)


def kernel(x_prompt, x_sample, c_prompt, c_sample, cache_fox_k, cache_fox_v, cache_fox_logf,
           state_ret, state_gdn, state_gdn_conv, norm_w, ada_w, ada_b, w_in, fox_f_bias,
           gdn_a_log, gdn_dt_bias, gdn_conv_w, ret_norm_w, gdn_norm_w, w_branch, w_out,
           final_norm_w):
    B, S, D = x_prompt.shape
    BS, SS, _ = x_sample.shape
    L = ada_w.shape[0]
    P = cache_fox_k.shape[3]
    DK_RET, DV_RET = state_ret.shape[-2:]
    DK_GDN, DV_GDN = state_gdn.shape[-2:]

    rows = -(-(B + BS) // 8) * 8
    c_all = jnp.concatenate([c_prompt, c_sample, jnp.zeros((rows - B - BS, D), F32)], axis=0)
    mod = _adaln(c_all, ada_w, ada_b)
    mod_p = mod[:, :, :B].reshape(L, 3, B, 1, D)
    mod_s = mod[:, :, B:B + BS].reshape(L, 3, BS, 1, D)

    w_main, w_small = _prep_w_in(w_in, D)
    w_branch_b = w_branch.astype(BF16)
    w_out_b = w_out.astype(BF16)
    norm_w3 = norm_w.reshape(L, 1, D)
    final_w = final_norm_w.reshape(1, D)
    ret_nw = ret_norm_w.reshape(L, H_RET, 1, DV_RET)
    gdn_nw = gdn_norm_w.reshape(L, 1, DV_GDN)
    fb_rows = _lane_row(fox_f_bias, 0)
    a_rows = _lane_row(gdn_a_log, H_FOX)
    dt_rows = _lane_row(gdn_dt_bias, H_FOX)

    tm_p = min(S, 1024)
    tmm_p = min(S, 512)
    t_ret = min(S, 256)
    t_gdn = min(S, 256)
    t_prep = min(S, 256)
    t_flash = min(S, 1024)

    tabs_p = _ret_tables(t_ret, 0, S, DK_RET)
    tabs_s = _ret_tables(SS, P, SS, DK_RET)

    fcc = _cumsum_rows(cache_fox_logf.reshape(-1, LANES), P // LANES).reshape(L, BS, H_FOX, 1, P)
    tot = jnp.zeros((L, BS, 1, LANES), F32).at[:, :, 0, :H_FOX].set(fcc[:, :, :, 0, P - 1])

    zeros_ret = jnp.zeros((B, H_RET, DK_RET, DV_RET), F32)
    zeros_gdn = jnp.zeros((B, H_GDN, DK_GDN, DV_GDN), F32)
    zeros_conv = jnp.zeros((B, CONV_W - 1, gdn_conv_w.shape[-1]), F32)

    xp, xs = x_prompt, x_sample
    cache_kt = jnp.swapaxes(cache_fox_k, 3, 4)
    cache_vt = jnp.swapaxes(cache_fox_v, 3, 4)
    pk = jnp.zeros((L, B, H_FOX, cache_fox_k.shape[-1], S), F32)
    pv = jnp.zeros_like(pk)
    p_lf, p_ret, p_gdn, p_conv = [], [], [], []
    s_k, s_v, s_lf, s_ret, s_gdn, s_conv = [], [], [], [], [], []
    for l in range(L):
        final = l == L - 1
        proj, small = _inproj(xp, mod_p, l, norm_w3, w_main, w_small, 1, tm_p)
        o_ret, st = _ret(proj, (COL_RQK, COL_RV, COL_RZ), tabs_p, ret_nw[l], zeros_ret, t_ret, BF16)
        p_ret.append(st)
        o_gdn, st, cv = _gdn(proj, small, (COL_GQKV // 3, COL_GZ), gdn_conv_w[l], zeros_conv,
                             a_rows[l], dt_rows[l], gdn_nw[l], zeros_gdn, t_gdn, BF16)
        p_gdn.append(st)
        p_conv.append(cv)
        qa, ka, va, pk, pv, lf = _foxprep(proj, small, COL_FQKV // 3, fb_rows[l], l, pk, pv, L, t_prep)
        p_lf.append(lf)
        o_fox = _flash(qa, ka, va, proj, COL_FZ * 4, t_flash)
        xp = _merge(xp, mod_p, l, o_ret, o_fox, o_gdn, proj, w_branch_b, w_out_b, final_w,
                    1, tmm_p, final)
        proj, small = _inproj(xs, mod_s, l, norm_w3, w_main, w_small, BS, SS)
        o_ret, st = _ret(proj, (COL_RQK, COL_RV, COL_RZ), tabs_s, ret_nw[l], state_ret[l], SS, F32)
        s_ret.append(st)
        o_gdn, st, cv = _gdn(proj, small, (COL_GQKV // 3, COL_GZ), gdn_conv_w[l], state_gdn_conv[l],
                             a_rows[l], dt_rows[l], gdn_nw[l], state_gdn[l], SS, F32)
        s_gdn.append(st)
        s_conv.append(cv)
        o_fox, kk, vv, lf = _foxsamp(proj, small,
                                     (COL_FQKV * 4, (COL_FQKV + 1) * 4, (COL_FQKV + 2) * 4, COL_FZ * 4),
                                     fb_rows[l], tot[l], fcc[l], cache_kt, cache_vt, l)
        s_k.append(kk)
        s_v.append(vv)
        s_lf.append(lf)
        xs = _merge(xs, mod_s, l, o_ret, o_fox, o_gdn, proj, w_branch_b, w_out_b, final_w,
                    BS, SS, final)

    logf_out = lambda lfs: jnp.stack(lfs)[..., :H_FOX].transpose(0, 1, 3, 2)
    return (xp, xs,
            jnp.swapaxes(pk, 3, 4), jnp.swapaxes(pv, 3, 4), logf_out(p_lf), jnp.stack(p_ret), jnp.stack(p_gdn), jnp.stack(p_conv),
            jnp.stack(s_k), jnp.stack(s_v), logf_out(s_lf), jnp.stack(s_ret), jnp.stack(s_gdn),
            jnp.stack(s_conv))
```

```python
import functools

import numpy as np
import jax
import jax.numpy as jnp
from jax import lax
from jax.experimental import pallas as pl
from jax.experimental.pallas import tpu as pltpu

F32 = jnp.float32
BF16 = jnp.bfloat16

N_BRANCH = 3
H_RET = 4
H_FOX = 8
H_GDN = 4
CONV_W = 4
CHUNK = 64
NORM_EPS = 1e-6
ROPE_BASE = 10000.0
LOG2E = 1.4426950408889634
LANES = 128
GDN_CHUNK = 128
VMEM_LIMIT = 56 * 1024 * 1024


def _cparams(sem):
    return pltpu.CompilerParams(dimension_semantics=sem, vmem_limit_bytes=VMEM_LIMIT)


def _sigmoid(x):
    return 0.5 * jnp.tanh(0.5 * x) + 0.5


def _silu(x):
    return x * _sigmoid(x)


def _softplus(x):
    return jnp.maximum(x, 0.0) + jnp.log(1.0 + jnp.exp(-jnp.abs(x)))


def _dot(a, b):
    return jnp.dot(a.astype(BF16), b.astype(BF16), preferred_element_type=F32)


def _dot_nt(a, b):
    return lax.dot_general(a.astype(BF16), b.astype(BF16), (((1,), (1,)), ((), ())),
                           preferred_element_type=F32)


def _split3(x):
    hi = x.astype(BF16)
    r = x - hi.astype(F32)
    mid = r.astype(BF16)
    lo = (r - mid.astype(F32)).astype(BF16)
    return hi, mid, lo


def _dot_sel_l(m, x):
    hi, mid, lo = _split3(x)
    d = lambda y: jnp.dot(m, y, preferred_element_type=F32)
    return d(hi) + (d(mid) + d(lo))


def _dot_sel_r(x, m):
    hi, mid, lo = _split3(x)
    d = lambda y: jnp.dot(y, m, preferred_element_type=F32)
    return d(hi) + (d(mid) + d(lo))


def _iota2(shape, dim):
    return lax.broadcasted_iota(jnp.int32, shape, dim)


def _adaln_kernel(c_ref, w_ref, b_ref, o_ref):
    s = _silu(c_ref[...])
    o_ref[...] = _dot(s, w_ref[...]) + b_ref[...]


def _adaln(c_all, ada_w, ada_b):
    L, D, _ = ada_w.shape
    R = c_all.shape[0]
    return pl.pallas_call(
        _adaln_kernel,
        grid=(L, 3),
        in_specs=[pl.BlockSpec((R, D), lambda l, k: (0, 0)),
                  pl.BlockSpec((None, D, D), lambda l, k: (l, 0, k)),
                  pl.BlockSpec((None, None, 1, D), lambda l, k: (l, k, 0, 0))],
        out_specs=pl.BlockSpec((None, None, R, D), lambda l, k: (l, k, 0, 0)),
        out_shape=jax.ShapeDtypeStruct((L, 3, R, D), F32),
        compiler_params=_cparams(("parallel", "parallel")),
        name="adaln",
    )(c_all, ada_w, ada_b.reshape(L, 3, 1, D))


PROJ_TN = 512


def _inproj_kernel(x_ref, nw_ref, sc_ref, sh_ref, w_ref, ws_ref, o_ref, os_ref, h_ref):
    bb, tm, D = x_ref.shape

    @pl.when(pl.program_id(2) == 0)
    def _():
        x = x_ref[...]
        ms = jnp.mean(x * x, axis=-1, keepdims=True)
        xn = (x * lax.rsqrt(ms + NORM_EPS)) * nw_ref[...]
        h = xn * (1.0 + sc_ref[...]) + sh_ref[...]
        hb = h.astype(BF16).reshape(bb * tm, D)
        h_ref[...] = hb
        os_ref[...] = jnp.dot(hb, ws_ref[...], preferred_element_type=F32).reshape(bb, tm, LANES)

    o_ref[...] = jnp.dot(h_ref[...], w_ref[...],
                         preferred_element_type=F32).reshape(bb, tm, PROJ_TN)


def _inproj(x, mod, l, norm_w, w_main, w_small, bb, tm):
    B, S, D = x.shape
    NP = w_main.shape[-1]
    nj = NP // PROJ_TN
    return pl.pallas_call(
        _inproj_kernel,
        grid=(B // bb, S // tm, nj),
        in_specs=[pl.BlockSpec((bb, tm, D), lambda b, i, j: (b, i, 0)),
                  pl.BlockSpec((None, 1, D), lambda b, i, j: (l, 0, 0)),
                  pl.BlockSpec((None, None, bb, 1, D), lambda b, i, j: (l, 1, b, 0, 0)),
                  pl.BlockSpec((None, None, bb, 1, D), lambda b, i, j: (l, 0, b, 0, 0)),
                  pl.BlockSpec((None, D, PROJ_TN), lambda b, i, j: (l, 0, j)),
                  pl.BlockSpec((None, D, LANES), lambda b, i, j: (l, 0, 0))],
        out_specs=[pl.BlockSpec((bb, tm, PROJ_TN), lambda b, i, j: (b, i, j)),
                   pl.BlockSpec((bb, tm, LANES), lambda b, i, j: (b, i, 0))],
        out_shape=[jax.ShapeDtypeStruct((B, S, NP), F32),
                   jax.ShapeDtypeStruct((B, S, LANES), F32)],
        scratch_shapes=[pltpu.VMEM((bb * tm, D), BF16)],
        compiler_params=_cparams(("parallel", "parallel", "arbitrary")),
        name="inproj",
    )(x, norm_w, mod, mod, w_main, w_small)


def _merge_kernel(x_ref, g_ref, o1_ref, o2_ref, o3_ref, mg_ref, wb_ref, wo_ref, fw_ref, y_ref,
                  *, final):
    bb, tm, D = x_ref.shape
    M = bb * tm
    acc = None
    for n, o_ref in enumerate((o1_ref, o2_ref, o3_ref)):
        br = jnp.dot(o_ref[...].reshape(M, o_ref.shape[-1]).astype(BF16), wb_ref[n],
                     preferred_element_type=F32)
        gate = _sigmoid(mg_ref[:, :, n * D:(n + 1) * D].reshape(M, D))
        term = br * gate
        acc = term if acc is None else acc + term
    out = jnp.dot(acc.astype(BF16), wo_ref[...], preferred_element_type=F32)
    xn = x_ref[...] + g_ref[...] * out.reshape(bb, tm, D)
    if final:
        ms = jnp.mean(xn * xn, axis=-1, keepdims=True)
        xn = (xn * lax.rsqrt(ms + NORM_EPS)) * fw_ref[...]
    y_ref[...] = xn


def _merge(x, mod, l, o_ret, o_fox, o_gdn, proj, w_branch, w_out, final_w, bb, tm, final):
    B, S, D = x.shape
    W = o_ret.shape[-1]
    return pl.pallas_call(
        functools.partial(_merge_kernel, final=final),
        grid=(B // bb, S // tm),
        in_specs=[pl.BlockSpec((bb, tm, D), lambda b, i: (b, i, 0)),
                  pl.BlockSpec((None, None, bb, 1, D), lambda b, i: (l, 2, b, 0, 0)),
                  pl.BlockSpec((bb, tm, W), lambda b, i: (b, i, 0)),
                  pl.BlockSpec((bb, tm, W), lambda b, i: (b, i, 0)),
                  pl.BlockSpec((bb, tm, W), lambda b, i: (b, i, 0)),
                  pl.BlockSpec((bb, tm, N_BRANCH * D), lambda b, i: (b, i, 0)),
                  pl.BlockSpec((None, N_BRANCH, W, D), lambda b, i: (l, 0, 0, 0)),
                  pl.BlockSpec((None, D, D), lambda b, i: (l, 0, 0)),
                  pl.BlockSpec((1, D), lambda b, i: (0, 0))],
        out_specs=pl.BlockSpec((bb, tm, D), lambda b, i: (b, i, 0)),
        out_shape=jax.ShapeDtypeStruct((B, S, D), F32),
        compiler_params=_cparams(("parallel", "parallel")),
        name="merge",
    )(x, mod, o_ret, o_fox, o_gdn, proj, w_branch, w_out, final_w)


def _ret_kernel(qk_ref, v_ref, z_ref, cos_ref, sin_ref, m_ref, dq_ref, dk_ref, dc_ref, nw_ref,
                s0_ref, o_ref, so_ref, s_ref):
    T = qk_ref.shape[0]
    H = s_ref.shape[0]
    DK = s0_ref.shape[1]
    t = pl.program_id(1)

    @pl.when(t == 0)
    def _():
        zero = jnp.zeros((DK, LANES), F32)
        for h in range(H):
            s0 = s0_ref[h]
            s_ref[h] = jnp.concatenate([s0, zero] if h % 2 == 0 else [zero, s0], axis=0)

    cos = cos_ref[...]
    sin = sin_ref[...]
    lane = _iota2((T, LANES), 1)
    first = (lane % DK) < (DK // 2)
    low = lane < DK

    def rot(x):
        sw = jnp.where(first, pltpu.roll(x, LANES - DK // 2, 1), pltpu.roll(x, DK // 2, 1))
        return x * cos + sw * sin

    nkq = H * DK
    hs = range(H)
    qcs = [rot(qk_ref[:, p * LANES:(p + 1) * LANES]) for p in range(H // 2)]
    kcs = [rot(qk_ref[:, nkq + p * LANES:nkq + (p + 1) * LANES]) * (DK ** -0.5)
           for p in range(H // 2)]
    kcbs = [kc.astype(BF16) for kc in kcs]
    sels = [low if h % 2 == 0 else jnp.logical_not(low) for h in hs]
    qms = [jnp.where(sels[h], qcs[h // 2], 0.0) for h in hs]
    vhs = [v_ref[:, h * LANES:(h + 1) * LANES].astype(BF16) for h in hs]
    ss = [s_ref[h] for h in hs]
    ams = [_dot_nt(qms[h], kcbs[h // 2]) * m_ref[h] for h in hs]
    kts = [(jnp.where(sels[h], kcs[h // 2], 0.0) * dk_ref[h]).T for h in hs]
    os_ = [_dot(ams[h], vhs[h]) + _dot(qms[h] * dq_ref[h], ss[h]) for h in hs]
    for h in hs:
        s_ref[h] = ss[h] * dc_ref[h] + _dot(kts[h], vhs[h])
    for h in hs:
        o = os_[h]
        ms = jnp.mean(o * o, axis=-1, keepdims=True)
        on = (o * lax.rsqrt(ms + NORM_EPS)) * nw_ref[h]
        z = z_ref[:, h * LANES:(h + 1) * LANES]
        o_ref[:, h * LANES:(h + 1) * LANES] = (on * _silu(z)).astype(o_ref.dtype)

    @pl.when(t == pl.num_programs(1) - 1)
    def _():
        for h in range(H):
            so_ref[h] = s_ref[h][(h % 2) * DK:(h % 2 + 1) * DK, :]


def _ret_tables(T, pos0, S, DK):
    half = DK // 2
    inv_freq = ROPE_BASE ** (-jnp.arange(half, dtype=F32) / half)
    pos = pos0 + jnp.arange(S)
    ang = pos.astype(F32)[:, None] * inv_freq[None, :]
    cos, sin = jnp.cos(ang), jnp.sin(ang)
    reps = LANES // DK
    cos_t = jnp.tile(jnp.concatenate([cos, cos], axis=1), (1, reps))
    sin_t = jnp.tile(jnp.concatenate([-sin, sin], axis=1), (1, reps))
    log_g = jnp.log1p(-jnp.exp2(-5.0 - jnp.arange(H_RET, dtype=F32)))
    i = jnp.arange(T)
    dist = jnp.abs(i[:, None] - i[None, :]).astype(F32)
    allowed = (i[None, :] // CHUNK) <= (i[:, None] // CHUNK)
    m = jnp.where(allowed[None], jnp.exp(dist[None] * log_g[:, None, None]), 0.0)
    fi = i.astype(F32)
    dq = jnp.exp((fi[None, :] + 1.0) * log_g[:, None])
    dk = jnp.exp((T - 1.0 - fi)[None, :] * log_g[:, None])
    dc = jnp.exp(T * log_g)
    bc = lambda a: jnp.broadcast_to(a[..., None], a.shape + (LANES,))
    return cos_t, sin_t, m, bc(dq), bc(dk), bc(dc[:, None])


def _ret(proj, cols, tabs, ret_nw, s0, T, out_dtype):
    B, S, _ = proj.shape
    H, DK, DV = s0.shape[1:]
    W = H * DV
    cqk, cv, cz = cols
    cos_t, sin_t, m, dq, dk, dc = tabs
    return pl.pallas_call(
        _ret_kernel,
        grid=(B, S // T),
        in_specs=[pl.BlockSpec((None, T, W), lambda b, t: (b, t, cqk)),
                  pl.BlockSpec((None, T, W), lambda b, t: (b, t, cv)),
                  pl.BlockSpec((None, T, W), lambda b, t: (b, t, cz)),
                  pl.BlockSpec((T, LANES), lambda b, t: (t, 0)),
                  pl.BlockSpec((T, LANES), lambda b, t: (t, 0)),
                  pl.BlockSpec((H, T, T), lambda b, t: (0, 0, 0)),
                  pl.BlockSpec((H, T, LANES), lambda b, t: (0, 0, 0)),
                  pl.BlockSpec((H, T, LANES), lambda b, t: (0, 0, 0)),
                  pl.BlockSpec((H, 1, LANES), lambda b, t: (0, 0, 0)),
                  pl.BlockSpec((H, 1, DV), lambda b, t: (0, 0, 0)),
                  pl.BlockSpec((None, H, DK, DV), lambda b, t: (b, 0, 0, 0))],
        out_specs=[pl.BlockSpec((None, T, W), lambda b, t: (b, t, 0)),
                   pl.BlockSpec((None, H, DK, DV), lambda b, t: (b, 0, 0, 0))],
        out_shape=[jax.ShapeDtypeStruct((B, S, W), out_dtype),
                   jax.ShapeDtypeStruct((B, H, DK, DV), F32)],
        scratch_shapes=[pltpu.VMEM((H, LANES, DV), F32)],
        compiler_params=_cparams(("parallel", "arbitrary")),
        name="ret",
    )(proj, proj, proj, cos_t, sin_t, m, dq, dk, dc, ret_nw, s0)


INV_BASE = 8


def _unit_lower_inverses(mats, eye, ri, ci):
    n = mats[0].shape[0]
    b = INV_BASE
    d = lambda x, y: jnp.dot(x, y, preferred_element_type=F32)
    same = (ri // b) == (ci // b)
    ads = [jnp.where(same, a, 0.0) for a in mats]
    xs = [eye - ad for ad in ads]
    ps = [ad.astype(BF16) for ad in ads]
    k = 2
    while k < b:
        ps = [d(p, p).astype(BF16) for p in ps]
        xs = [x + d(x.astype(BF16), p) for x, p in zip(xs, ps)]
        k *= 2
    while b < n:
        off = ((ri // (2 * b)) == (ci // (2 * b))) & ((ri // b) != (ci // b))
        xbs = [x.astype(BF16) for x in xs]
        ts = [d(jnp.where(off, a, 0.0).astype(BF16), xb) for a, xb in zip(mats, xbs)]
        xs = [x - d(xb, t.astype(BF16)) for x, xb, t in zip(xs, xbs, ts)]
        b *= 2
    return xs


def _gdn_kernel(u_ref, z_ref, sm_ref, cw_ref, cb_ref, al_ref, dt_ref, nw_ref, s0_ref,
                o_ref, so_ref, co_ref, ext_ref, s_ref):
    T, C3 = u_ref.shape
    H = s_ref.shape[0]
    DK = s_ref.shape[1]
    W = H * DK
    C = GDN_CHUNK
    TP = -(-T // C) * C
    t = pl.program_id(1)
    last = pl.num_programs(1) - 1
    base = 8
    lo = base - (CONV_W - 1)

    @pl.when(t == 0)
    def _():
        s_ref[...] = s0_ref[...]
        ext_ref[lo:base, :] = cb_ref[...]

    ext_ref[base:base + T, :] = u_ref[...]
    y = ext_ref[lo:lo + T, :] * cw_ref[0:1, :]
    for j in range(1, CONV_W):
        y = y + ext_ref[lo + j:lo + j + T, :] * cw_ref[j:j + 1, :]
    tail = ext_ref[lo + T:base + T, :]
    ext_ref[lo:base, :] = tail

    @pl.when(t == last)
    def _():
        co_ref[...] = tail

    qkv = _silu(y)
    sm = sm_ref[...]
    g_all = -jnp.exp(al_ref[...]) * _softplus(sm + dt_ref[...])
    b_all = _sigmoid(sm)
    z_all = z_ref[...]
    if TP > T:
        pad = lambda a: jnp.concatenate([a, jnp.zeros((TP - T, a.shape[1]), F32)], axis=0)
        qkv, g_all, b_all, z_all = pad(qkv), pad(g_all), pad(b_all), pad(z_all)

    ri = _iota2((C, C), 0)
    ci = _iota2((C, C), 1)
    tri = ri >= ci
    strict = ri > ci
    ltri = jnp.where(tri, 1.0, 0.0).astype(BF16)
    eye = jnp.where(ri == ci, 1.0, 0.0)

    nc = TP // C
    items = [(c, h) for c in range(nc) for h in range(H)]
    gcums = [_dot_sel_l(ltri, g_all[c * C:(c + 1) * C, :]) for c in range(nc)]

    qs, ks, vs, gcs, bcs, decs, kbs = [], [], [], [], [], [], []
    for c, h in items:
        r0 = c * C
        q = qkv[r0:r0 + C, h * DK:(h + 1) * DK]
        k = qkv[r0:r0 + C, W + h * DK:W + (h + 1) * DK]
        qs.append(q * lax.rsqrt(jnp.sum(q * q, axis=-1, keepdims=True) + NORM_EPS) * (DK ** -0.5))
        k = k * lax.rsqrt(jnp.sum(k * k, axis=-1, keepdims=True) + NORM_EPS)
        ks.append(k)
        kbs.append(k.astype(BF16))
        vs.append(qkv[r0:r0 + C, 2 * W + h * DK:2 * W + (h + 1) * DK])
        gc = gcums[c][:, 8 + h:9 + h]
        gcs.append(gc)
        bcs.append(b_all[r0:r0 + C, 12 + h:13 + h])
        gm = jnp.broadcast_to(gc, (C, C))
        decs.append(jnp.exp(jnp.where(tri, gm - gm.T, -jnp.inf)))
    kks = [_dot_nt(kb, kb) for kb in kbs]
    amats = [jnp.where(strict, kk * dec, 0.0) * bc for kk, dec, bc in zip(kks, decs, bcs)]
    tinvs = _unit_lower_inverses(amats, eye, ri, ci)
    egs = [jnp.exp(gc) for gc in gcs]
    rhss = [jnp.concatenate([k * (bc * eg), v * bc], axis=1)
            for k, v, bc, eg in zip(ks, vs, bcs, egs)]
    wus = [_dot(tinv, rhs) for tinv, rhs in zip(tinvs, rhss)]
    qks = [(_dot_nt(q, kb) * dec).astype(BF16) for q, kb, dec in zip(qs, kbs, decs)]
    qes = [(q * eg).astype(BF16) for q, eg in zip(qs, egs)]
    glasts = [gcums[c][C - 1:C, 8 + h:9 + h] for c, h in items]
    kdts = [(k * jnp.exp(gl - gc)).T.astype(BF16) for k, gl, gc in zip(ks, glasts, gcs)]
    sdec = [jnp.exp(gl) for gl in glasts]

    for c in range(nc):
        idx = [c * H + h for h in range(H)]
        ss = [s_ref[h] for h in range(H)]
        sbs = [s.astype(BF16) for s in ss]
        us = [wus[i][:, DK:] - _dot(wus[i][:, :DK], sb) for i, sb in zip(idx, sbs)]
        ubs = [u.astype(BF16) for u in us]
        os_ = [jnp.dot(qes[i], sb, preferred_element_type=F32)
               + jnp.dot(qks[i], ub, preferred_element_type=F32)
               for i, sb, ub in zip(idx, sbs, ubs)]
        for h, i in enumerate(idx):
            s_ref[h] = ss[h] * sdec[i] + jnp.dot(kdts[i], ubs[h], preferred_element_type=F32)
        for h, o in enumerate(os_):
            ms = jnp.mean(o * o, axis=-1, keepdims=True)
            on = (o * lax.rsqrt(ms + NORM_EPS)) * nw_ref[...]
            z = z_all[c * C:(c + 1) * C, h * DK:(h + 1) * DK]
            res = (on * _silu(z)).astype(o_ref.dtype)
            if TP > T:
                o_ref[:, h * DK:(h + 1) * DK] = res[:T, :]
            else:
                o_ref[c * C:(c + 1) * C, h * DK:(h + 1) * DK] = res

    @pl.when(t == last)
    def _():
        so_ref[...] = s_ref[...]


def _gdn(proj, small, cols, conv_w, conv_buf, a_row, dt_row, gdn_nw, s0, T, out_dtype):
    B, S, _ = proj.shape
    H, DK, DV = s0.shape[1:]
    W = H * DK
    C3 = conv_w.shape[-1]
    cu, cz = cols
    return pl.pallas_call(
        _gdn_kernel,
        grid=(B, S // T),
        in_specs=[pl.BlockSpec((None, T, C3), lambda b, t: (b, t, cu)),
                  pl.BlockSpec((None, T, W), lambda b, t: (b, t, cz)),
                  pl.BlockSpec((None, T, LANES), lambda b, t: (b, t, 0)),
                  pl.BlockSpec((CONV_W, C3), lambda b, t: (0, 0)),
                  pl.BlockSpec((None, CONV_W - 1, C3), lambda b, t: (b, 0, 0)),
                  pl.BlockSpec((1, LANES), lambda b, t: (0, 0)),
                  pl.BlockSpec((1, LANES), lambda b, t: (0, 0)),
                  pl.BlockSpec((1, DV), lambda b, t: (0, 0)),
                  pl.BlockSpec((None, H, DK, DV), lambda b, t: (b, 0, 0, 0))],
        out_specs=[pl.BlockSpec((None, T, W), lambda b, t: (b, t, 0)),
                   pl.BlockSpec((None, H, DK, DV), lambda b, t: (b, 0, 0, 0)),
                   pl.BlockSpec((None, CONV_W - 1, C3), lambda b, t: (b, 0, 0))],
        out_shape=[jax.ShapeDtypeStruct((B, S, W), out_dtype),
                   jax.ShapeDtypeStruct((B, H, DK, DV), F32),
                   jax.ShapeDtypeStruct((B, CONV_W - 1, C3), F32)],
        scratch_shapes=[pltpu.VMEM((T + 8, C3), F32),
                        pltpu.VMEM((H, DK, DV), F32)],
        compiler_params=_cparams(("parallel", "arbitrary")),
        name="gdn",
    )(proj, proj, small, conv_w, conv_buf, a_row, dt_row, gdn_nw, s0)


def _logf_from_small(sm, fb):
    lane = _iota2(sm.shape, 1)
    return jnp.where(lane < H_FOX, -_softplus(-(sm + fb)), 0.0)


def _foxprep_kernel(qkv_ref, sm_ref, fb_ref, qa_ref, ka_ref, vt_ref, ko_ref, vo_ref, lf_ref,
                    carry_ref):
    T = sm_ref.shape[0]
    H = qa_ref.shape[0]
    HD = ko_ref.shape[1]
    W = H * HD
    t = pl.program_id(1)

    @pl.when(t == 0)
    def _():
        carry_ref[...] = jnp.zeros_like(carry_ref)

    logf = _logf_from_small(sm_ref[...], fb_ref[...])
    lf_ref[...] = logf
    ltri = jnp.where(_iota2((T, T), 0) >= _iota2((T, T), 1), 1.0, 0.0).astype(BF16)
    fc = _dot_sel_l(ltri, logf) + carry_ref[0:1, :]
    carry_ref[0:1, :] = fc[T - 1:T, :]
    f_hi, f_mid, f_lo = [p.astype(F32) for p in _split3(fc * LOG2E)]

    lane = _iota2((T, LANES), 1)
    low = lane < HD
    one_q = jnp.where((lane >= HD + 3) & (lane < HD + 6), 1.0, 0.0)
    one_k = jnp.where((lane >= HD) & (lane < HD + 3), 1.0, 0.0)
    ones_rows = jnp.where(_iota2((LANES - HD, T), 0) == 0, 1.0, 0.0)
    scale = (HD ** -0.5) * LOG2E
    for p in range(H // 2):
        qc = qkv_ref[:, p * LANES:(p + 1) * LANES] * scale
        kc = qkv_ref[:, W + p * LANES:W + (p + 1) * LANES]
        kct = kc.T
        vct = qkv_ref[:, 2 * W + p * LANES:2 * W + (p + 1) * LANES].T
        for half in range(2):
            h = 2 * p + half
            if half == 1:
                qh, kh = [pltpu.roll(a, HD, 1) for a in (qc, kc)]
            else:
                qh, kh = qc, kc
            vht = vct[half * HD:(half + 1) * HD, :]
            c_hi, c_mid, c_lo = [a[:, h:h + 1] for a in (f_hi, f_mid, f_lo)]
            ex_q = jnp.where(lane == HD, c_hi,
                             jnp.where(lane == HD + 1, c_mid,
                                       jnp.where(lane == HD + 2, c_lo, one_q)))
            ex_k = jnp.where(lane == HD + 3, -c_hi,
                             jnp.where(lane == HD + 4, -c_mid,
                                       jnp.where(lane == HD + 5, -c_lo, one_k)))
            qa_ref[h] = jnp.where(low, qh, ex_q).astype(BF16)
            ka_ref[h] = jnp.where(low, kh, ex_k).astype(BF16)
            vt_ref[h] = jnp.concatenate([vht, ones_rows], axis=0).astype(BF16)
            ko_ref[h] = kct[half * HD:(half + 1) * HD, :]
            vo_ref[h] = vht


def _foxprep(proj, small, col, fb_row, l, k_prev, v_prev, L, T):
    B, S, _ = proj.shape
    H = H_FOX
    HD = 512 // H
    aug = jax.ShapeDtypeStruct((B, H, S, LANES), BF16)
    augt = jax.ShapeDtypeStruct((B, H, LANES, S), BF16)
    kv = jax.ShapeDtypeStruct((L, B, H, HD, S), F32)
    aug_spec = pl.BlockSpec((None, H, T, LANES), lambda b, t: (b, 0, t, 0))
    augt_spec = pl.BlockSpec((None, H, LANES, T), lambda b, t: (b, 0, 0, t))
    kv_spec = pl.BlockSpec((None, None, H, HD, T), lambda b, t: (l, b, 0, 0, t))
    in_specs = [pl.BlockSpec((None, T, 3 * H * HD), lambda b, t: (b, t, col)),
                pl.BlockSpec((None, T, LANES), lambda b, t: (b, t, 0)),
                pl.BlockSpec((1, LANES), lambda b, t: (0, 0)),
                pl.BlockSpec(memory_space=pl.ANY), pl.BlockSpec(memory_space=pl.ANY)]
    args = [proj, small, fb_row, k_prev, v_prev]
    aliases = {3: 3, 4: 4}
    kernel = lambda a, b, c, _k, _v, *rest: _foxprep_kernel(a, b, c, *rest)
    return pl.pallas_call(
        kernel,
        grid=(B, S // T),
        in_specs=in_specs,
        out_specs=[aug_spec, aug_spec, augt_spec, kv_spec, kv_spec,
                   pl.BlockSpec((None, T, LANES), lambda b, t: (b, t, 0))],
        out_shape=[aug, aug, augt, kv, kv, jax.ShapeDtypeStruct((B, S, LANES), F32)],
        scratch_shapes=[pltpu.VMEM((8, LANES), F32)],
        input_output_aliases=aliases,
        compiler_params=_cparams(("parallel", "arbitrary")),
        name="foxprep",
    )(*args)


FLASH_QS = 256
FLASH_AHEAD1 = 3


def _flash_kernel(qi_ref, kj_ref, qa_ref, ka_ref, vt_ref, z_ref, o_ref,
                  s_scr, p_scr, m_scr, acc_scr):
    TQ = qa_ref.shape[1]
    TK = ka_ref.shape[1]
    QS = FLASH_QS
    HD = LANES // 2
    nqs = TQ // QS
    st = pl.program_id(2)
    i = qi_ref[st]
    j = kj_ref[st]
    nt = (((1,), (1,)), ((), ()))

    @pl.when(j == 0)
    def _():
        m_scr[...] = jnp.full(m_scr.shape, -jnp.inf, F32)
        acc_scr[...] = jnp.zeros_like(acc_scr)

    def unit(n):
        return n // 2, n % 2

    def scores(n, nk, diag):
        qs, hh = unit(n)
        slot = n % FLASH_AHEAD1
        q = qa_ref[hh, pl.ds(pl.multiple_of(qs * QS, QS), QS), :]
        s = lax.dot_general(ka_ref[hh, 0:nk, :], q, nt, preferred_element_type=F32)
        if diag:
            keep = _iota2((QS, QS), 0) <= _iota2((QS, QS), 1)
            last = jnp.where(keep, s[nk - QS:nk, :], -jnp.inf)
            s_scr[slot, nk - QS:nk, :] = last
            m_cur = jnp.max(last, axis=0, keepdims=True)
            if nk > QS:
                s_scr[slot, 0:nk - QS, :] = s[0:nk - QS, :]
                m_cur = jnp.maximum(m_cur, jnp.max(s[0:nk - QS, :], axis=0, keepdims=True))
            return m_cur
        s_scr[slot, 0:nk, :] = s
        return jnp.max(s, axis=0, keepdims=True)

    def softmax(n, nk, m_cur):
        qs, hh = unit(n)
        m_prev = m_scr[hh, qs, 0:1, :]
        m_new = jnp.maximum(m_prev, m_cur)
        p_scr[n % 2, 0:nk, :] = jnp.exp2(s_scr[n % FLASH_AHEAD1, 0:nk, :] - m_new).astype(BF16)
        m_scr[hh, qs, 0:1, :] = m_new
        return jnp.exp2(m_prev - m_new)

    def pv(n, nk, alpha):
        qs, hh = unit(n)
        o = jnp.dot(vt_ref[hh, :, 0:nk], p_scr[n % 2, 0:nk, :], preferred_element_type=F32)
        acc_scr[hh, qs] = acc_scr[hh, qs] * alpha + o

    ahead = FLASH_AHEAD1 - 1
    nu = 2 * nqs

    @pl.when(j < i)
    def _():
        head = tuple(scores(n, TK, False) for n in range(ahead))

        def body(n, m_curs):
            alpha = softmax(n, TK, m_curs[0])
            m_next = scores(n + ahead, TK, False)
            pv(n, TK, alpha)
            return m_curs[1:] + (m_next,)

        tail = lax.fori_loop(0, nu - ahead, body, head, unroll=True)
        for k in range(ahead):
            n = nu - ahead + k
            pv(n, TK, softmax(n, TK, tail[k]))

    @pl.when(j == i)
    def _():
        nks = [(n // 2 + 1) * QS for n in range(nu)]
        m_curs = [scores(n, nks[n], True) for n in range(ahead)]
        for n in range(nu):
            alpha = softmax(n, nks[n], m_curs[n])
            if n + ahead < nu:
                m_curs.append(scores(n + ahead, nks[n + ahead], True))
            pv(n, nks[n], alpha)
        lane = _iota2((QS, LANES), 1)
        for qs in range(nqs):
            a0 = acc_scr[0, qs]
            a1 = acc_scr[1, qs]
            o0 = (a0 / a0[HD:HD + 1, :]).T
            o1 = (a1 / a1[HD:HD + 1, :]).T
            o = jnp.where(lane < HD, o0, pltpu.roll(o1, HD, 1))
            z = z_ref[qs * QS:(qs + 1) * QS, :]
            o_ref[qs * QS:(qs + 1) * QS, :] = (o * _silu(z)).astype(o_ref.dtype)


def _flash(qa, ka, vt, proj, zcol, tq):
    B, H, S, _ = qa.shape
    n = S // tq
    steps = [(i, j) for i in range(n) for j in range(i + 1)]
    qi = jnp.asarray([s[0] for s in steps], jnp.int32)
    kj = jnp.asarray([s[1] for s in steps], jnp.int32)
    nqs = tq // FLASH_QS
    grid_spec = pltpu.PrefetchScalarGridSpec(
        num_scalar_prefetch=2,
        grid=(B, H // 2, len(steps)),
        in_specs=[pl.BlockSpec((None, 2, tq, LANES), lambda b, p, s, qi, kj: (b, p, qi[s], 0)),
                  pl.BlockSpec((None, 2, tq, LANES), lambda b, p, s, qi, kj: (b, p, kj[s], 0)),
                  pl.BlockSpec((None, 2, LANES, tq), lambda b, p, s, qi, kj: (b, p, 0, kj[s])),
                  pl.BlockSpec((None, tq, LANES), lambda b, p, s, qi, kj: (b, qi[s], zcol + p))],
        out_specs=pl.BlockSpec((None, tq, LANES), lambda b, p, s, qi, kj: (b, qi[s], p)),
        scratch_shapes=[pltpu.VMEM((FLASH_AHEAD1, tq, FLASH_QS), F32),
                        pltpu.VMEM((2, tq, FLASH_QS), BF16),
                        pltpu.VMEM((2, nqs, 8, FLASH_QS), F32),
                        pltpu.VMEM((2, nqs, LANES, FLASH_QS), F32)])
    return pl.pallas_call(
        _flash_kernel,
        grid_spec=grid_spec,
        out_shape=jax.ShapeDtypeStruct((B, S, H * LANES // 2), BF16),
        compiler_params=_cparams(("parallel", "parallel", "arbitrary")),
        name="flash",
    )(qi, kj, qa, ka, vt, proj)


def _cumsum_kernel(x_ref, o_ref, *, group):
    R = x_ref.shape[0]
    x = x_ref[...]
    up = jnp.where(_iota2((LANES, LANES), 0) <= _iota2((LANES, LANES), 1), 1.0, 0.0).astype(BF16)
    w = _dot_sel_r(x, up)
    tot = jnp.broadcast_to(w[:, LANES - 1:LANES], (R, LANES))
    ri = _iota2((R, R), 0)
    ci = _iota2((R, R), 1)
    prev = jnp.where((ci < ri) & ((ci // group) == (ri // group)), 1.0, 0.0).astype(BF16)
    o_ref[...] = w + _dot_sel_l(prev, tot)


def _cumsum_rows(x, group):
    R = x.shape[0]
    rb = min(R, 256)
    return pl.pallas_call(
        functools.partial(_cumsum_kernel, group=group),
        grid=(R // rb,),
        in_specs=[pl.BlockSpec((rb, LANES), lambda r: (r, 0))],
        out_specs=pl.BlockSpec((rb, LANES), lambda r: (r, 0)),
        out_shape=jax.ShapeDtypeStruct((R, LANES), F32),
        compiler_params=_cparams(("parallel",)),
        name="cumsum",
    )(x)


def _foxsamp_kernel(q_ref, k_ref, v_ref, z_ref, sm_ref, fb_ref, tot_ref, fcc_ref, ck_ref, cv_ref,
                    o_ref, ko_ref, vo_ref, lf_ref):
    S = q_ref.shape[0]
    HD = ck_ref.shape[1]
    p = pl.program_id(1)
    logf = _logf_from_small(sm_ref[...], fb_ref[...])
    lf_ref[...] = logf
    ltri = jnp.where(_iota2((S, S), 0) >= _iota2((S, S), 1), 1.0, 0.0).astype(BF16)
    fc = _dot_sel_l(ltri, logf) + tot_ref[...]
    fct = jnp.concatenate([fc, jnp.zeros((LANES - S, LANES), F32)], axis=0).T
    lane = _iota2((S, LANES), 1)
    row = _iota2((LANES, LANES), 0)
    causal = _iota2((S, S), 0) >= _iota2((S, S), 1)
    scale = HD ** -0.5
    for hh in range(2):
        h = 2 * p + hh
        fq = jnp.sum(jnp.where(lane == h, fc, 0.0), axis=1, keepdims=True)
        fkn = jnp.sum(jnp.where(row == h, fct, 0.0), axis=0, keepdims=True)[:, :S]
        q = q_ref[:, hh * HD:(hh + 1) * HD] * scale
        kn = k_ref[:, hh * HD:(hh + 1) * HD]
        vn = v_ref[:, hh * HD:(hh + 1) * HD]
        ko_ref[hh] = kn
        vo_ref[hh] = vn
        sc = _dot(q, ck_ref[hh]) + fq - fcc_ref[hh]
        sn = jnp.where(causal, _dot_nt(q, kn) + fq - fkn, -jnp.inf)
        m = jnp.maximum(jnp.max(sc, axis=1, keepdims=True), jnp.max(sn, axis=1, keepdims=True))
        pc = jnp.exp(sc - m)
        pn = jnp.exp(sn - m)
        l = jnp.sum(pc, axis=1, keepdims=True) + jnp.sum(pn, axis=1, keepdims=True)
        o = (_dot_nt(pc, cv_ref[hh]) + _dot(pn, vn)) / l
        o_ref[:, hh * HD:(hh + 1) * HD] = o * _silu(z_ref[:, hh * HD:(hh + 1) * HD])


def _foxsamp(proj, small, cols, fb_row, tot, fcc, cache_k, cache_v, l):
    B, S, _ = proj.shape
    _, _, H, HD, P = cache_k.shape
    cq, ck, cv, cz = cols
    pair = lambda c: pl.BlockSpec((None, S, LANES), lambda b, p: (b, 0, c + p))
    kv_out = jax.ShapeDtypeStruct((B, H, S, HD), F32)
    return pl.pallas_call(
        _foxsamp_kernel,
        grid=(B, H // 2),
        in_specs=[pair(cq), pair(ck), pair(cv), pair(cz),
                  pl.BlockSpec((None, S, LANES), lambda b, p: (b, 0, 0)),
                  pl.BlockSpec((1, LANES), lambda b, p: (0, 0)),
                  pl.BlockSpec((None, 1, LANES), lambda b, p: (b, 0, 0)),
                  pl.BlockSpec((None, 2, 1, P), lambda b, p: (b, p, 0, 0)),
                  pl.BlockSpec((None, None, 2, HD, P), lambda b, p: (l, b, p, 0, 0)),
                  pl.BlockSpec((None, None, 2, HD, P), lambda b, p: (l, b, p, 0, 0))],
        out_specs=[pl.BlockSpec((None, S, LANES), lambda b, p: (b, 0, p)),
                   pl.BlockSpec((None, 2, S, HD), lambda b, p: (b, p, 0, 0)),
                   pl.BlockSpec((None, 2, S, HD), lambda b, p: (b, p, 0, 0)),
                   pl.BlockSpec((None, S, LANES), lambda b, p: (b, 0, 0))],
        out_shape=[jax.ShapeDtypeStruct((B, S, H * HD), F32), kv_out, kv_out,
                   jax.ShapeDtypeStruct((B, S, LANES), F32)],
        compiler_params=_cparams(("parallel", "arbitrary")),
        name="foxsamp",
    )(proj, proj, proj, proj, small, fb_row, tot, fcc, cache_k, cache_v)


COL_MG = 0
COL_GQKV = 6
COL_FQKV = 9
COL_RQK = 12
COL_RV = 13
COL_RZ = 14
COL_FZ = 15
COL_GZ = 16


def _prep_w_in(w_in, D):
    BW = D // 2
    hk = BW // 2
    splits = (hk, hk, BW, BW, BW, BW, BW, H_FOX, BW, BW, BW, BW, H_GDN, H_GDN, BW, N_BRANCH * D)
    offs = np.cumsum((0,) + splits)
    seg = lambda i: w_in[..., offs[i]:offs[i + 1]]
    rq, rk, rv, rz, fq, fk, fv, ff, fz, gq, gk, gv, ga, gb, gz, mg = [seg(i) for i in range(16)]
    main = jnp.concatenate([mg, gq, gk, gv, fq, fk, fv, rq, rk, rv, rz, fz, gz], axis=-1)
    pad = jnp.zeros(w_in.shape[:-1] + (LANES - H_FOX - 2 * H_GDN,), w_in.dtype)
    small = jnp.concatenate([ff, ga, gb, pad], axis=-1)
    return main.astype(BF16), small.astype(BF16)


def _lane_row(vals, off):
    L, n = vals.shape
    return jnp.zeros((L, 1, LANES), F32).at[:, 0, off:off + n].set(vals)


def kernel(x_prompt, x_sample, c_prompt, c_sample, cache_fox_k, cache_fox_v, cache_fox_logf,
           state_ret, state_gdn, state_gdn_conv, norm_w, ada_w, ada_b, w_in, fox_f_bias,
           gdn_a_log, gdn_dt_bias, gdn_conv_w, ret_norm_w, gdn_norm_w, w_branch, w_out,
           final_norm_w):
    B, S, D = x_prompt.shape
    BS, SS, _ = x_sample.shape
    L = ada_w.shape[0]
    P = cache_fox_k.shape[3]
    DK_RET, DV_RET = state_ret.shape[-2:]
    DK_GDN, DV_GDN = state_gdn.shape[-2:]

    rows = -(-(B + BS) // 8) * 8
    c_all = jnp.concatenate([c_prompt, c_sample, jnp.zeros((rows - B - BS, D), F32)], axis=0)
    mod = _adaln(c_all, ada_w, ada_b)
    mod_p = mod[:, :, :B].reshape(L, 3, B, 1, D)
    mod_s = mod[:, :, B:B + BS].reshape(L, 3, BS, 1, D)

    w_main, w_small = _prep_w_in(w_in, D)
    w_branch_b = w_branch.astype(BF16)
    w_out_b = w_out.astype(BF16)
    norm_w3 = norm_w.reshape(L, 1, D)
    final_w = final_norm_w.reshape(1, D)
    ret_nw = ret_norm_w.reshape(L, H_RET, 1, DV_RET)
    gdn_nw = gdn_norm_w.reshape(L, 1, DV_GDN)
    fb_rows = _lane_row(fox_f_bias, 0)
    a_rows = _lane_row(gdn_a_log, H_FOX)
    dt_rows = _lane_row(gdn_dt_bias, H_FOX)

    tm_p = min(S, 1024)
    tmm_p = min(S, 512)
    t_ret = min(S, 256)
    t_gdn = min(S, 256)
    t_prep = min(S, 256)
    t_flash = min(S, 1024)

    tabs_p = _ret_tables(t_ret, 0, S, DK_RET)
    tabs_s = _ret_tables(SS, P, SS, DK_RET)

    fcc = _cumsum_rows(cache_fox_logf.reshape(-1, LANES), P // LANES).reshape(L, BS, H_FOX, 1, P)
    tot = jnp.zeros((L, BS, 1, LANES), F32).at[:, :, 0, :H_FOX].set(fcc[:, :, :, 0, P - 1])

    zeros_ret = jnp.zeros((B, H_RET, DK_RET, DV_RET), F32)
    zeros_gdn = jnp.zeros((B, H_GDN, DK_GDN, DV_GDN), F32)
    zeros_conv = jnp.zeros((B, CONV_W - 1, gdn_conv_w.shape[-1]), F32)

    xp, xs = x_prompt, x_sample
    cache_kt = jnp.swapaxes(cache_fox_k, 3, 4)
    cache_vt = jnp.swapaxes(cache_fox_v, 3, 4)
    pk = jnp.zeros((L, B, H_FOX, cache_fox_k.shape[-1], S), F32)
    pv = jnp.zeros_like(pk)
    p_lf, p_ret, p_gdn, p_conv = [], [], [], []
    s_k, s_v, s_lf, s_ret, s_gdn, s_conv = [], [], [], [], [], []
    for l in range(L):
        final = l == L - 1
        proj, small = _inproj(xp, mod_p, l, norm_w3, w_main, w_small, 1, tm_p)
        o_ret, st = _ret(proj, (COL_RQK, COL_RV, COL_RZ), tabs_p, ret_nw[l], zeros_ret, t_ret, BF16)
        p_ret.append(st)
        o_gdn, st, cv = _gdn(proj, small, (COL_GQKV // 3, COL_GZ), gdn_conv_w[l], zeros_conv,
                             a_rows[l], dt_rows[l], gdn_nw[l], zeros_gdn, t_gdn, BF16)
        p_gdn.append(st)
        p_conv.append(cv)
        qa, ka, va, pk, pv, lf = _foxprep(proj, small, COL_FQKV // 3, fb_rows[l], l, pk, pv, L, t_prep)
        p_lf.append(lf)
        o_fox = _flash(qa, ka, va, proj, COL_FZ * 4, t_flash)
        xp = _merge(xp, mod_p, l, o_ret, o_fox, o_gdn, proj, w_branch_b, w_out_b, final_w,
                    1, tmm_p, final)
        proj, small = _inproj(xs, mod_s, l, norm_w3, w_main, w_small, BS, SS)
        o_ret, st = _ret(proj, (COL_RQK, COL_RV, COL_RZ), tabs_s, ret_nw[l], state_ret[l], SS, F32)
        s_ret.append(st)
        o_gdn, st, cv = _gdn(proj, small, (COL_GQKV // 3, COL_GZ), gdn_conv_w[l], state_gdn_conv[l],
                             a_rows[l], dt_rows[l], gdn_nw[l], state_gdn[l], SS, F32)
        s_gdn.append(st)
        s_conv.append(cv)
        o_fox, kk, vv, lf = _foxsamp(proj, small,
                                     (COL_FQKV * 4, (COL_FQKV + 1) * 4, (COL_FQKV + 2) * 4, COL_FZ * 4),
                                     fb_rows[l], tot[l], fcc[l], cache_kt, cache_vt, l)
        s_k.append(kk)
        s_v.append(vv)
        s_lf.append(lf)
        xs = _merge(xs, mod_s, l, o_ret, o_fox, o_gdn, proj, w_branch_b, w_out_b, final_w,
                    BS, SS, final)

    logf_out = lambda lfs: jnp.stack(lfs)[..., :H_FOX].transpose(0, 1, 3, 2)
    return (xp, xs,
            jnp.swapaxes(pk, 3, 4), jnp.swapaxes(pv, 3, 4), logf_out(p_lf), jnp.stack(p_ret), jnp.stack(p_gdn), jnp.stack(p_conv),
            jnp.stack(s_k), jnp.stack(s_v), logf_out(s_lf), jnp.stack(s_ret), jnp.stack(s_gdn),
            jnp.stack(s_conv))
```

```python
import functools

import numpy as np
import jax
import jax.numpy as jnp
from jax import lax
from jax.experimental import pallas as pl
from jax.experimental.pallas import tpu as pltpu

F32 = jnp.float32
BF16 = jnp.bfloat16

N_BRANCH = 3
H_RET = 4
H_FOX = 8
H_GDN = 4
CONV_W = 4
CHUNK = 64
NORM_EPS = 1e-6
ROPE_BASE = 10000.0
LOG2E = 1.4426950408889634
LANES = 128
GDN_CHUNK = 128
VMEM_LIMIT = 56 * 1024 * 1024


def _cparams(sem):
    return pltpu.CompilerParams(dimension_semantics=sem, vmem_limit_bytes=VMEM_LIMIT)


def _sigmoid(x):
    return 0.5 * jnp.tanh(0.5 * x) + 0.5


def _silu(x):
    return x * _sigmoid(x)


def _softplus(x):
    return jnp.maximum(x, 0.0) + jnp.log(1.0 + jnp.exp(-jnp.abs(x)))


def _dot(a, b):
    return jnp.dot(a.astype(BF16), b.astype(BF16), preferred_element_type=F32)


def _dot_nt(a, b):
    return lax.dot_general(a.astype(BF16), b.astype(BF16), (((1,), (1,)), ((), ())),
                           preferred_element_type=F32)


def _split3(x):
    hi = x.astype(BF16)
    r = x - hi.astype(F32)
    mid = r.astype(BF16)
    lo = (r - mid.astype(F32)).astype(BF16)
    return hi, mid, lo


def _dot_sel_l(m, x):
    hi, mid, lo = _split3(x)
    d = lambda y: jnp.dot(m, y, preferred_element_type=F32)
    return d(hi) + (d(mid) + d(lo))


def _dot_sel_r(x, m):
    hi, mid, lo = _split3(x)
    d = lambda y: jnp.dot(y, m, preferred_element_type=F32)
    return d(hi) + (d(mid) + d(lo))


def _iota2(shape, dim):
    return lax.broadcasted_iota(jnp.int32, shape, dim)


def _adaln_kernel(c_ref, w_ref, b_ref, o_ref):
    s = _silu(c_ref[...])
    o_ref[...] = _dot(s, w_ref[...]) + b_ref[...]


def _adaln(c_all, ada_w, ada_b):
    L, D, _ = ada_w.shape
    R = c_all.shape[0]
    return pl.pallas_call(
        _adaln_kernel,
        grid=(L, 3),
        in_specs=[pl.BlockSpec((R, D), lambda l, k: (0, 0)),
                  pl.BlockSpec((None, D, D), lambda l, k: (l, 0, k)),
                  pl.BlockSpec((None, None, 1, D), lambda l, k: (l, k, 0, 0))],
        out_specs=pl.BlockSpec((None, None, R, D), lambda l, k: (l, k, 0, 0)),
        out_shape=jax.ShapeDtypeStruct((L, 3, R, D), F32),
        compiler_params=_cparams(("parallel", "parallel")),
        name="adaln",
    )(c_all, ada_w, ada_b.reshape(L, 3, 1, D))


PROJ_TN = 512
PROJ_DTYPE = BF16


def _inproj_kernel(x_ref, nw_ref, sc_ref, sh_ref, w_ref, ws_ref, o_ref, os_ref, h_ref):
    bb, tm, D = x_ref.shape

    @pl.when(pl.program_id(2) == 0)
    def _():
        x = x_ref[...]
        ms = jnp.mean(x * x, axis=-1, keepdims=True)
        xn = (x * lax.rsqrt(ms + NORM_EPS)) * nw_ref[...]
        h = xn * (1.0 + sc_ref[...]) + sh_ref[...]
        hb = h.astype(BF16).reshape(bb * tm, D)
        h_ref[...] = hb
        os_ref[...] = jnp.dot(hb, ws_ref[...], preferred_element_type=F32).reshape(bb, tm, LANES)

    o_ref[...] = jnp.dot(h_ref[...], w_ref[...],
                         preferred_element_type=F32).reshape(bb, tm, PROJ_TN).astype(o_ref.dtype)


def _inproj(x, mod, l, norm_w, w_main, w_small, bb, tm):
    B, S, D = x.shape
    NP = w_main.shape[-1]
    nj = NP // PROJ_TN
    return pl.pallas_call(
        _inproj_kernel,
        grid=(B // bb, S // tm, nj),
        in_specs=[pl.BlockSpec((bb, tm, D), lambda b, i, j: (b, i, 0)),
                  pl.BlockSpec((None, 1, D), lambda b, i, j: (l, 0, 0)),
                  pl.BlockSpec((None, None, bb, 1, D), lambda b, i, j: (l, 1, b, 0, 0)),
                  pl.BlockSpec((None, None, bb, 1, D), lambda b, i, j: (l, 0, b, 0, 0)),
                  pl.BlockSpec((None, D, PROJ_TN), lambda b, i, j: (l, 0, j)),
                  pl.BlockSpec((None, D, LANES), lambda b, i, j: (l, 0, 0))],
        out_specs=[pl.BlockSpec((bb, tm, PROJ_TN), lambda b, i, j: (b, i, j)),
                   pl.BlockSpec((bb, tm, LANES), lambda b, i, j: (b, i, 0))],
        out_shape=[jax.ShapeDtypeStruct((B, S, NP), PROJ_DTYPE),
                   jax.ShapeDtypeStruct((B, S, LANES), F32)],
        scratch_shapes=[pltpu.VMEM((bb * tm, D), BF16)],
        compiler_params=_cparams(("parallel", "parallel", "arbitrary")),
        name="inproj",
    )(x, norm_w, mod, mod, w_main, w_small)


def _merge_kernel(x_ref, g_ref, o1_ref, o2_ref, o3_ref, mg_ref, wb_ref, wo_ref, fw_ref, y_ref,
                  *, final):
    bb, tm, D = x_ref.shape
    M = bb * tm
    acc = None
    for n, o_ref in enumerate((o1_ref, o2_ref, o3_ref)):
        br = jnp.dot(o_ref[...].reshape(M, o_ref.shape[-1]).astype(BF16), wb_ref[n],
                     preferred_element_type=F32)
        gate = _sigmoid(mg_ref[:, :, n * D:(n + 1) * D].astype(F32).reshape(M, D))
        term = br * gate
        acc = term if acc is None else acc + term
    out = jnp.dot(acc.astype(BF16), wo_ref[...], preferred_element_type=F32)
    xn = x_ref[...] + g_ref[...] * out.reshape(bb, tm, D)
    if final:
        ms = jnp.mean(xn * xn, axis=-1, keepdims=True)
        xn = (xn * lax.rsqrt(ms + NORM_EPS)) * fw_ref[...]
    y_ref[...] = xn


def _merge(x, mod, l, o_ret, o_fox, o_gdn, proj, w_branch, w_out, final_w, bb, tm, final):
    B, S, D = x.shape
    W = o_ret.shape[-1]
    return pl.pallas_call(
        functools.partial(_merge_kernel, final=final),
        grid=(B // bb, S // tm),
        in_specs=[pl.BlockSpec((bb, tm, D), lambda b, i: (b, i, 0)),
                  pl.BlockSpec((None, None, bb, 1, D), lambda b, i: (l, 2, b, 0, 0)),
                  pl.BlockSpec((bb, tm, W), lambda b, i: (b, i, 0)),
                  pl.BlockSpec((bb, tm, W), lambda b, i: (b, i, 0)),
                  pl.BlockSpec((bb, tm, W), lambda b, i: (b, i, 0)),
                  pl.BlockSpec((bb, tm, N_BRANCH * D), lambda b, i: (b, i, 0)),
                  pl.BlockSpec((None, N_BRANCH, W, D), lambda b, i: (l, 0, 0, 0)),
                  pl.BlockSpec((None, D, D), lambda b, i: (l, 0, 0)),
                  pl.BlockSpec((1, D), lambda b, i: (0, 0))],
        out_specs=pl.BlockSpec((bb, tm, D), lambda b, i: (b, i, 0)),
        out_shape=jax.ShapeDtypeStruct((B, S, D), F32),
        compiler_params=_cparams(("parallel", "parallel")),
        name="merge",
    )(x, mod, o_ret, o_fox, o_gdn, proj, w_branch, w_out, final_w)


def _ret_kernel(qk_ref, v_ref, z_ref, cos_ref, sin_ref, m_ref, dq_ref, dk_ref, dc_ref, nw_ref,
                s0_ref, o_ref, so_ref, s_ref):
    T = qk_ref.shape[0]
    H = s_ref.shape[0]
    DK = s0_ref.shape[1]
    t = pl.program_id(1)

    @pl.when(t == 0)
    def _():
        zero = jnp.zeros((DK, LANES), F32)
        for h in range(H):
            s0 = s0_ref[h]
            s_ref[h] = jnp.concatenate([s0, zero] if h % 2 == 0 else [zero, s0], axis=0)

    cos = cos_ref[...]
    sin = sin_ref[...]
    lane = _iota2((T, LANES), 1)
    first = (lane % DK) < (DK // 2)
    low = lane < DK

    def rot(x):
        sw = jnp.where(first, pltpu.roll(x, LANES - DK // 2, 1), pltpu.roll(x, DK // 2, 1))
        return x * cos + sw * sin

    nkq = H * DK
    hs = range(H)
    qcs = [rot(qk_ref[:, p * LANES:(p + 1) * LANES].astype(F32)) for p in range(H // 2)]
    kcs = [rot(qk_ref[:, nkq + p * LANES:nkq + (p + 1) * LANES].astype(F32)) * (DK ** -0.5)
           for p in range(H // 2)]
    kcbs = [kc.astype(BF16) for kc in kcs]
    sels = [low if h % 2 == 0 else jnp.logical_not(low) for h in hs]
    qms = [jnp.where(sels[h], qcs[h // 2], 0.0) for h in hs]
    vhs = [v_ref[:, h * LANES:(h + 1) * LANES].astype(BF16) for h in hs]
    ss = [s_ref[h] for h in hs]
    ams = [_dot_nt(qms[h], kcbs[h // 2]) * m_ref[h] for h in hs]
    kts = [(jnp.where(sels[h], kcs[h // 2], 0.0) * dk_ref[h]).T for h in hs]
    os_ = [_dot(ams[h], vhs[h]) + _dot(qms[h] * dq_ref[h], ss[h]) for h in hs]
    for h in hs:
        s_ref[h] = ss[h] * dc_ref[h] + _dot(kts[h], vhs[h])
    for h in hs:
        o = os_[h]
        ms = jnp.mean(o * o, axis=-1, keepdims=True)
        on = (o * lax.rsqrt(ms + NORM_EPS)) * nw_ref[h]
        z = z_ref[:, h * LANES:(h + 1) * LANES].astype(F32)
        o_ref[:, h * LANES:(h + 1) * LANES] = (on * _silu(z)).astype(o_ref.dtype)

    @pl.when(t == pl.num_programs(1) - 1)
    def _():
        for h in range(H):
            so_ref[h] = s_ref[h][(h % 2) * DK:(h % 2 + 1) * DK, :]


def _ret_tables(T, pos0, S, DK):
    half = DK // 2
    inv_freq = ROPE_BASE ** (-jnp.arange(half, dtype=F32) / half)
    pos = pos0 + jnp.arange(S)
    ang = pos.astype(F32)[:, None] * inv_freq[None, :]
    cos, sin = jnp.cos(ang), jnp.sin(ang)
    reps = LANES // DK
    cos_t = jnp.tile(jnp.concatenate([cos, cos], axis=1), (1, reps))
    sin_t = jnp.tile(jnp.concatenate([-sin, sin], axis=1), (1, reps))
    log_g = jnp.log1p(-jnp.exp2(-5.0 - jnp.arange(H_RET, dtype=F32)))
    i = jnp.arange(T)
    dist = jnp.abs(i[:, None] - i[None, :]).astype(F32)
    allowed = (i[None, :] // CHUNK) <= (i[:, None] // CHUNK)
    m = jnp.where(allowed[None], jnp.exp(dist[None] * log_g[:, None, None]), 0.0)
    fi = i.astype(F32)
    dq = jnp.exp((fi[None, :] + 1.0) * log_g[:, None])
    dk = jnp.exp((T - 1.0 - fi)[None, :] * log_g[:, None])
    dc = jnp.exp(T * log_g)
    bc = lambda a: jnp.broadcast_to(a[..., None], a.shape + (LANES,))
    return cos_t, sin_t, m, bc(dq), bc(dk), bc(dc[:, None])


def _ret(proj, cols, tabs, ret_nw, s0, T, out_dtype):
    B, S, _ = proj.shape
    H, DK, DV = s0.shape[1:]
    W = H * DV
    cqk, cv, cz = cols
    cos_t, sin_t, m, dq, dk, dc = tabs
    return pl.pallas_call(
        _ret_kernel,
        grid=(B, S // T),
        in_specs=[pl.BlockSpec((None, T, W), lambda b, t: (b, t, cqk)),
                  pl.BlockSpec((None, T, W), lambda b, t: (b, t, cv)),
                  pl.BlockSpec((None, T, W), lambda b, t: (b, t, cz)),
                  pl.BlockSpec((T, LANES), lambda b, t: (t, 0)),
                  pl.BlockSpec((T, LANES), lambda b, t: (t, 0)),
                  pl.BlockSpec((H, T, T), lambda b, t: (0, 0, 0)),
                  pl.BlockSpec((H, T, LANES), lambda b, t: (0, 0, 0)),
                  pl.BlockSpec((H, T, LANES), lambda b, t: (0, 0, 0)),
                  pl.BlockSpec((H, 1, LANES), lambda b, t: (0, 0, 0)),
                  pl.BlockSpec((H, 1, DV), lambda b, t: (0, 0, 0)),
                  pl.BlockSpec((None, H, DK, DV), lambda b, t: (b, 0, 0, 0))],
        out_specs=[pl.BlockSpec((None, T, W), lambda b, t: (b, t, 0)),
                   pl.BlockSpec((None, H, DK, DV), lambda b, t: (b, 0, 0, 0))],
        out_shape=[jax.ShapeDtypeStruct((B, S, W), out_dtype),
                   jax.ShapeDtypeStruct((B, H, DK, DV), F32)],
        scratch_shapes=[pltpu.VMEM((H, LANES, DV), F32)],
        compiler_params=_cparams(("parallel", "arbitrary")),
        name="ret",
    )(proj, proj, proj, cos_t, sin_t, m, dq, dk, dc, ret_nw, s0)


INV_BASE = 8


def _unit_lower_inverses(mats, eye, ri, ci):
    n = mats[0].shape[0]
    b = INV_BASE
    d = lambda x, y: jnp.dot(x, y, preferred_element_type=F32)
    same = (ri // b) == (ci // b)
    ads = [jnp.where(same, a, 0.0) for a in mats]
    xs = [eye - ad for ad in ads]
    ps = [ad.astype(BF16) for ad in ads]
    k = 2
    while k < b:
        ps = [d(p, p).astype(BF16) for p in ps]
        xs = [x + d(x.astype(BF16), p) for x, p in zip(xs, ps)]
        k *= 2
    while b < n:
        off = ((ri // (2 * b)) == (ci // (2 * b))) & ((ri // b) != (ci // b))
        xbs = [x.astype(BF16) for x in xs]
        ts = [d(jnp.where(off, a, 0.0).astype(BF16), xb) for a, xb in zip(mats, xbs)]
        xs = [x - d(xb, t.astype(BF16)) for x, xb, t in zip(xs, xbs, ts)]
        b *= 2
    return xs


def _gdn_kernel(u_ref, z_ref, sm_ref, cw_ref, cb_ref, al_ref, dt_ref, nw_ref, s0_ref,
                o_ref, so_ref, co_ref, ext_ref, s_ref):
    T, C3 = u_ref.shape
    H = s_ref.shape[0]
    DK = s_ref.shape[1]
    W = H * DK
    C = GDN_CHUNK
    TP = -(-T // C) * C
    t = pl.program_id(1)
    last = pl.num_programs(1) - 1
    base = 8
    lo = base - (CONV_W - 1)

    @pl.when(t == 0)
    def _():
        s_ref[...] = s0_ref[...]
        ext_ref[lo:base, :] = cb_ref[...]

    ext_ref[base:base + T, :] = u_ref[...].astype(F32)
    y = ext_ref[lo:lo + T, :] * cw_ref[0:1, :]
    for j in range(1, CONV_W):
        y = y + ext_ref[lo + j:lo + j + T, :] * cw_ref[j:j + 1, :]
    tail = ext_ref[lo + T:base + T, :]
    ext_ref[lo:base, :] = tail

    @pl.when(t == last)
    def _():
        co_ref[...] = tail

    qkv = _silu(y)
    sm = sm_ref[...]
    g_all = -jnp.exp(al_ref[...]) * _softplus(sm + dt_ref[...])
    b_all = _sigmoid(sm)
    z_all = z_ref[...].astype(F32)
    if TP > T:
        pad = lambda a: jnp.concatenate([a, jnp.zeros((TP - T, a.shape[1]), F32)], axis=0)
        qkv, g_all, b_all, z_all = pad(qkv), pad(g_all), pad(b_all), pad(z_all)

    ri = _iota2((C, C), 0)
    ci = _iota2((C, C), 1)
    tri = ri >= ci
    strict = ri > ci
    ltri = jnp.where(tri, 1.0, 0.0).astype(BF16)
    eye = jnp.where(ri == ci, 1.0, 0.0)

    nc = TP // C
    items = [(c, h) for c in range(nc) for h in range(H)]
    gcums = [_dot_sel_l(ltri, g_all[c * C:(c + 1) * C, :]) for c in range(nc)]

    qs, ks, vs, gcs, bcs, decs, kbs = [], [], [], [], [], [], []
    for c, h in items:
        r0 = c * C
        q = qkv[r0:r0 + C, h * DK:(h + 1) * DK]
        k = qkv[r0:r0 + C, W + h * DK:W + (h + 1) * DK]
        qs.append(q * lax.rsqrt(jnp.sum(q * q, axis=-1, keepdims=True) + NORM_EPS) * (DK ** -0.5))
        k = k * lax.rsqrt(jnp.sum(k * k, axis=-1, keepdims=True) + NORM_EPS)
        ks.append(k)
        kbs.append(k.astype(BF16))
        vs.append(qkv[r0:r0 + C, 2 * W + h * DK:2 * W + (h + 1) * DK])
        gc = gcums[c][:, 8 + h:9 + h]
        gcs.append(gc)
        bcs.append(b_all[r0:r0 + C, 12 + h:13 + h])
        gm = jnp.broadcast_to(gc, (C, C))
        decs.append(jnp.exp(jnp.where(tri, gm - gm.T, -jnp.inf)))
    kks = [_dot_nt(kb, kb) for kb in kbs]
    amats = [jnp.where(strict, kk * dec, 0.0) * bc for kk, dec, bc in zip(kks, decs, bcs)]
    tinvs = _unit_lower_inverses(amats, eye, ri, ci)
    egs = [jnp.exp(gc) for gc in gcs]
    rhss = [jnp.concatenate([k * (bc * eg), v * bc], axis=1)
            for k, v, bc, eg in zip(ks, vs, bcs, egs)]
    wus = [_dot(tinv, rhs) for tinv, rhs in zip(tinvs, rhss)]
    qks = [(_dot_nt(q, kb) * dec).astype(BF16) for q, kb, dec in zip(qs, kbs, decs)]
    qes = [(q * eg).astype(BF16) for q, eg in zip(qs, egs)]
    glasts = [gcums[c][C - 1:C, 8 + h:9 + h] for c, h in items]
    kdts = [(k * jnp.exp(gl - gc)).T.astype(BF16) for k, gl, gc in zip(ks, glasts, gcs)]
    sdec = [jnp.exp(gl) for gl in glasts]

    for c in range(nc):
        idx = [c * H + h for h in range(H)]
        ss = [s_ref[h] for h in range(H)]
        sbs = [s.astype(BF16) for s in ss]
        us = [wus[i][:, DK:] - _dot(wus[i][:, :DK], sb) for i, sb in zip(idx, sbs)]
        ubs = [u.astype(BF16) for u in us]
        os_ = [jnp.dot(qes[i], sb, preferred_element_type=F32)
               + jnp.dot(qks[i], ub, preferred_element_type=F32)
               for i, sb, ub in zip(idx, sbs, ubs)]
        for h, i in enumerate(idx):
            s_ref[h] = ss[h] * sdec[i] + jnp.dot(kdts[i], ubs[h], preferred_element_type=F32)
        for h, o in enumerate(os_):
            ms = jnp.mean(o * o, axis=-1, keepdims=True)
            on = (o * lax.rsqrt(ms + NORM_EPS)) * nw_ref[...]
            z = z_all[c * C:(c + 1) * C, h * DK:(h + 1) * DK]
            res = (on * _silu(z)).astype(o_ref.dtype)
            if TP > T:
                o_ref[:, h * DK:(h + 1) * DK] = res[:T, :]
            else:
                o_ref[c * C:(c + 1) * C, h * DK:(h + 1) * DK] = res

    @pl.when(t == last)
    def _():
        so_ref[...] = s_ref[...]


def _gdn(proj, small, cols, conv_w, conv_buf, a_row, dt_row, gdn_nw, s0, T, out_dtype):
    B, S, _ = proj.shape
    H, DK, DV = s0.shape[1:]
    W = H * DK
    C3 = conv_w.shape[-1]
    cu, cz = cols
    return pl.pallas_call(
        _gdn_kernel,
        grid=(B, S // T),
        in_specs=[pl.BlockSpec((None, T, C3), lambda b, t: (b, t, cu)),
                  pl.BlockSpec((None, T, W), lambda b, t: (b, t, cz)),
                  pl.BlockSpec((None, T, LANES), lambda b, t: (b, t, 0)),
                  pl.BlockSpec((CONV_W, C3), lambda b, t: (0, 0)),
                  pl.BlockSpec((None, CONV_W - 1, C3), lambda b, t: (b, 0, 0)),
                  pl.BlockSpec((1, LANES), lambda b, t: (0, 0)),
                  pl.BlockSpec((1, LANES), lambda b, t: (0, 0)),
                  pl.BlockSpec((1, DV), lambda b, t: (0, 0)),
                  pl.BlockSpec((None, H, DK, DV), lambda b, t: (b, 0, 0, 0))],
        out_specs=[pl.BlockSpec((None, T, W), lambda b, t: (b, t, 0)),
                   pl.BlockSpec((None, H, DK, DV), lambda b, t: (b, 0, 0, 0)),
                   pl.BlockSpec((None, CONV_W - 1, C3), lambda b, t: (b, 0, 0))],
        out_shape=[jax.ShapeDtypeStruct((B, S, W), out_dtype),
                   jax.ShapeDtypeStruct((B, H, DK, DV), F32),
                   jax.ShapeDtypeStruct((B, CONV_W - 1, C3), F32)],
        scratch_shapes=[pltpu.VMEM((T + 8, C3), F32),
                        pltpu.VMEM((H, DK, DV), F32)],
        compiler_params=_cparams(("parallel", "arbitrary")),
        name="gdn",
    )(proj, proj, small, conv_w, conv_buf, a_row, dt_row, gdn_nw, s0)


def _logf_from_small(sm, fb):
    lane = _iota2(sm.shape, 1)
    return jnp.where(lane < H_FOX, -_softplus(-(sm + fb)), 0.0)


def _foxprep_kernel(qkv_ref, sm_ref, fb_ref, qa_ref, ka_ref, vt_ref, ko_ref, vo_ref, lf_ref,
                    carry_ref):
    T = sm_ref.shape[0]
    H = qa_ref.shape[0]
    HD = ko_ref.shape[1]
    W = H * HD
    t = pl.program_id(1)

    @pl.when(t == 0)
    def _():
        carry_ref[...] = jnp.zeros_like(carry_ref)

    logf = _logf_from_small(sm_ref[...], fb_ref[...])
    lf_ref[...] = logf
    ltri = jnp.where(_iota2((T, T), 0) >= _iota2((T, T), 1), 1.0, 0.0).astype(BF16)
    fc = _dot_sel_l(ltri, logf) + carry_ref[0:1, :]
    carry_ref[0:1, :] = fc[T - 1:T, :]
    f_hi, f_mid, f_lo = [p.astype(F32) for p in _split3(fc * LOG2E)]

    lane = _iota2((T, LANES), 1)
    low = lane < HD
    one_q = jnp.where((lane >= HD + 3) & (lane < HD + 6), 1.0, 0.0)
    one_k = jnp.where((lane >= HD) & (lane < HD + 3), 1.0, 0.0)
    ones_rows = jnp.where(_iota2((LANES - HD, T), 0) == 0, 1.0, 0.0)
    scale = (HD ** -0.5) * LOG2E
    for p in range(H // 2):
        qc = qkv_ref[:, p * LANES:(p + 1) * LANES].astype(F32) * scale
        kc = qkv_ref[:, W + p * LANES:W + (p + 1) * LANES].astype(F32)
        kct = kc.T
        vct = qkv_ref[:, 2 * W + p * LANES:2 * W + (p + 1) * LANES].astype(F32).T
        for half in range(2):
            h = 2 * p + half
            if half == 1:
                qh, kh = [pltpu.roll(a, HD, 1) for a in (qc, kc)]
            else:
                qh, kh = qc, kc
            vht = vct[half * HD:(half + 1) * HD, :]
            c_hi, c_mid, c_lo = [a[:, h:h + 1] for a in (f_hi, f_mid, f_lo)]
            ex_q = jnp.where(lane == HD, c_hi,
                             jnp.where(lane == HD + 1, c_mid,
                                       jnp.where(lane == HD + 2, c_lo, one_q)))
            ex_k = jnp.where(lane == HD + 3, -c_hi,
                             jnp.where(lane == HD + 4, -c_mid,
                                       jnp.where(lane == HD + 5, -c_lo, one_k)))
            qa_ref[h] = jnp.where(low, qh, ex_q).astype(BF16)
            ka_ref[h] = jnp.where(low, kh, ex_k).astype(BF16)
            vt_ref[h] = jnp.concatenate([vht, ones_rows], axis=0).astype(BF16)
            ko_ref[h] = kct[half * HD:(half + 1) * HD, :]
            vo_ref[h] = vht


def _foxprep(proj, small, col, fb_row, l, k_prev, v_prev, L, T):
    B, S, _ = proj.shape
    H = H_FOX
    HD = 512 // H
    aug = jax.ShapeDtypeStruct((B, H, S, LANES), BF16)
    augt = jax.ShapeDtypeStruct((B, H, LANES, S), BF16)
    kv = jax.ShapeDtypeStruct((L, B, H, HD, S), F32)
    aug_spec = pl.BlockSpec((None, H, T, LANES), lambda b, t: (b, 0, t, 0))
    augt_spec = pl.BlockSpec((None, H, LANES, T), lambda b, t: (b, 0, 0, t))
    kv_spec = pl.BlockSpec((None, None, H, HD, T), lambda b, t: (l, b, 0, 0, t))
    in_specs = [pl.BlockSpec((None, T, 3 * H * HD), lambda b, t: (b, t, col)),
                pl.BlockSpec((None, T, LANES), lambda b, t: (b, t, 0)),
                pl.BlockSpec((1, LANES), lambda b, t: (0, 0)),
                pl.BlockSpec(memory_space=pl.ANY), pl.BlockSpec(memory_space=pl.ANY)]
    args = [proj, small, fb_row, k_prev, v_prev]
    aliases = {3: 3, 4: 4}
    kernel = lambda a, b, c, _k, _v, *rest: _foxprep_kernel(a, b, c, *rest)
    return pl.pallas_call(
        kernel,
        grid=(B, S // T),
        in_specs=in_specs,
        out_specs=[aug_spec, aug_spec, augt_spec, kv_spec, kv_spec,
                   pl.BlockSpec((None, T, LANES), lambda b, t: (b, t, 0))],
        out_shape=[aug, aug, augt, kv, kv, jax.ShapeDtypeStruct((B, S, LANES), F32)],
        scratch_shapes=[pltpu.VMEM((8, LANES), F32)],
        input_output_aliases=aliases,
        compiler_params=_cparams(("parallel", "arbitrary")),
        name="foxprep",
    )(*args)


FLASH_QS = 256
FLASH_AHEAD1 = 3


def _flash_kernel(qi_ref, kj_ref, qa_ref, ka_ref, vt_ref, z_ref, o_ref,
                  s_scr, p_scr, m_scr, acc_scr):
    TQ = qa_ref.shape[1]
    TK = ka_ref.shape[1]
    QS = FLASH_QS
    HD = LANES // 2
    nqs = TQ // QS
    st = pl.program_id(2)
    i = qi_ref[st]
    j = kj_ref[st]
    nt = (((1,), (1,)), ((), ()))

    @pl.when(j == 0)
    def _():
        m_scr[...] = jnp.full(m_scr.shape, -jnp.inf, F32)
        acc_scr[...] = jnp.zeros_like(acc_scr)

    def unit(n):
        return n // 2, n % 2

    def scores(n, nk, diag):
        qs, hh = unit(n)
        slot = n % FLASH_AHEAD1
        q = qa_ref[hh, pl.ds(pl.multiple_of(qs * QS, QS), QS), :]
        s = lax.dot_general(ka_ref[hh, 0:nk, :], q, nt, preferred_element_type=F32)
        if diag:
            keep = _iota2((QS, QS), 0) <= _iota2((QS, QS), 1)
            last = jnp.where(keep, s[nk - QS:nk, :], -jnp.inf)
            s_scr[slot, nk - QS:nk, :] = last
            m_cur = jnp.max(last, axis=0, keepdims=True)
            if nk > QS:
                s_scr[slot, 0:nk - QS, :] = s[0:nk - QS, :]
                m_cur = jnp.maximum(m_cur, jnp.max(s[0:nk - QS, :], axis=0, keepdims=True))
            return m_cur
        s_scr[slot, 0:nk, :] = s
        return jnp.max(s, axis=0, keepdims=True)

    def softmax(n, nk, m_cur):
        qs, hh = unit(n)
        m_prev = m_scr[hh, qs, 0:1, :]
        m_new = jnp.maximum(m_prev, m_cur)
        p_scr[n % 2, 0:nk, :] = jnp.exp2(s_scr[n % FLASH_AHEAD1, 0:nk, :] - m_new).astype(BF16)
        m_scr[hh, qs, 0:1, :] = m_new
        return jnp.exp2(m_prev - m_new)

    def pv(n, nk, alpha):
        qs, hh = unit(n)
        o = jnp.dot(vt_ref[hh, :, 0:nk], p_scr[n % 2, 0:nk, :], preferred_element_type=F32)
        acc_scr[hh, qs] = acc_scr[hh, qs] * alpha + o

    ahead = FLASH_AHEAD1 - 1
    nu = 2 * nqs

    @pl.when(j < i)
    def _():
        head = tuple(scores(n, TK, False) for n in range(ahead))

        def body(n, m_curs):
            alpha = softmax(n, TK, m_curs[0])
            m_next = scores(n + ahead, TK, False)
            pv(n, TK, alpha)
            return m_curs[1:] + (m_next,)

        tail = lax.fori_loop(0, nu - ahead, body, head, unroll=True)
        for k in range(ahead):
            n = nu - ahead + k
            pv(n, TK, softmax(n, TK, tail[k]))

    @pl.when(j == i)
    def _():
        nks = [(n // 2 + 1) * QS for n in range(nu)]
        m_curs = [scores(n, nks[n], True) for n in range(ahead)]
        for n in range(nu):
            alpha = softmax(n, nks[n], m_curs[n])
            if n + ahead < nu:
                m_curs.append(scores(n + ahead, nks[n + ahead], True))
            pv(n, nks[n], alpha)
        lane = _iota2((QS, LANES), 1)
        for qs in range(nqs):
            a0 = acc_scr[0, qs]
            a1 = acc_scr[1, qs]
            o0 = (a0 / a0[HD:HD + 1, :]).T
            o1 = (a1 / a1[HD:HD + 1, :]).T
            o = jnp.where(lane < HD, o0, pltpu.roll(o1, HD, 1))
            z = z_ref[qs * QS:(qs + 1) * QS, :].astype(F32)
            o_ref[qs * QS:(qs + 1) * QS, :] = (o * _silu(z)).astype(o_ref.dtype)


def _flash(qa, ka, vt, proj, zcol, tq):
    B, H, S, _ = qa.shape
    n = S // tq
    steps = [(i, j) for i in range(n) for j in range(i + 1)]
    qi = jnp.asarray([s[0] for s in steps], jnp.int32)
    kj = jnp.asarray([s[1] for s in steps], jnp.int32)
    nqs = tq // FLASH_QS
    grid_spec = pltpu.PrefetchScalarGridSpec(
        num_scalar_prefetch=2,
        grid=(B, H // 2, len(steps)),
        in_specs=[pl.BlockSpec((None, 2, tq, LANES), lambda b, p, s, qi, kj: (b, p, qi[s], 0)),
                  pl.BlockSpec((None, 2, tq, LANES), lambda b, p, s, qi, kj: (b, p, kj[s], 0)),
                  pl.BlockSpec((None, 2, LANES, tq), lambda b, p, s, qi, kj: (b, p, 0, kj[s])),
                  pl.BlockSpec((None, tq, LANES), lambda b, p, s, qi, kj: (b, qi[s], zcol + p))],
        out_specs=pl.BlockSpec((None, tq, LANES), lambda b, p, s, qi, kj: (b, qi[s], p)),
        scratch_shapes=[pltpu.VMEM((FLASH_AHEAD1, tq, FLASH_QS), F32),
                        pltpu.VMEM((2, tq, FLASH_QS), BF16),
                        pltpu.VMEM((2, nqs, 8, FLASH_QS), F32),
                        pltpu.VMEM((2, nqs, LANES, FLASH_QS), F32)])
    return pl.pallas_call(
        _flash_kernel,
        grid_spec=grid_spec,
        out_shape=jax.ShapeDtypeStruct((B, S, H * LANES // 2), BF16),
        compiler_params=_cparams(("parallel", "parallel", "arbitrary")),
        name="flash",
    )(qi, kj, qa, ka, vt, proj)


def _cumsum_kernel(x_ref, o_ref, *, group):
    R = x_ref.shape[0]
    x = x_ref[...]
    up = jnp.where(_iota2((LANES, LANES), 0) <= _iota2((LANES, LANES), 1), 1.0, 0.0).astype(BF16)
    w = _dot_sel_r(x, up)
    tot = jnp.broadcast_to(w[:, LANES - 1:LANES], (R, LANES))
    ri = _iota2((R, R), 0)
    ci = _iota2((R, R), 1)
    prev = jnp.where((ci < ri) & ((ci // group) == (ri // group)), 1.0, 0.0).astype(BF16)
    o_ref[...] = w + _dot_sel_l(prev, tot)


def _cumsum_rows(x, group):
    R = x.shape[0]
    rb = min(R, 256)
    return pl.pallas_call(
        functools.partial(_cumsum_kernel, group=group),
        grid=(R // rb,),
        in_specs=[pl.BlockSpec((rb, LANES), lambda r: (r, 0))],
        out_specs=pl.BlockSpec((rb, LANES), lambda r: (r, 0)),
        out_shape=jax.ShapeDtypeStruct((R, LANES), F32),
        compiler_params=_cparams(("parallel",)),
        name="cumsum",
    )(x)


def _foxsamp_kernel(q_ref, k_ref, v_ref, z_ref, sm_ref, fb_ref, tot_ref, fcc_ref, ck_ref, cv_ref,
                    o_ref, ko_ref, vo_ref, lf_ref):
    S = q_ref.shape[0]
    HD = ck_ref.shape[1]
    p = pl.program_id(1)
    logf = _logf_from_small(sm_ref[...], fb_ref[...])
    lf_ref[...] = logf
    ltri = jnp.where(_iota2((S, S), 0) >= _iota2((S, S), 1), 1.0, 0.0).astype(BF16)
    fc = _dot_sel_l(ltri, logf) + tot_ref[...]
    fct = jnp.concatenate([fc, jnp.zeros((LANES - S, LANES), F32)], axis=0).T
    lane = _iota2((S, LANES), 1)
    row = _iota2((LANES, LANES), 0)
    causal = _iota2((S, S), 0) >= _iota2((S, S), 1)
    scale = HD ** -0.5
    for hh in range(2):
        h = 2 * p + hh
        fq = jnp.sum(jnp.where(lane == h, fc, 0.0), axis=1, keepdims=True)
        fkn = jnp.sum(jnp.where(row == h, fct, 0.0), axis=0, keepdims=True)[:, :S]
        q = q_ref[:, hh * HD:(hh + 1) * HD].astype(F32) * scale
        kn = k_ref[:, hh * HD:(hh + 1) * HD].astype(F32)
        vn = v_ref[:, hh * HD:(hh + 1) * HD].astype(F32)
        ko_ref[hh] = kn
        vo_ref[hh] = vn
        sc = _dot(q, ck_ref[hh]) + fq - fcc_ref[hh]
        sn = jnp.where(causal, _dot_nt(q, kn) + fq - fkn, -jnp.inf)
        m = jnp.maximum(jnp.max(sc, axis=1, keepdims=True), jnp.max(sn, axis=1, keepdims=True))
        pc = jnp.exp(sc - m)
        pn = jnp.exp(sn - m)
        l = jnp.sum(pc, axis=1, keepdims=True) + jnp.sum(pn, axis=1, keepdims=True)
        o = (_dot_nt(pc, cv_ref[hh]) + _dot(pn, vn)) / l
        o_ref[:, hh * HD:(hh + 1) * HD] = o * _silu(z_ref[:, hh * HD:(hh + 1) * HD].astype(F32))


def _foxsamp(proj, small, cols, fb_row, tot, fcc, cache_k, cache_v, l):
    B, S, _ = proj.shape
    _, _, H, HD, P = cache_k.shape
    cq, ck, cv, cz = cols
    pair = lambda c: pl.BlockSpec((None, S, LANES), lambda b, p: (b, 0, c + p))
    kv_out = jax.ShapeDtypeStruct((B, H, S, HD), F32)
    return pl.pallas_call(
        _foxsamp_kernel,
        grid=(B, H // 2),
        in_specs=[pair(cq), pair(ck), pair(cv), pair(cz),
                  pl.BlockSpec((None, S, LANES), lambda b, p: (b, 0, 0)),
                  pl.BlockSpec((1, LANES), lambda b, p: (0, 0)),
                  pl.BlockSpec((None, 1, LANES), lambda b, p: (b, 0, 0)),
                  pl.BlockSpec((None, 2, 1, P), lambda b, p: (b, p, 0, 0)),
                  pl.BlockSpec((None, None, 2, HD, P), lambda b, p: (l, b, p, 0, 0)),
                  pl.BlockSpec((None, None, 2, HD, P), lambda b, p: (l, b, p, 0, 0))],
        out_specs=[pl.BlockSpec((None, S, LANES), lambda b, p: (b, 0, p)),
                   pl.BlockSpec((None, 2, S, HD), lambda b, p: (b, p, 0, 0)),
                   pl.BlockSpec((None, 2, S, HD), lambda b, p: (b, p, 0, 0)),
                   pl.BlockSpec((None, S, LANES), lambda b, p: (b, 0, 0))],
        out_shape=[jax.ShapeDtypeStruct((B, S, H * HD), F32), kv_out, kv_out,
                   jax.ShapeDtypeStruct((B, S, LANES), F32)],
        compiler_params=_cparams(("parallel", "arbitrary")),
        name="foxsamp",
    )(proj, proj, proj, proj, small, fb_row, tot, fcc, cache_k, cache_v)


COL_MG = 0
COL_GQKV = 6
COL_FQKV = 9
COL_RQK = 12
COL_RV = 13
COL_RZ = 14
COL_FZ = 15
COL_GZ = 16


def _prep_w_in(w_in, D):
    BW = D // 2
    hk = BW // 2
    splits = (hk, hk, BW, BW, BW, BW, BW, H_FOX, BW, BW, BW, BW, H_GDN, H_GDN, BW, N_BRANCH * D)
    offs = np.cumsum((0,) + splits)
    seg = lambda i: w_in[..., offs[i]:offs[i + 1]]
    rq, rk, rv, rz, fq, fk, fv, ff, fz, gq, gk, gv, ga, gb, gz, mg = [seg(i) for i in range(16)]
    main = jnp.concatenate([mg, gq, gk, gv, fq, fk, fv, rq, rk, rv, rz, fz, gz], axis=-1)
    pad = jnp.zeros(w_in.shape[:-1] + (LANES - H_FOX - 2 * H_GDN,), w_in.dtype)
    small = jnp.concatenate([ff, ga, gb, pad], axis=-1)
    return main.astype(BF16), small.astype(BF16)


def _lane_row(vals, off):
    L, n = vals.shape
    return jnp.zeros((L, 1, LANES), F32).at[:, 0, off:off + n].set(vals)


def kernel(x_prompt, x_sample, c_prompt, c_sample, cache_fox_k, cache_fox_v, cache_fox_logf,
           state_ret, state_gdn, state_gdn_conv, norm_w, ada_w, ada_b, w_in, fox_f_bias,
           gdn_a_log, gdn_dt_bias, gdn_conv_w, ret_norm_w, gdn_norm_w, w_branch, w_out,
           final_norm_w):
    B, S, D = x_prompt.shape
    BS, SS, _ = x_sample.shape
    L = ada_w.shape[0]
    P = cache_fox_k.shape[3]
    DK_RET, DV_RET = state_ret.shape[-2:]
    DK_GDN, DV_GDN = state_gdn.shape[-2:]

    rows = -(-(B + BS) // 8) * 8
    c_all = jnp.concatenate([c_prompt, c_sample, jnp.zeros((rows - B - BS, D), F32)], axis=0)
    mod = _adaln(c_all, ada_w, ada_b)
    mod_p = mod[:, :, :B].reshape(L, 3, B, 1, D)
    mod_s = mod[:, :, B:B + BS].reshape(L, 3, BS, 1, D)

    w_main, w_small = _prep_w_in(w_in, D)
    w_branch_b = w_branch.astype(BF16)
    w_out_b = w_out.astype(BF16)
    norm_w3 = norm_w.reshape(L, 1, D)
    final_w = final_norm_w.reshape(1, D)
    ret_nw = ret_norm_w.reshape(L, H_RET, 1, DV_RET)
    gdn_nw = gdn_norm_w.reshape(L, 1, DV_GDN)
    fb_rows = _lane_row(fox_f_bias, 0)
    a_rows = _lane_row(gdn_a_log, H_FOX)
    dt_rows = _lane_row(gdn_dt_bias, H_FOX)

    tm_p = min(S, 1024)
    tmm_p = min(S, 512)
    t_ret = min(S, 256)
    t_gdn = min(S, 256)
    t_prep = min(S, 256)
    t_flash = min(S, 1024)

    tabs_p = _ret_tables(t_ret, 0, S, DK_RET)
    tabs_s = _ret_tables(SS, P, SS, DK_RET)

    fcc = _cumsum_rows(cache_fox_logf.reshape(-1, LANES), P // LANES).reshape(L, BS, H_FOX, 1, P)
    tot = jnp.zeros((L, BS, 1, LANES), F32).at[:, :, 0, :H_FOX].set(fcc[:, :, :, 0, P - 1])

    zeros_ret = jnp.zeros((B, H_RET, DK_RET, DV_RET), F32)
    zeros_gdn = jnp.zeros((B, H_GDN, DK_GDN, DV_GDN), F32)
    zeros_conv = jnp.zeros((B, CONV_W - 1, gdn_conv_w.shape[-1]), F32)

    xp, xs = x_prompt, x_sample
    cache_kt = jnp.swapaxes(cache_fox_k, 3, 4)
    cache_vt = jnp.swapaxes(cache_fox_v, 3, 4)
    pk = jnp.zeros((L, B, H_FOX, cache_fox_k.shape[-1], S), F32)
    pv = jnp.zeros_like(pk)
    p_lf, p_ret, p_gdn, p_conv = [], [], [], []
    s_k, s_v, s_lf, s_ret, s_gdn, s_conv = [], [], [], [], [], []
    for l in range(L):
        final = l == L - 1
        proj, small = _inproj(xp, mod_p, l, norm_w3, w_main, w_small, 1, tm_p)
        o_ret, st = _ret(proj, (COL_RQK, COL_RV, COL_RZ), tabs_p, ret_nw[l], zeros_ret, t_ret, BF16)
        p_ret.append(st)
        o_gdn, st, cv = _gdn(proj, small, (COL_GQKV // 3, COL_GZ), gdn_conv_w[l], zeros_conv,
                             a_rows[l], dt_rows[l], gdn_nw[l], zeros_gdn, t_gdn, BF16)
        p_gdn.append(st)
        p_conv.append(cv)
        qa, ka, va, pk, pv, lf = _foxprep(proj, small, COL_FQKV // 3, fb_rows[l], l, pk, pv, L, t_prep)
        p_lf.append(lf)
        o_fox = _flash(qa, ka, va, proj, COL_FZ * 4, t_flash)
        xp = _merge(xp, mod_p, l, o_ret, o_fox, o_gdn, proj, w_branch_b, w_out_b, final_w,
                    1, tmm_p, final)
        proj, small = _inproj(xs, mod_s, l, norm_w3, w_main, w_small, BS, SS)
        o_ret, st = _ret(proj, (COL_RQK, COL_RV, COL_RZ), tabs_s, ret_nw[l], state_ret[l], SS, F32)
        s_ret.append(st)
        o_gdn, st, cv = _gdn(proj, small, (COL_GQKV // 3, COL_GZ), gdn_conv_w[l], state_gdn_conv[l],
                             a_rows[l], dt_rows[l], gdn_nw[l], state_gdn[l], SS, F32)
        s_gdn.append(st)
        s_conv.append(cv)
        o_fox, kk, vv, lf = _foxsamp(proj, small,
                                     (COL_FQKV * 4, (COL_FQKV + 1) * 4, (COL_FQKV + 2) * 4, COL_FZ * 4),
                                     fb_rows[l], tot[l], fcc[l], cache_kt, cache_vt, l)
        s_k.append(kk)
        s_v.append(vv)
        s_lf.append(lf)
        xs = _merge(xs, mod_s, l, o_ret, o_fox, o_gdn, proj, w_branch_b, w_out_b, final_w,
                    BS, SS, final)

    logf_out = lambda lfs: jnp.stack(lfs)[..., :H_FOX].transpose(0, 1, 3, 2)
    return (xp, xs,
            jnp.swapaxes(pk, 3, 4), jnp.swapaxes(pv, 3, 4), logf_out(p_lf), jnp.stack(p_ret), jnp.stack(p_gdn), jnp.stack(p_conv),
            jnp.stack(s_k), jnp.stack(s_v), logf_out(s_lf), jnp.stack(s_ret), jnp.stack(s_gdn),
            jnp.stack(s_conv))
```

```python
import functools

import numpy as np
import jax
import jax.numpy as jnp
from jax import lax
from jax.experimental import pallas as pl
from jax.experimental.pallas import tpu as pltpu

F32 = jnp.float32
BF16 = jnp.bfloat16

N_BRANCH = 3
H_RET = 4
H_FOX = 8
H_GDN = 4
CONV_W = 4
CHUNK = 64
NORM_EPS = 1e-6
ROPE_BASE = 10000.0
LOG2E = 1.4426950408889634
LANES = 128
GDN_CHUNK = 128
VMEM_LIMIT = 56 * 1024 * 1024


def _cparams(sem):
    return pltpu.CompilerParams(dimension_semantics=sem, vmem_limit_bytes=VMEM_LIMIT)


def _sigmoid(x):
    return 0.5 * jnp.tanh(0.5 * x) + 0.5


def _silu(x):
    return x * _sigmoid(x)


def _softplus(x):
    return jnp.maximum(x, 0.0) + jnp.log(1.0 + jnp.exp(-jnp.abs(x)))


def _dot(a, b):
    return jnp.dot(a.astype(BF16), b.astype(BF16), preferred_element_type=F32)


def _dot_nt(a, b):
    return lax.dot_general(a.astype(BF16), b.astype(BF16), (((1,), (1,)), ((), ())),
                           preferred_element_type=F32)


def _split3(x):
    hi = x.astype(BF16)
    r = x - hi.astype(F32)
    mid = r.astype(BF16)
    lo = (r - mid.astype(F32)).astype(BF16)
    return hi, mid, lo


def _dot_sel_l(m, x):
    hi, mid, lo = _split3(x)
    d = lambda y: jnp.dot(m, y, preferred_element_type=F32)
    return d(hi) + (d(mid) + d(lo))


def _dot_sel_r(x, m):
    hi, mid, lo = _split3(x)
    d = lambda y: jnp.dot(y, m, preferred_element_type=F32)
    return d(hi) + (d(mid) + d(lo))


def _iota2(shape, dim):
    return lax.broadcasted_iota(jnp.int32, shape, dim)


def _adaln_kernel(c_ref, w_ref, b_ref, o_ref):
    s = _silu(c_ref[...])
    o_ref[...] = _dot(s, w_ref[...]) + b_ref[...]


def _adaln(c_all, ada_w, ada_b):
    L, D, _ = ada_w.shape
    R = c_all.shape[0]
    return pl.pallas_call(
        _adaln_kernel,
        grid=(L, 3),
        in_specs=[pl.BlockSpec((R, D), lambda l, k: (0, 0)),
                  pl.BlockSpec((None, D, D), lambda l, k: (l, 0, k)),
                  pl.BlockSpec((None, None, 1, D), lambda l, k: (l, k, 0, 0))],
        out_specs=pl.BlockSpec((None, None, R, D), lambda l, k: (l, k, 0, 0)),
        out_shape=jax.ShapeDtypeStruct((L, 3, R, D), F32),
        compiler_params=_cparams(("parallel", "parallel")),
        name="adaln",
    )(c_all, ada_w, ada_b.reshape(L, 3, 1, D))


PROJ_TN = 512
PROJ_DTYPE = BF16


def _inproj_kernel(x_ref, nw_ref, sc_ref, sh_ref, w_ref, ws_ref, o_ref, os_ref, h_ref):
    bb, tm, D = x_ref.shape

    @pl.when(pl.program_id(2) == 0)
    def _():
        x = x_ref[...]
        ms = jnp.mean(x * x, axis=-1, keepdims=True)
        xn = (x * lax.rsqrt(ms + NORM_EPS)) * nw_ref[...]
        h = xn * (1.0 + sc_ref[...]) + sh_ref[...]
        hb = h.astype(BF16).reshape(bb * tm, D)
        h_ref[...] = hb
        os_ref[...] = jnp.dot(hb, ws_ref[...], preferred_element_type=F32).reshape(bb, tm, LANES)

    o_ref[...] = jnp.dot(h_ref[...], w_ref[...],
                         preferred_element_type=F32).reshape(bb, tm, PROJ_TN).astype(o_ref.dtype)


def _inproj(x, mod, l, norm_w, w_main, w_small, bb, tm):
    B, S, D = x.shape
    NP = w_main.shape[-1]
    nj = NP // PROJ_TN
    return pl.pallas_call(
        _inproj_kernel,
        grid=(B // bb, S // tm, nj),
        in_specs=[pl.BlockSpec((bb, tm, D), lambda b, i, j: (b, i, 0)),
                  pl.BlockSpec((None, 1, D), lambda b, i, j: (l, 0, 0)),
                  pl.BlockSpec((None, None, bb, 1, D), lambda b, i, j: (l, 1, b, 0, 0)),
                  pl.BlockSpec((None, None, bb, 1, D), lambda b, i, j: (l, 0, b, 0, 0)),
                  pl.BlockSpec((None, D, PROJ_TN), lambda b, i, j: (l, 0, j)),
                  pl.BlockSpec((None, D, LANES), lambda b, i, j: (l, 0, 0))],
        out_specs=[pl.BlockSpec((bb, tm, PROJ_TN), lambda b, i, j: (b, i, j)),
                   pl.BlockSpec((bb, tm, LANES), lambda b, i, j: (b, i, 0))],
        out_shape=[jax.ShapeDtypeStruct((B, S, NP), PROJ_DTYPE),
                   jax.ShapeDtypeStruct((B, S, LANES), F32)],
        scratch_shapes=[pltpu.VMEM((bb * tm, D), BF16)],
        compiler_params=_cparams(("parallel", "parallel", "arbitrary")),
        name="inproj",
    )(x, norm_w, mod, mod, w_main, w_small)


def _merge_kernel(x_ref, g_ref, o1_ref, o2_ref, o3_ref, mg_ref, wb_ref, wo_ref, fw_ref, y_ref,
                  *, final):
    bb, tm, D = x_ref.shape
    M = bb * tm
    acc = None
    for n, o_ref in enumerate((o1_ref, o2_ref, o3_ref)):
        br = jnp.dot(o_ref[...].reshape(M, o_ref.shape[-1]).astype(BF16), wb_ref[n],
                     preferred_element_type=F32)
        gate = _sigmoid(mg_ref[:, :, n * D:(n + 1) * D].astype(F32).reshape(M, D))
        term = br * gate
        acc = term if acc is None else acc + term
    out = jnp.dot(acc.astype(BF16), wo_ref[...], preferred_element_type=F32)
    xn = x_ref[...] + g_ref[...] * out.reshape(bb, tm, D)
    if final:
        ms = jnp.mean(xn * xn, axis=-1, keepdims=True)
        xn = (xn * lax.rsqrt(ms + NORM_EPS)) * fw_ref[...]
    y_ref[...] = xn


def _merge(x, mod, l, o_ret, o_fox, o_gdn, proj, w_branch, w_out, final_w, bb, tm, final):
    B, S, D = x.shape
    W = o_ret.shape[-1]
    return pl.pallas_call(
        functools.partial(_merge_kernel, final=final),
        grid=(B // bb, S // tm),
        in_specs=[pl.BlockSpec((bb, tm, D), lambda b, i: (b, i, 0)),
                  pl.BlockSpec((None, None, bb, 1, D), lambda b, i: (l, 2, b, 0, 0)),
                  pl.BlockSpec((bb, tm, W), lambda b, i: (b, i, 0)),
                  pl.BlockSpec((bb, tm, W), lambda b, i: (b, i, 0)),
                  pl.BlockSpec((bb, tm, W), lambda b, i: (b, i, 0)),
                  pl.BlockSpec((bb, tm, N_BRANCH * D), lambda b, i: (b, i, 0)),
                  pl.BlockSpec((None, N_BRANCH, W, D), lambda b, i: (l, 0, 0, 0)),
                  pl.BlockSpec((None, D, D), lambda b, i: (l, 0, 0)),
                  pl.BlockSpec((1, D), lambda b, i: (0, 0))],
        out_specs=pl.BlockSpec((bb, tm, D), lambda b, i: (b, i, 0)),
        out_shape=jax.ShapeDtypeStruct((B, S, D), F32),
        compiler_params=_cparams(("parallel", "parallel")),
        name="merge",
    )(x, mod, o_ret, o_fox, o_gdn, proj, w_branch, w_out, final_w)


def _ret_kernel(qk_ref, v_ref, z_ref, cos_ref, sin_ref, m_ref, dq_ref, dk_ref, dc_ref, nw_ref,
                s0_ref, o_ref, so_ref, s_ref):
    T = qk_ref.shape[0]
    H = s_ref.shape[0]
    DK = s0_ref.shape[1]
    t = pl.program_id(1)

    @pl.when(t == 0)
    def _():
        zero = jnp.zeros((DK, LANES), F32)
        for h in range(H):
            s0 = s0_ref[h]
            s_ref[h] = jnp.concatenate([s0, zero] if h % 2 == 0 else [zero, s0], axis=0)

    cos = cos_ref[...]
    sin = sin_ref[...]
    lane = _iota2((T, LANES), 1)
    first = (lane % DK) < (DK // 2)
    low = lane < DK

    def rot(x):
        sw = jnp.where(first, pltpu.roll(x, LANES - DK // 2, 1), pltpu.roll(x, DK // 2, 1))
        return x * cos + sw * sin

    nkq = H * DK
    hs = range(H)
    qcs = [rot(qk_ref[:, p * LANES:(p + 1) * LANES].astype(F32)) for p in range(H // 2)]
    kcs = [rot(qk_ref[:, nkq + p * LANES:nkq + (p + 1) * LANES].astype(F32)) * (DK ** -0.5)
           for p in range(H // 2)]
    kcbs = [kc.astype(BF16) for kc in kcs]
    sels = [low if h % 2 == 0 else jnp.logical_not(low) for h in hs]
    qms = [jnp.where(sels[h], qcs[h // 2], 0.0) for h in hs]
    vhs = [v_ref[:, h * LANES:(h + 1) * LANES].astype(BF16) for h in hs]
    ss = [s_ref[h] for h in hs]
    ams = [_dot_nt(qms[h], kcbs[h // 2]) * m_ref[h] for h in hs]
    kts = [(jnp.where(sels[h], kcs[h // 2], 0.0) * dk_ref[h]).T for h in hs]
    os_ = [_dot(ams[h], vhs[h]) + _dot(qms[h] * dq_ref[h], ss[h]) for h in hs]
    for h in hs:
        s_ref[h] = ss[h] * dc_ref[h] + _dot(kts[h], vhs[h])
    for h in hs:
        o = os_[h]
        ms = jnp.mean(o * o, axis=-1, keepdims=True)
        on = (o * lax.rsqrt(ms + NORM_EPS)) * nw_ref[h]
        z = z_ref[:, h * LANES:(h + 1) * LANES].astype(F32)
        o_ref[:, h * LANES:(h + 1) * LANES] = (on * _silu(z)).astype(o_ref.dtype)

    @pl.when(t == pl.num_programs(1) - 1)
    def _():
        for h in range(H):
            so_ref[h] = s_ref[h][(h % 2) * DK:(h % 2 + 1) * DK, :]


def _ret_tables(T, pos0, S, DK):
    half = DK // 2
    inv_freq = ROPE_BASE ** (-jnp.arange(half, dtype=F32) / half)
    pos = pos0 + jnp.arange(S)
    ang = pos.astype(F32)[:, None] * inv_freq[None, :]
    cos, sin = jnp.cos(ang), jnp.sin(ang)
    reps = LANES // DK
    cos_t = jnp.tile(jnp.concatenate([cos, cos], axis=1), (1, reps))
    sin_t = jnp.tile(jnp.concatenate([-sin, sin], axis=1), (1, reps))
    log_g = jnp.log1p(-jnp.exp2(-5.0 - jnp.arange(H_RET, dtype=F32)))
    i = jnp.arange(T)
    dist = jnp.abs(i[:, None] - i[None, :]).astype(F32)
    allowed = (i[None, :] // CHUNK) <= (i[:, None] // CHUNK)
    m = jnp.where(allowed[None], jnp.exp(dist[None] * log_g[:, None, None]), 0.0)
    fi = i.astype(F32)
    dq = jnp.exp((fi[None, :] + 1.0) * log_g[:, None])
    dk = jnp.exp((T - 1.0 - fi)[None, :] * log_g[:, None])
    dc = jnp.exp(T * log_g)
    bc = lambda a: jnp.broadcast_to(a[..., None], a.shape + (LANES,))
    return cos_t, sin_t, m, bc(dq), bc(dk), bc(dc[:, None])


def _ret(proj, cols, tabs, ret_nw, s0, T, out_dtype):
    B, S, _ = proj.shape
    H, DK, DV = s0.shape[1:]
    W = H * DV
    cqk, cv, cz = cols
    cos_t, sin_t, m, dq, dk, dc = tabs
    return pl.pallas_call(
        _ret_kernel,
        grid=(B, S // T),
        in_specs=[pl.BlockSpec((None, T, W), lambda b, t: (b, t, cqk)),
                  pl.BlockSpec((None, T, W), lambda b, t: (b, t, cv)),
                  pl.BlockSpec((None, T, W), lambda b, t: (b, t, cz)),
                  pl.BlockSpec((T, LANES), lambda b, t: (t, 0)),
                  pl.BlockSpec((T, LANES), lambda b, t: (t, 0)),
                  pl.BlockSpec((H, T, T), lambda b, t: (0, 0, 0)),
                  pl.BlockSpec((H, T, LANES), lambda b, t: (0, 0, 0)),
                  pl.BlockSpec((H, T, LANES), lambda b, t: (0, 0, 0)),
                  pl.BlockSpec((H, 1, LANES), lambda b, t: (0, 0, 0)),
                  pl.BlockSpec((H, 1, DV), lambda b, t: (0, 0, 0)),
                  pl.BlockSpec((None, H, DK, DV), lambda b, t: (b, 0, 0, 0))],
        out_specs=[pl.BlockSpec((None, T, W), lambda b, t: (b, t, 0)),
                   pl.BlockSpec((None, H, DK, DV), lambda b, t: (b, 0, 0, 0))],
        out_shape=[jax.ShapeDtypeStruct((B, S, W), out_dtype),
                   jax.ShapeDtypeStruct((B, H, DK, DV), F32)],
        scratch_shapes=[pltpu.VMEM((H, LANES, DV), F32)],
        compiler_params=_cparams(("parallel", "arbitrary")),
        name="ret",
    )(proj, proj, proj, cos_t, sin_t, m, dq, dk, dc, ret_nw, s0)


INV_BASE = 8


def _unit_lower_inverses(mats, eye, ri, ci):
    n = mats[0].shape[0]
    b = INV_BASE
    d = lambda x, y: jnp.dot(x, y, preferred_element_type=F32)
    same = (ri // b) == (ci // b)
    ads = [jnp.where(same, a, 0.0) for a in mats]
    xs = [eye - ad for ad in ads]
    ps = [ad.astype(BF16) for ad in ads]
    k = 2
    while k < b:
        ps = [d(p, p).astype(BF16) for p in ps]
        xs = [x + d(x.astype(BF16), p) for x, p in zip(xs, ps)]
        k *= 2
    while b < n:
        off = ((ri // (2 * b)) == (ci // (2 * b))) & ((ri // b) != (ci // b))
        xbs = [x.astype(BF16) for x in xs]
        ts = [d(jnp.where(off, a, 0.0).astype(BF16), xb) for a, xb in zip(mats, xbs)]
        xs = [x - d(xb, t.astype(BF16)) for x, xb, t in zip(xs, xbs, ts)]
        b *= 2
    return xs


def _gdn_kernel(u_ref, z_ref, sm_ref, cw_ref, cb_ref, al_ref, dt_ref, nw_ref, s0_ref,
                o_ref, so_ref, co_ref, ext_ref, s_ref):
    T, C3 = u_ref.shape
    H = s_ref.shape[0]
    DK = s_ref.shape[1]
    W = H * DK
    C = GDN_CHUNK
    TP = -(-T // C) * C
    t = pl.program_id(1)
    last = pl.num_programs(1) - 1
    base = 8
    lo = base - (CONV_W - 1)

    @pl.when(t == 0)
    def _():
        s_ref[...] = s0_ref[...]
        ext_ref[lo:base, :] = cb_ref[...]

    ext_ref[base:base + T, :] = u_ref[...].astype(F32)
    y = ext_ref[lo:lo + T, :] * cw_ref[0:1, :]
    for j in range(1, CONV_W):
        y = y + ext_ref[lo + j:lo + j + T, :] * cw_ref[j:j + 1, :]
    tail = ext_ref[lo + T:base + T, :]
    ext_ref[lo:base, :] = tail

    @pl.when(t == last)
    def _():
        co_ref[...] = tail

    qkv = _silu(y)
    sm = sm_ref[...]
    g_all = -jnp.exp(al_ref[...]) * _softplus(sm + dt_ref[...])
    b_all = _sigmoid(sm)
    z_all = z_ref[...].astype(F32)
    if TP > T:
        pad = lambda a: jnp.concatenate([a, jnp.zeros((TP - T, a.shape[1]), F32)], axis=0)
        qkv, g_all, b_all, z_all = pad(qkv), pad(g_all), pad(b_all), pad(z_all)

    ri = _iota2((C, C), 0)
    ci = _iota2((C, C), 1)
    tri = ri >= ci
    strict = ri > ci
    ltri = jnp.where(tri, 1.0, 0.0).astype(BF16)
    eye = jnp.where(ri == ci, 1.0, 0.0)

    nc = TP // C
    items = [(c, h) for c in range(nc) for h in range(H)]
    gcums = [_dot_sel_l(ltri, g_all[c * C:(c + 1) * C, :]) for c in range(nc)]

    qs, ks, vs, gcs, bcs, decs, kbs = [], [], [], [], [], [], []
    for c, h in items:
        r0 = c * C
        q = qkv[r0:r0 + C, h * DK:(h + 1) * DK]
        k = qkv[r0:r0 + C, W + h * DK:W + (h + 1) * DK]
        qs.append(q * lax.rsqrt(jnp.sum(q * q, axis=-1, keepdims=True) + NORM_EPS) * (DK ** -0.5))
        k = k * lax.rsqrt(jnp.sum(k * k, axis=-1, keepdims=True) + NORM_EPS)
        ks.append(k)
        kbs.append(k.astype(BF16))
        vs.append(qkv[r0:r0 + C, 2 * W + h * DK:2 * W + (h + 1) * DK])
        gc = gcums[c][:, 8 + h:9 + h]
        gcs.append(gc)
        bcs.append(b_all[r0:r0 + C, 12 + h:13 + h])
        gm = jnp.broadcast_to(gc, (C, C))
        decs.append(jnp.exp(jnp.where(tri, gm - gm.T, -jnp.inf)))
    kks = [_dot_nt(kb, kb) for kb in kbs]
    amats = [jnp.where(strict, kk * dec, 0.0) * bc for kk, dec, bc in zip(kks, decs, bcs)]
    tinvs = _unit_lower_inverses(amats, eye, ri, ci)
    egs = [jnp.exp(gc) for gc in gcs]
    rhss = [jnp.concatenate([k * (bc * eg), v * bc], axis=1)
            for k, v, bc, eg in zip(ks, vs, bcs, egs)]
    wus = [_dot(tinv, rhs) for tinv, rhs in zip(tinvs, rhss)]
    qks = [(_dot_nt(q, kb) * dec).astype(BF16) for q, kb, dec in zip(qs, kbs, decs)]
    qes = [(q * eg).astype(BF16) for q, eg in zip(qs, egs)]
    glasts = [gcums[c][C - 1:C, 8 + h:9 + h] for c, h in items]
    kdts = [(k * jnp.exp(gl - gc)).T.astype(BF16) for k, gl, gc in zip(ks, glasts, gcs)]
    sdec = [jnp.exp(gl) for gl in glasts]

    for c in range(nc):
        idx = [c * H + h for h in range(H)]
        ss = [s_ref[h] for h in range(H)]
        sbs = [s.astype(BF16) for s in ss]
        us = [wus[i][:, DK:] - _dot(wus[i][:, :DK], sb) for i, sb in zip(idx, sbs)]
        ubs = [u.astype(BF16) for u in us]
        os_ = [jnp.dot(qes[i], sb, preferred_element_type=F32)
               + jnp.dot(qks[i], ub, preferred_element_type=F32)
               for i, sb, ub in zip(idx, sbs, ubs)]
        for h, i in enumerate(idx):
            s_ref[h] = ss[h] * sdec[i] + jnp.dot(kdts[i], ubs[h], preferred_element_type=F32)
        for h, o in enumerate(os_):
            ms = jnp.mean(o * o, axis=-1, keepdims=True)
            on = (o * lax.rsqrt(ms + NORM_EPS)) * nw_ref[...]
            z = z_all[c * C:(c + 1) * C, h * DK:(h + 1) * DK]
            res = (on * _silu(z)).astype(o_ref.dtype)
            if TP > T:
                o_ref[:, h * DK:(h + 1) * DK] = res[:T, :]
            else:
                o_ref[c * C:(c + 1) * C, h * DK:(h + 1) * DK] = res

    @pl.when(t == last)
    def _():
        so_ref[...] = s_ref[...]


def _gdn(proj, small, cols, conv_w, conv_buf, a_row, dt_row, gdn_nw, s0, T, out_dtype):
    B, S, _ = proj.shape
    H, DK, DV = s0.shape[1:]
    W = H * DK
    C3 = conv_w.shape[-1]
    cu, cz = cols
    return pl.pallas_call(
        _gdn_kernel,
        grid=(B, S // T),
        in_specs=[pl.BlockSpec((None, T, C3), lambda b, t: (b, t, cu)),
                  pl.BlockSpec((None, T, W), lambda b, t: (b, t, cz)),
                  pl.BlockSpec((None, T, LANES), lambda b, t: (b, t, 0)),
                  pl.BlockSpec((CONV_W, C3), lambda b, t: (0, 0)),
                  pl.BlockSpec((None, CONV_W - 1, C3), lambda b, t: (b, 0, 0)),
                  pl.BlockSpec((1, LANES), lambda b, t: (0, 0)),
                  pl.BlockSpec((1, LANES), lambda b, t: (0, 0)),
                  pl.BlockSpec((1, DV), lambda b, t: (0, 0)),
                  pl.BlockSpec((None, H, DK, DV), lambda b, t: (b, 0, 0, 0))],
        out_specs=[pl.BlockSpec((None, T, W), lambda b, t: (b, t, 0)),
                   pl.BlockSpec((None, H, DK, DV), lambda b, t: (b, 0, 0, 0)),
                   pl.BlockSpec((None, CONV_W - 1, C3), lambda b, t: (b, 0, 0))],
        out_shape=[jax.ShapeDtypeStruct((B, S, W), out_dtype),
                   jax.ShapeDtypeStruct((B, H, DK, DV), F32),
                   jax.ShapeDtypeStruct((B, CONV_W - 1, C3), F32)],
        scratch_shapes=[pltpu.VMEM((T + 8, C3), F32),
                        pltpu.VMEM((H, DK, DV), F32)],
        compiler_params=_cparams(("parallel", "arbitrary")),
        name="gdn",
    )(proj, proj, small, conv_w, conv_buf, a_row, dt_row, gdn_nw, s0)


def _logf_from_small(sm, fb):
    lane = _iota2(sm.shape, 1)
    return jnp.where(lane < H_FOX, -_softplus(-(sm + fb)), 0.0)


def _foxprep_kernel(qkv_ref, sm_ref, fb_ref, qa_ref, ka_ref, vt_ref, ko_ref, vo_ref, lf_ref,
                    carry_ref):
    T = sm_ref.shape[0]
    H = qa_ref.shape[0]
    HD = ko_ref.shape[1]
    W = H * HD
    t = pl.program_id(1)

    @pl.when(t == 0)
    def _():
        carry_ref[...] = jnp.zeros_like(carry_ref)

    logf = _logf_from_small(sm_ref[...], fb_ref[...])
    lf_ref[...] = logf
    ltri = jnp.where(_iota2((T, T), 0) >= _iota2((T, T), 1), 1.0, 0.0).astype(BF16)
    fc = _dot_sel_l(ltri, logf) + carry_ref[0:1, :]
    carry_ref[0:1, :] = fc[T - 1:T, :]
    f_hi, f_mid, f_lo = [p.astype(F32) for p in _split3(fc * LOG2E)]

    lane = _iota2((T, LANES), 1)
    low = lane < HD
    one_q = jnp.where((lane >= HD + 3) & (lane < HD + 6), 1.0, 0.0)
    one_k = jnp.where((lane >= HD) & (lane < HD + 3), 1.0, 0.0)
    ones_rows = jnp.where(_iota2((LANES - HD, T), 0) == 0, 1.0, 0.0)
    scale = (HD ** -0.5) * LOG2E
    for p in range(H // 2):
        qc = qkv_ref[:, p * LANES:(p + 1) * LANES].astype(F32) * scale
        kc = qkv_ref[:, W + p * LANES:W + (p + 1) * LANES].astype(F32)
        kct = kc.T
        vct = qkv_ref[:, 2 * W + p * LANES:2 * W + (p + 1) * LANES].astype(F32).T
        for half in range(2):
            h = 2 * p + half
            if half == 1:
                qh, kh = [pltpu.roll(a, HD, 1) for a in (qc, kc)]
            else:
                qh, kh = qc, kc
            vht = vct[half * HD:(half + 1) * HD, :]
            c_hi, c_mid, c_lo = [a[:, h:h + 1] for a in (f_hi, f_mid, f_lo)]
            ex_q = jnp.where(lane == HD, c_hi,
                             jnp.where(lane == HD + 1, c_mid,
                                       jnp.where(lane == HD + 2, c_lo, one_q)))
            ex_k = jnp.where(lane == HD + 3, -c_hi,
                             jnp.where(lane == HD + 4, -c_mid,
                                       jnp.where(lane == HD + 5, -c_lo, one_k)))
            qa_ref[h] = jnp.where(low, qh, ex_q).astype(BF16)
            ka_ref[h] = jnp.where(low, kh, ex_k).astype(BF16)
            vt_ref[h] = jnp.concatenate([vht, ones_rows], axis=0).astype(BF16)
            ko_ref[h] = kct[half * HD:(half + 1) * HD, :]
            vo_ref[h] = vht


def _foxprep(proj, small, col, fb_row, l, k_prev, v_prev, L, T):
    B, S, _ = proj.shape
    H = H_FOX
    HD = 512 // H
    aug = jax.ShapeDtypeStruct((B, H, S, LANES), BF16)
    augt = jax.ShapeDtypeStruct((B, H, LANES, S), BF16)
    kv = jax.ShapeDtypeStruct((L, B, H, HD, S), F32)
    aug_spec = pl.BlockSpec((None, H, T, LANES), lambda b, t: (b, 0, t, 0))
    augt_spec = pl.BlockSpec((None, H, LANES, T), lambda b, t: (b, 0, 0, t))
    kv_spec = pl.BlockSpec((None, None, H, HD, T), lambda b, t: (l, b, 0, 0, t))
    in_specs = [pl.BlockSpec((None, T, 3 * H * HD), lambda b, t: (b, t, col)),
                pl.BlockSpec((None, T, LANES), lambda b, t: (b, t, 0)),
                pl.BlockSpec((1, LANES), lambda b, t: (0, 0)),
                pl.BlockSpec(memory_space=pl.ANY), pl.BlockSpec(memory_space=pl.ANY)]
    args = [proj, small, fb_row, k_prev, v_prev]
    aliases = {3: 3, 4: 4}
    kernel = lambda a, b, c, _k, _v, *rest: _foxprep_kernel(a, b, c, *rest)
    return pl.pallas_call(
        kernel,
        grid=(B, S // T),
        in_specs=in_specs,
        out_specs=[aug_spec, aug_spec, augt_spec, kv_spec, kv_spec,
                   pl.BlockSpec((None, T, LANES), lambda b, t: (b, t, 0))],
        out_shape=[aug, aug, augt, kv, kv, jax.ShapeDtypeStruct((B, S, LANES), F32)],
        scratch_shapes=[pltpu.VMEM((8, LANES), F32)],
        input_output_aliases=aliases,
        compiler_params=_cparams(("parallel", "arbitrary")),
        name="foxprep",
    )(*args)


FLASH_QS = 256
FLASH_AHEAD1 = 6


def _flash_kernel(qi_ref, kj_ref, qa_ref, ka_ref, vt_ref, z_ref, o_ref,
                  s_scr, p_scr, m_scr, acc_scr):
    TQ = qa_ref.shape[1]
    TK = ka_ref.shape[1]
    QS = FLASH_QS
    HD = LANES // 2
    nqs = TQ // QS
    st = pl.program_id(2)
    i = qi_ref[st]
    j = kj_ref[st]
    nt = (((1,), (1,)), ((), ()))

    @pl.when(j == 0)
    def _():
        m_scr[...] = jnp.full(m_scr.shape, -jnp.inf, F32)
        acc_scr[...] = jnp.zeros_like(acc_scr)

    def unit(n):
        return n // 2, n % 2

    def scores(n, nk, diag):
        qs, hh = unit(n)
        slot = n % FLASH_AHEAD1
        q = qa_ref[hh, pl.ds(pl.multiple_of(qs * QS, QS), QS), :]
        s = lax.dot_general(ka_ref[hh, 0:nk, :], q, nt, preferred_element_type=F32)
        if diag:
            keep = _iota2((QS, QS), 0) <= _iota2((QS, QS), 1)
            last = jnp.where(keep, s[nk - QS:nk, :], -jnp.inf)
            s_scr[slot, nk - QS:nk, :] = last
            m_cur = jnp.max(last, axis=0, keepdims=True)
            if nk > QS:
                s_scr[slot, 0:nk - QS, :] = s[0:nk - QS, :]
                m_cur = jnp.maximum(m_cur, jnp.max(s[0:nk - QS, :], axis=0, keepdims=True))
            return m_cur
        s_scr[slot, 0:nk, :] = s
        return jnp.max(s, axis=0, keepdims=True)

    def softmax(n, nk, m_cur):
        qs, hh = unit(n)
        m_prev = m_scr[hh, qs, 0:1, :]
        m_new = jnp.maximum(m_prev, m_cur)
        p_scr[n % 2, 0:nk, :] = jnp.exp2(s_scr[n % FLASH_AHEAD1, 0:nk, :] - m_new).astype(BF16)
        m_scr[hh, qs, 0:1, :] = m_new
        return jnp.exp2(m_prev - m_new)

    def pv(n, nk, alpha):
        qs, hh = unit(n)
        o = jnp.dot(vt_ref[hh, :, 0:nk], p_scr[n % 2, 0:nk, :], preferred_element_type=F32)
        acc_scr[hh, qs] = acc_scr[hh, qs] * alpha + o

    ahead = FLASH_AHEAD1 - 1
    nu = 2 * nqs

    @pl.when(j < i)
    def _():
        head = tuple(scores(n, TK, False) for n in range(ahead))

        def body(n, m_curs):
            alpha = softmax(n, TK, m_curs[0])
            m_next = scores(n + ahead, TK, False)
            pv(n, TK, alpha)
            return m_curs[1:] + (m_next,)

        tail = lax.fori_loop(0, nu - ahead, body, head, unroll=True)
        for k in range(ahead):
            n = nu - ahead + k
            pv(n, TK, softmax(n, TK, tail[k]))

    @pl.when(j == i)
    def _():
        nks = [(n // 2 + 1) * QS for n in range(nu)]
        m_curs = [scores(n, nks[n], True) for n in range(ahead)]
        for n in range(nu):
            alpha = softmax(n, nks[n], m_curs[n])
            if n + ahead < nu:
                m_curs.append(scores(n + ahead, nks[n + ahead], True))
            pv(n, nks[n], alpha)
        lane = _iota2((QS, LANES), 1)
        for qs in range(nqs):
            a0 = acc_scr[0, qs]
            a1 = acc_scr[1, qs]
            o0 = (a0 / a0[HD:HD + 1, :]).T
            o1 = (a1 / a1[HD:HD + 1, :]).T
            o = jnp.where(lane < HD, o0, pltpu.roll(o1, HD, 1))
            z = z_ref[qs * QS:(qs + 1) * QS, :].astype(F32)
            o_ref[qs * QS:(qs + 1) * QS, :] = (o * _silu(z)).astype(o_ref.dtype)


def _flash(qa, ka, vt, proj, zcol, tq):
    B, H, S, _ = qa.shape
    n = S // tq
    steps = [(i, j) for i in range(n) for j in range(i + 1)]
    qi = jnp.asarray([s[0] for s in steps], jnp.int32)
    kj = jnp.asarray([s[1] for s in steps], jnp.int32)
    nqs = tq // FLASH_QS
    grid_spec = pltpu.PrefetchScalarGridSpec(
        num_scalar_prefetch=2,
        grid=(B, H // 2, len(steps)),
        in_specs=[pl.BlockSpec((None, 2, tq, LANES), lambda b, p, s, qi, kj: (b, p, qi[s], 0)),
                  pl.BlockSpec((None, 2, tq, LANES), lambda b, p, s, qi, kj: (b, p, kj[s], 0)),
                  pl.BlockSpec((None, 2, LANES, tq), lambda b, p, s, qi, kj: (b, p, 0, kj[s])),
                  pl.BlockSpec((None, tq, LANES), lambda b, p, s, qi, kj: (b, qi[s], zcol + p))],
        out_specs=pl.BlockSpec((None, tq, LANES), lambda b, p, s, qi, kj: (b, qi[s], p)),
        scratch_shapes=[pltpu.VMEM((FLASH_AHEAD1, tq, FLASH_QS), F32),
                        pltpu.VMEM((2, tq, FLASH_QS), BF16),
                        pltpu.VMEM((2, nqs, 8, FLASH_QS), F32),
                        pltpu.VMEM((2, nqs, LANES, FLASH_QS), F32)])
    return pl.pallas_call(
        _flash_kernel,
        grid_spec=grid_spec,
        out_shape=jax.ShapeDtypeStruct((B, S, H * LANES // 2), BF16),
        compiler_params=_cparams(("parallel", "parallel", "arbitrary")),
        name="flash",
    )(qi, kj, qa, ka, vt, proj)


def _cumsum_kernel(x_ref, o_ref, *, group):
    R = x_ref.shape[0]
    x = x_ref[...]
    up = jnp.where(_iota2((LANES, LANES), 0) <= _iota2((LANES, LANES), 1), 1.0, 0.0).astype(BF16)
    w = _dot_sel_r(x, up)
    tot = jnp.broadcast_to(w[:, LANES - 1:LANES], (R, LANES))
    ri = _iota2((R, R), 0)
    ci = _iota2((R, R), 1)
    prev = jnp.where((ci < ri) & ((ci // group) == (ri // group)), 1.0, 0.0).astype(BF16)
    o_ref[...] = w + _dot_sel_l(prev, tot)


def _cumsum_rows(x, group):
    R = x.shape[0]
    rb = min(R, 256)
    return pl.pallas_call(
        functools.partial(_cumsum_kernel, group=group),
        grid=(R // rb,),
        in_specs=[pl.BlockSpec((rb, LANES), lambda r: (r, 0))],
        out_specs=pl.BlockSpec((rb, LANES), lambda r: (r, 0)),
        out_shape=jax.ShapeDtypeStruct((R, LANES), F32),
        compiler_params=_cparams(("parallel",)),
        name="cumsum",
    )(x)


def _foxsamp_kernel(q_ref, k_ref, v_ref, z_ref, sm_ref, fb_ref, tot_ref, fcc_ref, ck_ref, cv_ref,
                    o_ref, ko_ref, vo_ref, lf_ref):
    S = q_ref.shape[0]
    HD = ck_ref.shape[1]
    p = pl.program_id(1)
    logf = _logf_from_small(sm_ref[...], fb_ref[...])
    lf_ref[...] = logf
    ltri = jnp.where(_iota2((S, S), 0) >= _iota2((S, S), 1), 1.0, 0.0).astype(BF16)
    fc = _dot_sel_l(ltri, logf) + tot_ref[...]
    fct = jnp.concatenate([fc, jnp.zeros((LANES - S, LANES), F32)], axis=0).T
    lane = _iota2((S, LANES), 1)
    row = _iota2((LANES, LANES), 0)
    causal = _iota2((S, S), 0) >= _iota2((S, S), 1)
    scale = HD ** -0.5
    for hh in range(2):
        h = 2 * p + hh
        fq = jnp.sum(jnp.where(lane == h, fc, 0.0), axis=1, keepdims=True)
        fkn = jnp.sum(jnp.where(row == h, fct, 0.0), axis=0, keepdims=True)[:, :S]
        q = q_ref[:, hh * HD:(hh + 1) * HD].astype(F32) * scale
        kn = k_ref[:, hh * HD:(hh + 1) * HD].astype(F32)
        vn = v_ref[:, hh * HD:(hh + 1) * HD].astype(F32)
        ko_ref[hh] = kn
        vo_ref[hh] = vn
        sc = _dot(q, ck_ref[hh]) + fq - fcc_ref[hh]
        sn = jnp.where(causal, _dot_nt(q, kn) + fq - fkn, -jnp.inf)
        m = jnp.maximum(jnp.max(sc, axis=1, keepdims=True), jnp.max(sn, axis=1, keepdims=True))
        pc = jnp.exp(sc - m)
        pn = jnp.exp(sn - m)
        l = jnp.sum(pc, axis=1, keepdims=True) + jnp.sum(pn, axis=1, keepdims=True)
        o = (_dot_nt(pc, cv_ref[hh]) + _dot(pn, vn)) / l
        o_ref[:, hh * HD:(hh + 1) * HD] = o * _silu(z_ref[:, hh * HD:(hh + 1) * HD].astype(F32))


def _foxsamp(proj, small, cols, fb_row, tot, fcc, cache_k, cache_v, l):
    B, S, _ = proj.shape
    _, _, H, HD, P = cache_k.shape
    cq, ck, cv, cz = cols
    pair = lambda c: pl.BlockSpec((None, S, LANES), lambda b, p: (b, 0, c + p))
    kv_out = jax.ShapeDtypeStruct((B, H, S, HD), F32)
    return pl.pallas_call(
        _foxsamp_kernel,
        grid=(B, H // 2),
        in_specs=[pair(cq), pair(ck), pair(cv), pair(cz),
                  pl.BlockSpec((None, S, LANES), lambda b, p: (b, 0, 0)),
                  pl.BlockSpec((1, LANES), lambda b, p: (0, 0)),
                  pl.BlockSpec((None, 1, LANES), lambda b, p: (b, 0, 0)),
                  pl.BlockSpec((None, 2, 1, P), lambda b, p: (b, p, 0, 0)),
                  pl.BlockSpec((None, None, 2, HD, P), lambda b, p: (l, b, p, 0, 0)),
                  pl.BlockSpec((None, None, 2, HD, P), lambda b, p: (l, b, p, 0, 0))],
        out_specs=[pl.BlockSpec((None, S, LANES), lambda b, p: (b, 0, p)),
                   pl.BlockSpec((None, 2, S, HD), lambda b, p: (b, p, 0, 0)),
                   pl.BlockSpec((None, 2, S, HD), lambda b, p: (b, p, 0, 0)),
                   pl.BlockSpec((None, S, LANES), lambda b, p: (b, 0, 0))],
        out_shape=[jax.ShapeDtypeStruct((B, S, H * HD), F32), kv_out, kv_out,
                   jax.ShapeDtypeStruct((B, S, LANES), F32)],
        compiler_params=_cparams(("parallel", "arbitrary")),
        name="foxsamp",
    )(proj, proj, proj, proj, small, fb_row, tot, fcc, cache_k, cache_v)


COL_MG = 0
COL_GQKV = 6
COL_FQKV = 9
COL_RQK = 12
COL_RV = 13
COL_RZ = 14
COL_FZ = 15
COL_GZ = 16


def _prep_w_in(w_in, D):
    BW = D // 2
    hk = BW // 2
    splits = (hk, hk, BW, BW, BW, BW, BW, H_FOX, BW, BW, BW, BW, H_GDN, H_GDN, BW, N_BRANCH * D)
    offs = np.cumsum((0,) + splits)
    seg = lambda i: w_in[..., offs[i]:offs[i + 1]]
    rq, rk, rv, rz, fq, fk, fv, ff, fz, gq, gk, gv, ga, gb, gz, mg = [seg(i) for i in range(16)]
    main = jnp.concatenate([mg, gq, gk, gv, fq, fk, fv, rq, rk, rv, rz, fz, gz], axis=-1)
    pad = jnp.zeros(w_in.shape[:-1] + (LANES - H_FOX - 2 * H_GDN,), w_in.dtype)
    small = jnp.concatenate([ff, ga, gb, pad], axis=-1)
    return main.astype(BF16), small.astype(BF16)


def _lane_row(vals, off):
    L, n = vals.shape
    return jnp.zeros((L, 1, LANES), F32).at[:, 0, off:off + n].set(vals)


def kernel(x_prompt, x_sample, c_prompt, c_sample, cache_fox_k, cache_fox_v, cache_fox_logf,
           state_ret, state_gdn, state_gdn_conv, norm_w, ada_w, ada_b, w_in, fox_f_bias,
           gdn_a_log, gdn_dt_bias, gdn_conv_w, ret_norm_w, gdn_norm_w, w_branch, w_out,
           final_norm_w):
    B, S, D = x_prompt.shape
    BS, SS, _ = x_sample.shape
    L = ada_w.shape[0]
    P = cache_fox_k.shape[3]
    DK_RET, DV_RET = state_ret.shape[-2:]
    DK_GDN, DV_GDN = state_gdn.shape[-2:]

    rows = -(-(B + BS) // 8) * 8
    c_all = jnp.concatenate([c_prompt, c_sample, jnp.zeros((rows - B - BS, D), F32)], axis=0)
    mod = _adaln(c_all, ada_w, ada_b)
    mod_p = mod[:, :, :B].reshape(L, 3, B, 1, D)
    mod_s = mod[:, :, B:B + BS].reshape(L, 3, BS, 1, D)

    w_main, w_small = _prep_w_in(w_in, D)
    w_branch_b = w_branch.astype(BF16)
    w_out_b = w_out.astype(BF16)
    norm_w3 = norm_w.reshape(L, 1, D)
    final_w = final_norm_w.reshape(1, D)
    ret_nw = ret_norm_w.reshape(L, H_RET, 1, DV_RET)
    gdn_nw = gdn_norm_w.reshape(L, 1, DV_GDN)
    fb_rows = _lane_row(fox_f_bias, 0)
    a_rows = _lane_row(gdn_a_log, H_FOX)
    dt_rows = _lane_row(gdn_dt_bias, H_FOX)

    tm_p = min(S, 2048)
    tmm_p = min(S, 512)
    t_ret = min(S, 256)
    t_gdn = min(S, 256)
    t_prep = min(S, 256)
    t_flash = min(S, 1024)

    tabs_p = _ret_tables(t_ret, 0, S, DK_RET)
    tabs_s = _ret_tables(SS, P, SS, DK_RET)

    fcc = _cumsum_rows(cache_fox_logf.reshape(-1, LANES), P // LANES).reshape(L, BS, H_FOX, 1, P)
    tot = jnp.zeros((L, BS, 1, LANES), F32).at[:, :, 0, :H_FOX].set(fcc[:, :, :, 0, P - 1])

    zeros_ret = jnp.zeros((B, H_RET, DK_RET, DV_RET), F32)
    zeros_gdn = jnp.zeros((B, H_GDN, DK_GDN, DV_GDN), F32)
    zeros_conv = jnp.zeros((B, CONV_W - 1, gdn_conv_w.shape[-1]), F32)

    xp, xs = x_prompt, x_sample
    cache_kt = jnp.swapaxes(cache_fox_k, 3, 4)
    cache_vt = jnp.swapaxes(cache_fox_v, 3, 4)
    pk = jnp.zeros((L, B, H_FOX, cache_fox_k.shape[-1], S), F32)
    pv = jnp.zeros_like(pk)
    p_lf, p_ret, p_gdn, p_conv = [], [], [], []
    s_k, s_v, s_lf, s_ret, s_gdn, s_conv = [], [], [], [], [], []
    for l in range(L):
        final = l == L - 1
        proj, small = _inproj(xp, mod_p, l, norm_w3, w_main, w_small, 1, tm_p)
        o_ret, st = _ret(proj, (COL_RQK, COL_RV, COL_RZ), tabs_p, ret_nw[l], zeros_ret, t_ret, BF16)
        p_ret.append(st)
        o_gdn, st, cv = _gdn(proj, small, (COL_GQKV // 3, COL_GZ), gdn_conv_w[l], zeros_conv,
                             a_rows[l], dt_rows[l], gdn_nw[l], zeros_gdn, t_gdn, BF16)
        p_gdn.append(st)
        p_conv.append(cv)
        qa, ka, va, pk, pv, lf = _foxprep(proj, small, COL_FQKV // 3, fb_rows[l], l, pk, pv, L, t_prep)
        p_lf.append(lf)
        o_fox = _flash(qa, ka, va, proj, COL_FZ * 4, t_flash)
        xp = _merge(xp, mod_p, l, o_ret, o_fox, o_gdn, proj, w_branch_b, w_out_b, final_w,
                    1, tmm_p, final)
        proj, small = _inproj(xs, mod_s, l, norm_w3, w_main, w_small, BS, SS)
        o_ret, st = _ret(proj, (COL_RQK, COL_RV, COL_RZ), tabs_s, ret_nw[l], state_ret[l], SS, F32)
        s_ret.append(st)
        o_gdn, st, cv = _gdn(proj, small, (COL_GQKV // 3, COL_GZ), gdn_conv_w[l], state_gdn_conv[l],
                             a_rows[l], dt_rows[l], gdn_nw[l], state_gdn[l], SS, F32)
        s_gdn.append(st)
        s_conv.append(cv)
        o_fox, kk, vv, lf = _foxsamp(proj, small,
                                     (COL_FQKV * 4, (COL_FQKV + 1) * 4, (COL_FQKV + 2) * 4, COL_FZ * 4),
                                     fb_rows[l], tot[l], fcc[l], cache_kt, cache_vt, l)
        s_k.append(kk)
        s_v.append(vv)
        s_lf.append(lf)
        xs = _merge(xs, mod_s, l, o_ret, o_fox, o_gdn, proj, w_branch_b, w_out_b, final_w,
                    BS, SS, final)

    logf_out = lambda lfs: jnp.stack(lfs)[..., :H_FOX].transpose(0, 1, 3, 2)
    return (xp, xs,
            jnp.swapaxes(pk, 3, 4), jnp.swapaxes(pv, 3, 4), logf_out(p_lf), jnp.stack(p_ret), jnp.stack(p_gdn), jnp.stack(p_conv),
            jnp.stack(s_k), jnp.stack(s_v), logf_out(s_lf), jnp.stack(s_ret), jnp.stack(s_gdn),
            jnp.stack(s_conv))
```

```python
import functools

import numpy as np
import jax
import jax.numpy as jnp
from jax import lax
from jax.experimental import pallas as pl
from jax.experimental.pallas import tpu as pltpu

F32 = jnp.float32
BF16 = jnp.bfloat16

N_BRANCH = 3
H_RET = 4
H_FOX = 8
H_GDN = 4
CONV_W = 4
CHUNK = 64
NORM_EPS = 1e-6
ROPE_BASE = 10000.0
LOG2E = 1.4426950408889634
LANES = 128
GDN_CHUNK = 128
VMEM_LIMIT = 56 * 1024 * 1024


def _cparams(sem):
    return pltpu.CompilerParams(dimension_semantics=sem, vmem_limit_bytes=VMEM_LIMIT)


def _sigmoid(x):
    return 0.5 * jnp.tanh(0.5 * x) + 0.5


def _silu(x):
    return x * _sigmoid(x)


def _softplus(x):
    return jnp.maximum(x, 0.0) + jnp.log(1.0 + jnp.exp(-jnp.abs(x)))


def _dot(a, b):
    return jnp.dot(a.astype(BF16), b.astype(BF16), preferred_element_type=F32)


def _dot_nt(a, b):
    return lax.dot_general(a.astype(BF16), b.astype(BF16), (((1,), (1,)), ((), ())),
                           preferred_element_type=F32)


def _split3(x):
    hi = x.astype(BF16)
    r = x - hi.astype(F32)
    mid = r.astype(BF16)
    lo = (r - mid.astype(F32)).astype(BF16)
    return hi, mid, lo


def _dot_sel_l(m, x):
    hi, mid, lo = _split3(x)
    d = lambda y: jnp.dot(m, y, preferred_element_type=F32)
    return d(hi) + (d(mid) + d(lo))


def _dot_sel_r(x, m):
    hi, mid, lo = _split3(x)
    d = lambda y: jnp.dot(y, m, preferred_element_type=F32)
    return d(hi) + (d(mid) + d(lo))


def _iota2(shape, dim):
    return lax.broadcasted_iota(jnp.int32, shape, dim)


def _adaln_kernel(c_ref, w_ref, b_ref, o_ref):
    s = _silu(c_ref[...])
    o_ref[...] = _dot(s, w_ref[...]) + b_ref[...]


def _adaln(c_all, ada_w, ada_b):
    L, D, _ = ada_w.shape
    R = c_all.shape[0]
    return pl.pallas_call(
        _adaln_kernel,
        grid=(L, 3),
        in_specs=[pl.BlockSpec((R, D), lambda l, k: (0, 0)),
                  pl.BlockSpec((None, D, D), lambda l, k: (l, 0, k)),
                  pl.BlockSpec((None, None, 1, D), lambda l, k: (l, k, 0, 0))],
        out_specs=pl.BlockSpec((None, None, R, D), lambda l, k: (l, k, 0, 0)),
        out_shape=jax.ShapeDtypeStruct((L, 3, R, D), F32),
        compiler_params=_cparams(("parallel", "parallel")),
        name="adaln",
    )(c_all, ada_w, ada_b.reshape(L, 3, 1, D))


PROJ_TN = 512
PROJ_DTYPE = BF16


def _inproj_kernel(x_ref, nw_ref, sc_ref, sh_ref, w_ref, ws_ref, o_ref, os_ref, h_ref):
    bb, tm, D = x_ref.shape

    @pl.when(pl.program_id(2) == 0)
    def _():
        x = x_ref[...]
        ms = jnp.mean(x * x, axis=-1, keepdims=True)
        xn = (x * lax.rsqrt(ms + NORM_EPS)) * nw_ref[...]
        h = xn * (1.0 + sc_ref[...]) + sh_ref[...]
        hb = h.astype(BF16).reshape(bb * tm, D)
        h_ref[...] = hb
        os_ref[...] = jnp.dot(hb, ws_ref[...], preferred_element_type=F32).reshape(bb, tm, LANES)

    o_ref[...] = jnp.dot(h_ref[...], w_ref[...],
                         preferred_element_type=F32).reshape(bb, tm, PROJ_TN).astype(o_ref.dtype)


def _inproj(x, mod, l, norm_w, w_main, w_small, bb, tm):
    B, S, D = x.shape
    NP = w_main.shape[-1]
    nj = NP // PROJ_TN
    return pl.pallas_call(
        _inproj_kernel,
        grid=(B // bb, S // tm, nj),
        in_specs=[pl.BlockSpec((bb, tm, D), lambda b, i, j: (b, i, 0)),
                  pl.BlockSpec((None, 1, D), lambda b, i, j: (l, 0, 0)),
                  pl.BlockSpec((None, None, bb, 1, D), lambda b, i, j: (l, 1, b, 0, 0)),
                  pl.BlockSpec((None, None, bb, 1, D), lambda b, i, j: (l, 0, b, 0, 0)),
                  pl.BlockSpec((None, D, PROJ_TN), lambda b, i, j: (l, 0, j)),
                  pl.BlockSpec((None, D, LANES), lambda b, i, j: (l, 0, 0))],
        out_specs=[pl.BlockSpec((bb, tm, PROJ_TN), lambda b, i, j: (b, i, j)),
                   pl.BlockSpec((bb, tm, LANES), lambda b, i, j: (b, i, 0))],
        out_shape=[jax.ShapeDtypeStruct((B, S, NP), PROJ_DTYPE),
                   jax.ShapeDtypeStruct((B, S, LANES), F32)],
        scratch_shapes=[pltpu.VMEM((bb * tm, D), BF16)],
        compiler_params=_cparams(("parallel", "parallel", "arbitrary")),
        name="inproj",
    )(x, norm_w, mod, mod, w_main, w_small)


def _merge_kernel(x_ref, g_ref, o1_ref, o2_ref, o3_ref, mg_ref, wb_ref, wo_ref, fw_ref, y_ref,
                  *, final):
    bb, tm, D = x_ref.shape
    M = bb * tm
    acc = None
    for n, o_ref in enumerate((o1_ref, o2_ref, o3_ref)):
        br = jnp.dot(o_ref[...].reshape(M, o_ref.shape[-1]).astype(BF16), wb_ref[n],
                     preferred_element_type=F32)
        gate = _sigmoid(mg_ref[:, :, n * D:(n + 1) * D].astype(F32).reshape(M, D))
        term = br * gate
        acc = term if acc is None else acc + term
    out = jnp.dot(acc.astype(BF16), wo_ref[...], preferred_element_type=F32)
    xn = x_ref[...] + g_ref[...] * out.reshape(bb, tm, D)
    if final:
        ms = jnp.mean(xn * xn, axis=-1, keepdims=True)
        xn = (xn * lax.rsqrt(ms + NORM_EPS)) * fw_ref[...]
    y_ref[...] = xn


def _merge(x, mod, l, o_ret, o_fox, o_gdn, proj, w_branch, w_out, final_w, bb, tm, final):
    B, S, D = x.shape
    W = o_ret.shape[-1]
    return pl.pallas_call(
        functools.partial(_merge_kernel, final=final),
        grid=(B // bb, S // tm),
        in_specs=[pl.BlockSpec((bb, tm, D), lambda b, i: (b, i, 0)),
                  pl.BlockSpec((None, None, bb, 1, D), lambda b, i: (l, 2, b, 0, 0)),
                  pl.BlockSpec((bb, tm, W), lambda b, i: (b, i, 0)),
                  pl.BlockSpec((bb, tm, W), lambda b, i: (b, i, 0)),
                  pl.BlockSpec((bb, tm, W), lambda b, i: (b, i, 0)),
                  pl.BlockSpec((bb, tm, N_BRANCH * D), lambda b, i: (b, i, 0)),
                  pl.BlockSpec((None, N_BRANCH, W, D), lambda b, i: (l, 0, 0, 0)),
                  pl.BlockSpec((None, D, D), lambda b, i: (l, 0, 0)),
                  pl.BlockSpec((1, D), lambda b, i: (0, 0))],
        out_specs=pl.BlockSpec((bb, tm, D), lambda b, i: (b, i, 0)),
        out_shape=jax.ShapeDtypeStruct((B, S, D), F32),
        compiler_params=_cparams(("parallel", "parallel")),
        name="merge",
    )(x, mod, o_ret, o_fox, o_gdn, proj, w_branch, w_out, final_w)


def _ret_kernel(qk_ref, v_ref, z_ref, cos_ref, sin_ref, m_ref, dq_ref, dk_ref, dc_ref, nw_ref,
                s0_ref, o_ref, so_ref, s_ref):
    T = qk_ref.shape[0]
    H = s_ref.shape[0]
    DK = s0_ref.shape[1]
    t = pl.program_id(1)

    @pl.when(t == 0)
    def _():
        zero = jnp.zeros((DK, LANES), F32)
        for h in range(H):
            s0 = s0_ref[h]
            s_ref[h] = jnp.concatenate([s0, zero] if h % 2 == 0 else [zero, s0], axis=0)

    cos = cos_ref[...]
    sin = sin_ref[...]
    lane = _iota2((T, LANES), 1)
    first = (lane % DK) < (DK // 2)
    low = lane < DK

    def rot(x):
        sw = jnp.where(first, pltpu.roll(x, LANES - DK // 2, 1), pltpu.roll(x, DK // 2, 1))
        return x * cos + sw * sin

    nkq = H * DK
    hs = range(H)
    qcs = [rot(qk_ref[:, p * LANES:(p + 1) * LANES].astype(F32)) for p in range(H // 2)]
    kcs = [rot(qk_ref[:, nkq + p * LANES:nkq + (p + 1) * LANES].astype(F32)) * (DK ** -0.5)
           for p in range(H // 2)]
    kcbs = [kc.astype(BF16) for kc in kcs]
    sels = [low if h % 2 == 0 else jnp.logical_not(low) for h in hs]
    qms = [jnp.where(sels[h], qcs[h // 2], 0.0) for h in hs]
    vhs = [v_ref[:, h * LANES:(h + 1) * LANES].astype(BF16) for h in hs]
    ss = [s_ref[h] for h in hs]
    ams = [_dot_nt(qms[h], kcbs[h // 2]) * m_ref[h] for h in hs]
    kts = [(jnp.where(sels[h], kcs[h // 2], 0.0) * dk_ref[h]).T for h in hs]
    os_ = [_dot(ams[h], vhs[h]) + _dot(qms[h] * dq_ref[h], ss[h]) for h in hs]
    for h in hs:
        s_ref[h] = ss[h] * dc_ref[h] + _dot(kts[h], vhs[h])
    for h in hs:
        o = os_[h]
        ms = jnp.mean(o * o, axis=-1, keepdims=True)
        on = (o * lax.rsqrt(ms + NORM_EPS)) * nw_ref[h]
        z = z_ref[:, h * LANES:(h + 1) * LANES].astype(F32)
        o_ref[:, h * LANES:(h + 1) * LANES] = (on * _silu(z)).astype(o_ref.dtype)

    @pl.when(t == pl.num_programs(1) - 1)
    def _():
        for h in range(H):
            so_ref[h] = s_ref[h][(h % 2) * DK:(h % 2 + 1) * DK, :]


def _ret_tables(T, pos0, S, DK):
    half = DK // 2
    inv_freq = ROPE_BASE ** (-jnp.arange(half, dtype=F32) / half)
    pos = pos0 + jnp.arange(S)
    ang = pos.astype(F32)[:, None] * inv_freq[None, :]
    cos, sin = jnp.cos(ang), jnp.sin(ang)
    reps = LANES // DK
    cos_t = jnp.tile(jnp.concatenate([cos, cos], axis=1), (1, reps))
    sin_t = jnp.tile(jnp.concatenate([-sin, sin], axis=1), (1, reps))
    log_g = jnp.log1p(-jnp.exp2(-5.0 - jnp.arange(H_RET, dtype=F32)))
    i = jnp.arange(T)
    dist = jnp.abs(i[:, None] - i[None, :]).astype(F32)
    allowed = (i[None, :] // CHUNK) <= (i[:, None] // CHUNK)
    m = jnp.where(allowed[None], jnp.exp(dist[None] * log_g[:, None, None]), 0.0)
    fi = i.astype(F32)
    dq = jnp.exp((fi[None, :] + 1.0) * log_g[:, None])
    dk = jnp.exp((T - 1.0 - fi)[None, :] * log_g[:, None])
    dc = jnp.exp(T * log_g)
    bc = lambda a: jnp.broadcast_to(a[..., None], a.shape + (LANES,))
    return cos_t, sin_t, m, bc(dq), bc(dk), bc(dc[:, None])


def _ret(proj, cols, tabs, ret_nw, s0, T, out_dtype):
    B, S, _ = proj.shape
    H, DK, DV = s0.shape[1:]
    W = H * DV
    cqk, cv, cz = cols
    cos_t, sin_t, m, dq, dk, dc = tabs
    return pl.pallas_call(
        _ret_kernel,
        grid=(B, S // T),
        in_specs=[pl.BlockSpec((None, T, W), lambda b, t: (b, t, cqk)),
                  pl.BlockSpec((None, T, W), lambda b, t: (b, t, cv)),
                  pl.BlockSpec((None, T, W), lambda b, t: (b, t, cz)),
                  pl.BlockSpec((T, LANES), lambda b, t: (t, 0)),
                  pl.BlockSpec((T, LANES), lambda b, t: (t, 0)),
                  pl.BlockSpec((H, T, T), lambda b, t: (0, 0, 0)),
                  pl.BlockSpec((H, T, LANES), lambda b, t: (0, 0, 0)),
                  pl.BlockSpec((H, T, LANES), lambda b, t: (0, 0, 0)),
                  pl.BlockSpec((H, 1, LANES), lambda b, t: (0, 0, 0)),
                  pl.BlockSpec((H, 1, DV), lambda b, t: (0, 0, 0)),
                  pl.BlockSpec((None, H, DK, DV), lambda b, t: (b, 0, 0, 0))],
        out_specs=[pl.BlockSpec((None, T, W), lambda b, t: (b, t, 0)),
                   pl.BlockSpec((None, H, DK, DV), lambda b, t: (b, 0, 0, 0))],
        out_shape=[jax.ShapeDtypeStruct((B, S, W), out_dtype),
                   jax.ShapeDtypeStruct((B, H, DK, DV), F32)],
        scratch_shapes=[pltpu.VMEM((H, LANES, DV), F32)],
        compiler_params=_cparams(("parallel", "arbitrary")),
        name="ret",
    )(proj, proj, proj, cos_t, sin_t, m, dq, dk, dc, ret_nw, s0)


INV_BASE = 8


def _unit_lower_inverses(mats, eye, ri, ci):
    n = mats[0].shape[0]
    b = INV_BASE
    d = lambda x, y: jnp.dot(x, y, preferred_element_type=F32)
    same = (ri // b) == (ci // b)
    ads = [jnp.where(same, a, 0.0) for a in mats]
    xs = [eye - ad for ad in ads]
    ps = [ad.astype(BF16) for ad in ads]
    k = 2
    while k < b:
        ps = [d(p, p).astype(BF16) for p in ps]
        xs = [x + d(x.astype(BF16), p) for x, p in zip(xs, ps)]
        k *= 2
    while b < n:
        off = ((ri // (2 * b)) == (ci // (2 * b))) & ((ri // b) != (ci // b))
        xbs = [x.astype(BF16) for x in xs]
        ts = [d(jnp.where(off, a, 0.0).astype(BF16), xb) for a, xb in zip(mats, xbs)]
        xs = [x - d(xb, t.astype(BF16)) for x, xb, t in zip(xs, xbs, ts)]
        b *= 2
    return xs


def _gdn_kernel(u_ref, z_ref, sm_ref, cw_ref, cb_ref, al_ref, dt_ref, nw_ref, s0_ref,
                o_ref, so_ref, co_ref, ext_ref, s_ref):
    T, C3 = u_ref.shape
    H = s_ref.shape[0]
    DK = s_ref.shape[1]
    W = H * DK
    C = GDN_CHUNK
    TP = -(-T // C) * C
    t = pl.program_id(1)
    last = pl.num_programs(1) - 1
    base = 8
    lo = base - (CONV_W - 1)

    @pl.when(t == 0)
    def _():
        s_ref[...] = s0_ref[...]
        ext_ref[lo:base, :] = cb_ref[...]

    ext_ref[base:base + T, :] = u_ref[...].astype(F32)
    y = ext_ref[lo:lo + T, :] * cw_ref[0:1, :]
    for j in range(1, CONV_W):
        y = y + ext_ref[lo + j:lo + j + T, :] * cw_ref[j:j + 1, :]
    tail = ext_ref[lo + T:base + T, :]
    ext_ref[lo:base, :] = tail

    @pl.when(t == last)
    def _():
        co_ref[...] = tail

    qkv = _silu(y)
    sm = sm_ref[...]
    g_all = -jnp.exp(al_ref[...]) * _softplus(sm + dt_ref[...])
    b_all = _sigmoid(sm)
    z_all = z_ref[...].astype(F32)
    if TP > T:
        pad = lambda a: jnp.concatenate([a, jnp.zeros((TP - T, a.shape[1]), F32)], axis=0)
        qkv, g_all, b_all, z_all = pad(qkv), pad(g_all), pad(b_all), pad(z_all)

    ri = _iota2((C, C), 0)
    ci = _iota2((C, C), 1)
    tri = ri >= ci
    strict = ri > ci
    ltri = jnp.where(tri, 1.0, 0.0).astype(BF16)
    eye = jnp.where(ri == ci, 1.0, 0.0)

    nc = TP // C
    items = [(c, h) for c in range(nc) for h in range(H)]
    gcums = [_dot_sel_l(ltri, g_all[c * C:(c + 1) * C, :]) for c in range(nc)]

    qs, ks, vs, gcs, bcs, decs, kbs = [], [], [], [], [], [], []
    for c, h in items:
        r0 = c * C
        q = qkv[r0:r0 + C, h * DK:(h + 1) * DK]
        k = qkv[r0:r0 + C, W + h * DK:W + (h + 1) * DK]
        qs.append(q * lax.rsqrt(jnp.sum(q * q, axis=-1, keepdims=True) + NORM_EPS) * (DK ** -0.5))
        k = k * lax.rsqrt(jnp.sum(k * k, axis=-1, keepdims=True) + NORM_EPS)
        ks.append(k)
        kbs.append(k.astype(BF16))
        vs.append(qkv[r0:r0 + C, 2 * W + h * DK:2 * W + (h + 1) * DK])
        gc = gcums[c][:, 8 + h:9 + h]
        gcs.append(gc)
        bcs.append(b_all[r0:r0 + C, 12 + h:13 + h])
        gm = jnp.broadcast_to(gc, (C, C))
        decs.append(jnp.exp(jnp.where(tri, gm - gm.T, -jnp.inf)))
    kks = [_dot_nt(kb, kb) for kb in kbs]
    amats = [jnp.where(strict, kk * dec, 0.0) * bc for kk, dec, bc in zip(kks, decs, bcs)]
    tinvs = _unit_lower_inverses(amats, eye, ri, ci)
    egs = [jnp.exp(gc) for gc in gcs]
    rhss = [jnp.concatenate([k * (bc * eg), v * bc], axis=1)
            for k, v, bc, eg in zip(ks, vs, bcs, egs)]
    wus = [_dot(tinv, rhs) for tinv, rhs in zip(tinvs, rhss)]
    qks = [(_dot_nt(q, kb) * dec).astype(BF16) for q, kb, dec in zip(qs, kbs, decs)]
    qes = [(q * eg).astype(BF16) for q, eg in zip(qs, egs)]
    glasts = [gcums[c][C - 1:C, 8 + h:9 + h] for c, h in items]
    kdts = [(k * jnp.exp(gl - gc)).T.astype(BF16) for k, gl, gc in zip(ks, glasts, gcs)]
    sdec = [jnp.exp(gl) for gl in glasts]

    for c in range(nc):
        idx = [c * H + h for h in range(H)]
        ss = [s_ref[h] for h in range(H)]
        sbs = [s.astype(BF16) for s in ss]
        us = [wus[i][:, DK:] - _dot(wus[i][:, :DK], sb) for i, sb in zip(idx, sbs)]
        ubs = [u.astype(BF16) for u in us]
        os_ = [jnp.dot(qes[i], sb, preferred_element_type=F32)
               + jnp.dot(qks[i], ub, preferred_element_type=F32)
               for i, sb, ub in zip(idx, sbs, ubs)]
        for h, i in enumerate(idx):
            s_ref[h] = ss[h] * sdec[i] + jnp.dot(kdts[i], ubs[h], preferred_element_type=F32)
        for h, o in enumerate(os_):
            ms = jnp.mean(o * o, axis=-1, keepdims=True)
            on = (o * lax.rsqrt(ms + NORM_EPS)) * nw_ref[...]
            z = z_all[c * C:(c + 1) * C, h * DK:(h + 1) * DK]
            res = (on * _silu(z)).astype(o_ref.dtype)
            if TP > T:
                o_ref[:, h * DK:(h + 1) * DK] = res[:T, :]
            else:
                o_ref[c * C:(c + 1) * C, h * DK:(h + 1) * DK] = res

    @pl.when(t == last)
    def _():
        so_ref[...] = s_ref[...]


def _gdn(proj, small, cols, conv_w, conv_buf, a_row, dt_row, gdn_nw, s0, T, out_dtype):
    B, S, _ = proj.shape
    H, DK, DV = s0.shape[1:]
    W = H * DK
    C3 = conv_w.shape[-1]
    cu, cz = cols
    return pl.pallas_call(
        _gdn_kernel,
        grid=(B, S // T),
        in_specs=[pl.BlockSpec((None, T, C3), lambda b, t: (b, t, cu)),
                  pl.BlockSpec((None, T, W), lambda b, t: (b, t, cz)),
                  pl.BlockSpec((None, T, LANES), lambda b, t: (b, t, 0)),
                  pl.BlockSpec((CONV_W, C3), lambda b, t: (0, 0)),
                  pl.BlockSpec((None, CONV_W - 1, C3), lambda b, t: (b, 0, 0)),
                  pl.BlockSpec((1, LANES), lambda b, t: (0, 0)),
                  pl.BlockSpec((1, LANES), lambda b, t: (0, 0)),
                  pl.BlockSpec((1, DV), lambda b, t: (0, 0)),
                  pl.BlockSpec((None, H, DK, DV), lambda b, t: (b, 0, 0, 0))],
        out_specs=[pl.BlockSpec((None, T, W), lambda b, t: (b, t, 0)),
                   pl.BlockSpec((None, H, DK, DV), lambda b, t: (b, 0, 0, 0)),
                   pl.BlockSpec((None, CONV_W - 1, C3), lambda b, t: (b, 0, 0))],
        out_shape=[jax.ShapeDtypeStruct((B, S, W), out_dtype),
                   jax.ShapeDtypeStruct((B, H, DK, DV), F32),
                   jax.ShapeDtypeStruct((B, CONV_W - 1, C3), F32)],
        scratch_shapes=[pltpu.VMEM((T + 8, C3), F32),
                        pltpu.VMEM((H, DK, DV), F32)],
        compiler_params=_cparams(("parallel", "arbitrary")),
        name="gdn",
    )(proj, proj, small, conv_w, conv_buf, a_row, dt_row, gdn_nw, s0)


def _logf_from_small(sm, fb):
    lane = _iota2(sm.shape, 1)
    return jnp.where(lane < H_FOX, -_softplus(-(sm + fb)), 0.0)


def _foxprep_kernel(qkv_ref, sm_ref, fb_ref, qa_ref, ka_ref, vt_ref, ko_ref, vo_ref, lf_ref,
                    carry_ref):
    T = sm_ref.shape[0]
    H = qa_ref.shape[0]
    HD = ko_ref.shape[1]
    W = H * HD
    t = pl.program_id(1)

    @pl.when(t == 0)
    def _():
        carry_ref[...] = jnp.zeros_like(carry_ref)

    logf = _logf_from_small(sm_ref[...], fb_ref[...])
    lf_ref[...] = logf
    ltri = jnp.where(_iota2((T, T), 0) >= _iota2((T, T), 1), 1.0, 0.0).astype(BF16)
    fc = _dot_sel_l(ltri, logf) + carry_ref[0:1, :]
    carry_ref[0:1, :] = fc[T - 1:T, :]
    f_hi, f_mid, f_lo = [p.astype(F32) for p in _split3(fc * LOG2E)]

    lane = _iota2((T, LANES), 1)
    low = lane < HD
    one_q = jnp.where((lane >= HD + 3) & (lane < HD + 6), 1.0, 0.0)
    one_k = jnp.where((lane >= HD) & (lane < HD + 3), 1.0, 0.0)
    ones_rows = jnp.where(_iota2((LANES - HD, T), 0) == 0, 1.0, 0.0)
    scale = (HD ** -0.5) * LOG2E
    for p in range(H // 2):
        qc = qkv_ref[:, p * LANES:(p + 1) * LANES].astype(F32) * scale
        kc = qkv_ref[:, W + p * LANES:W + (p + 1) * LANES].astype(F32)
        kct = kc.T
        vct = qkv_ref[:, 2 * W + p * LANES:2 * W + (p + 1) * LANES].astype(F32).T
        for half in range(2):
            h = 2 * p + half
            if half == 1:
                qh, kh = [pltpu.roll(a, HD, 1) for a in (qc, kc)]
            else:
                qh, kh = qc, kc
            vht = vct[half * HD:(half + 1) * HD, :]
            c_hi, c_mid, c_lo = [a[:, h:h + 1] for a in (f_hi, f_mid, f_lo)]
            ex_q = jnp.where(lane == HD, c_hi,
                             jnp.where(lane == HD + 1, c_mid,
                                       jnp.where(lane == HD + 2, c_lo, one_q)))
            ex_k = jnp.where(lane == HD + 3, -c_hi,
                             jnp.where(lane == HD + 4, -c_mid,
                                       jnp.where(lane == HD + 5, -c_lo, one_k)))
            qa_ref[h] = jnp.where(low, qh, ex_q).astype(BF16)
            ka_ref[h] = jnp.where(low, kh, ex_k).astype(BF16)
            vt_ref[h] = jnp.concatenate([vht, ones_rows], axis=0).astype(BF16)
            ko_ref[h] = kct[half * HD:(half + 1) * HD, :]
            vo_ref[h] = vht


def _foxprep(proj, small, col, fb_row, l, k_prev, v_prev, L, T):
    B, S, _ = proj.shape
    H = H_FOX
    HD = 512 // H
    aug = jax.ShapeDtypeStruct((B, H, S, LANES), BF16)
    augt = jax.ShapeDtypeStruct((B, H, LANES, S), BF16)
    kv = jax.ShapeDtypeStruct((L, B, H, HD, S), F32)
    aug_spec = pl.BlockSpec((None, H, T, LANES), lambda b, t: (b, 0, t, 0))
    augt_spec = pl.BlockSpec((None, H, LANES, T), lambda b, t: (b, 0, 0, t))
    kv_spec = pl.BlockSpec((None, None, H, HD, T), lambda b, t: (l, b, 0, 0, t))
    in_specs = [pl.BlockSpec((None, T, 3 * H * HD), lambda b, t: (b, t, col)),
                pl.BlockSpec((None, T, LANES), lambda b, t: (b, t, 0)),
                pl.BlockSpec((1, LANES), lambda b, t: (0, 0)),
                pl.BlockSpec(memory_space=pl.ANY), pl.BlockSpec(memory_space=pl.ANY)]
    args = [proj, small, fb_row, k_prev, v_prev]
    aliases = {3: 3, 4: 4}
    kernel = lambda a, b, c, _k, _v, *rest: _foxprep_kernel(a, b, c, *rest)
    return pl.pallas_call(
        kernel,
        grid=(B, S // T),
        in_specs=in_specs,
        out_specs=[aug_spec, aug_spec, augt_spec, kv_spec, kv_spec,
                   pl.BlockSpec((None, T, LANES), lambda b, t: (b, t, 0))],
        out_shape=[aug, aug, augt, kv, kv, jax.ShapeDtypeStruct((B, S, LANES), F32)],
        scratch_shapes=[pltpu.VMEM((8, LANES), F32)],
        input_output_aliases=aliases,
        compiler_params=_cparams(("parallel", "arbitrary")),
        name="foxprep",
    )(*args)


FLASH_QS = 256
FLASH_HEADS = 2
FLASH_AHEAD1 = 6


def _flash_kernel(qi_ref, kj_ref, qa_ref, ka_ref, vt_ref, z_ref, o_ref,
                  s_scr, p_scr, m_scr, acc_scr):
    TQ = qa_ref.shape[1]
    TK = ka_ref.shape[1]
    QS = FLASH_QS
    HD = LANES // 2
    nqs = TQ // QS
    HPS = qa_ref.shape[0]
    st = pl.program_id(2)
    i = qi_ref[st]
    j = kj_ref[st]
    nt = (((1,), (1,)), ((), ()))

    @pl.when(j == 0)
    def _():
        m_scr[...] = jnp.full(m_scr.shape, -jnp.inf, F32)
        acc_scr[...] = jnp.zeros_like(acc_scr)

    def unit(n):
        return n // HPS, n % HPS

    def scores(n, nk, diag):
        qs, hh = unit(n)
        slot = n % FLASH_AHEAD1
        q = qa_ref[hh, qs * QS:(qs + 1) * QS, :]
        s = lax.dot_general(ka_ref[hh, 0:nk, :], q, nt, preferred_element_type=F32)
        if diag:
            keep = _iota2((QS, QS), 0) <= _iota2((QS, QS), 1)
            last = jnp.where(keep, s[nk - QS:nk, :], -jnp.inf)
            s_scr[slot, nk - QS:nk, :] = last
            m_cur = jnp.max(last, axis=0, keepdims=True)
            if nk > QS:
                s_scr[slot, 0:nk - QS, :] = s[0:nk - QS, :]
                m_cur = jnp.maximum(m_cur, jnp.max(s[0:nk - QS, :], axis=0, keepdims=True))
            return m_cur
        s_scr[slot, 0:nk, :] = s
        return jnp.max(s, axis=0, keepdims=True)

    def softmax(n, nk, m_cur):
        qs, hh = unit(n)
        m_prev = m_scr[hh, qs, 0:1, :]
        m_new = jnp.maximum(m_prev, m_cur)
        p_scr[n % 2, 0:nk, :] = jnp.exp2(s_scr[n % FLASH_AHEAD1, 0:nk, :] - m_new).astype(BF16)
        m_scr[hh, qs, 0:1, :] = m_new
        return jnp.exp2(m_prev - m_new)

    def pv(n, nk, alpha):
        qs, hh = unit(n)
        o = jnp.dot(vt_ref[hh, :, 0:nk], p_scr[n % 2, 0:nk, :], preferred_element_type=F32)
        acc_scr[hh, qs] = acc_scr[hh, qs] * alpha + o

    ahead = FLASH_AHEAD1 - 1
    nu = HPS * nqs

    @pl.when(j < i)
    def _():
        m_curs = [scores(n, TK, False) for n in range(ahead)]
        for n in range(nu):
            alpha = softmax(n, TK, m_curs[n])
            if n + ahead < nu:
                m_curs.append(scores(n + ahead, TK, False))
            pv(n, TK, alpha)

    @pl.when(j == i)
    def _():
        nks = [(n // HPS + 1) * QS for n in range(nu)]
        m_curs = [scores(n, nks[n], True) for n in range(ahead)]
        for n in range(nu):
            alpha = softmax(n, nks[n], m_curs[n])
            if n + ahead < nu:
                m_curs.append(scores(n + ahead, nks[n + ahead], True))
            pv(n, nks[n], alpha)
        lane = _iota2((QS, LANES), 1)
        for qs in range(nqs):
            for pr in range(HPS // 2):
                a0 = acc_scr[2 * pr, qs]
                a1 = acc_scr[2 * pr + 1, qs]
                o0 = (a0 / a0[HD:HD + 1, :]).T
                o1 = (a1 / a1[HD:HD + 1, :]).T
                o = jnp.where(lane < HD, o0, pltpu.roll(o1, HD, 1))
                rows, cols = slice(qs * QS, (qs + 1) * QS), slice(pr * LANES, (pr + 1) * LANES)
                z = z_ref[rows, cols].astype(F32)
                o_ref[rows, cols] = (o * _silu(z)).astype(o_ref.dtype)


def _flash(qa, ka, vt, proj, zcol, tq):
    B, H, S, _ = qa.shape
    n = S // tq
    steps = [(i, j) for i in range(n) for j in range(i + 1)]
    qi = jnp.asarray([s[0] for s in steps], jnp.int32)
    kj = jnp.asarray([s[1] for s in steps], jnp.int32)
    nqs = tq // FLASH_QS
    hps = FLASH_HEADS
    ow = hps * LANES // 2
    grid_spec = pltpu.PrefetchScalarGridSpec(
        num_scalar_prefetch=2,
        grid=(B, H // hps, len(steps)),
        in_specs=[pl.BlockSpec((None, hps, tq, LANES), lambda b, p, s, qi, kj: (b, p, qi[s], 0)),
                  pl.BlockSpec((None, hps, tq, LANES), lambda b, p, s, qi, kj: (b, p, kj[s], 0)),
                  pl.BlockSpec((None, hps, LANES, tq), lambda b, p, s, qi, kj: (b, p, 0, kj[s])),
                  pl.BlockSpec((None, tq, ow), lambda b, p, s, qi, kj: (b, qi[s], zcol // (ow // LANES) + p))],
        out_specs=pl.BlockSpec((None, tq, ow), lambda b, p, s, qi, kj: (b, qi[s], p)),
        scratch_shapes=[pltpu.VMEM((FLASH_AHEAD1, tq, FLASH_QS), F32),
                        pltpu.VMEM((2, tq, FLASH_QS), BF16),
                        pltpu.VMEM((hps, nqs, 8, FLASH_QS), F32),
                        pltpu.VMEM((hps, nqs, LANES, FLASH_QS), F32)])
    return pl.pallas_call(
        _flash_kernel,
        grid_spec=grid_spec,
        out_shape=jax.ShapeDtypeStruct((B, S, H * LANES // 2), BF16),
        compiler_params=_cparams(("parallel", "parallel", "arbitrary")),
        name="flash",
    )(qi, kj, qa, ka, vt, proj)


def _cumsum_kernel(x_ref, o_ref, *, group):
    R = x_ref.shape[0]
    x = x_ref[...]
    up = jnp.where(_iota2((LANES, LANES), 0) <= _iota2((LANES, LANES), 1), 1.0, 0.0).astype(BF16)
    w = _dot_sel_r(x, up)
    tot = jnp.broadcast_to(w[:, LANES - 1:LANES], (R, LANES))
    ri = _iota2((R, R), 0)
    ci = _iota2((R, R), 1)
    prev = jnp.where((ci < ri) & ((ci // group) == (ri // group)), 1.0, 0.0).astype(BF16)
    o_ref[...] = w + _dot_sel_l(prev, tot)


def _cumsum_rows(x, group):
    R = x.shape[0]
    rb = min(R, 256)
    return pl.pallas_call(
        functools.partial(_cumsum_kernel, group=group),
        grid=(R // rb,),
        in_specs=[pl.BlockSpec((rb, LANES), lambda r: (r, 0))],
        out_specs=pl.BlockSpec((rb, LANES), lambda r: (r, 0)),
        out_shape=jax.ShapeDtypeStruct((R, LANES), F32),
        compiler_params=_cparams(("parallel",)),
        name="cumsum",
    )(x)


def _foxsamp_kernel(q_ref, k_ref, v_ref, z_ref, sm_ref, fb_ref, tot_ref, fcc_ref, ck_ref, cv_ref,
                    o_ref, ko_ref, vo_ref, lf_ref):
    S = q_ref.shape[0]
    HD = ck_ref.shape[1]
    p = pl.program_id(1)
    logf = _logf_from_small(sm_ref[...], fb_ref[...])
    lf_ref[...] = logf
    ltri = jnp.where(_iota2((S, S), 0) >= _iota2((S, S), 1), 1.0, 0.0).astype(BF16)
    fc = _dot_sel_l(ltri, logf) + tot_ref[...]
    fct = jnp.concatenate([fc, jnp.zeros((LANES - S, LANES), F32)], axis=0).T
    lane = _iota2((S, LANES), 1)
    row = _iota2((LANES, LANES), 0)
    causal = _iota2((S, S), 0) >= _iota2((S, S), 1)
    scale = HD ** -0.5
    for hh in range(2):
        h = 2 * p + hh
        fq = jnp.sum(jnp.where(lane == h, fc, 0.0), axis=1, keepdims=True)
        fkn = jnp.sum(jnp.where(row == h, fct, 0.0), axis=0, keepdims=True)[:, :S]
        q = q_ref[:, hh * HD:(hh + 1) * HD].astype(F32) * scale
        kn = k_ref[:, hh * HD:(hh + 1) * HD].astype(F32)
        vn = v_ref[:, hh * HD:(hh + 1) * HD].astype(F32)
        ko_ref[hh] = kn
        vo_ref[hh] = vn
        sc = _dot(q, ck_ref[hh]) + fq - fcc_ref[hh]
        sn = jnp.where(causal, _dot_nt(q, kn) + fq - fkn, -jnp.inf)
        m = jnp.maximum(jnp.max(sc, axis=1, keepdims=True), jnp.max(sn, axis=1, keepdims=True))
        pc = jnp.exp(sc - m)
        pn = jnp.exp(sn - m)
        l = jnp.sum(pc, axis=1, keepdims=True) + jnp.sum(pn, axis=1, keepdims=True)
        o = (_dot_nt(pc, cv_ref[hh]) + _dot(pn, vn)) / l
        o_ref[:, hh * HD:(hh + 1) * HD] = o * _silu(z_ref[:, hh * HD:(hh + 1) * HD].astype(F32))


def _foxsamp(proj, small, cols, fb_row, tot, fcc, cache_k, cache_v, l):
    B, S, _ = proj.shape
    _, _, H, HD, P = cache_k.shape
    cq, ck, cv, cz = cols
    pair = lambda c: pl.BlockSpec((None, S, LANES), lambda b, p: (b, 0, c + p))
    kv_out = jax.ShapeDtypeStruct((B, H, S, HD), F32)
    return pl.pallas_call(
        _foxsamp_kernel,
        grid=(B, H // 2),
        in_specs=[pair(cq), pair(ck), pair(cv), pair(cz),
                  pl.BlockSpec((None, S, LANES), lambda b, p: (b, 0, 0)),
                  pl.BlockSpec((1, LANES), lambda b, p: (0, 0)),
                  pl.BlockSpec((None, 1, LANES), lambda b, p: (b, 0, 0)),
                  pl.BlockSpec((None, 2, 1, P), lambda b, p: (b, p, 0, 0)),
                  pl.BlockSpec((None, None, 2, HD, P), lambda b, p: (l, b, p, 0, 0)),
                  pl.BlockSpec((None, None, 2, HD, P), lambda b, p: (l, b, p, 0, 0))],
        out_specs=[pl.BlockSpec((None, S, LANES), lambda b, p: (b, 0, p)),
                   pl.BlockSpec((None, 2, S, HD), lambda b, p: (b, p, 0, 0)),
                   pl.BlockSpec((None, 2, S, HD), lambda b, p: (b, p, 0, 0)),
                   pl.BlockSpec((None, S, LANES), lambda b, p: (b, 0, 0))],
        out_shape=[jax.ShapeDtypeStruct((B, S, H * HD), F32), kv_out, kv_out,
                   jax.ShapeDtypeStruct((B, S, LANES), F32)],
        compiler_params=_cparams(("parallel", "arbitrary")),
        name="foxsamp",
    )(proj, proj, proj, proj, small, fb_row, tot, fcc, cache_k, cache_v)


COL_MG = 0
COL_GQKV = 6
COL_FQKV = 9
COL_RQK = 12
COL_RV = 13
COL_RZ = 14
COL_FZ = 15
COL_GZ = 16


def _prep_w_in(w_in, D):
    BW = D // 2
    hk = BW // 2
    splits = (hk, hk, BW, BW, BW, BW, BW, H_FOX, BW, BW, BW, BW, H_GDN, H_GDN, BW, N_BRANCH * D)
    offs = np.cumsum((0,) + splits)
    seg = lambda i: w_in[..., offs[i]:offs[i + 1]]
    rq, rk, rv, rz, fq, fk, fv, ff, fz, gq, gk, gv, ga, gb, gz, mg = [seg(i) for i in range(16)]
    main = jnp.concatenate([mg, gq, gk, gv, fq, fk, fv, rq, rk, rv, rz, fz, gz], axis=-1)
    pad = jnp.zeros(w_in.shape[:-1] + (LANES - H_FOX - 2 * H_GDN,), w_in.dtype)
    small = jnp.concatenate([ff, ga, gb, pad], axis=-1)
    return main.astype(BF16), small.astype(BF16)


def _lane_row(vals, off):
    L, n = vals.shape
    return jnp.zeros((L, 1, LANES), F32).at[:, 0, off:off + n].set(vals)


def kernel(x_prompt, x_sample, c_prompt, c_sample, cache_fox_k, cache_fox_v, cache_fox_logf,
           state_ret, state_gdn, state_gdn_conv, norm_w, ada_w, ada_b, w_in, fox_f_bias,
           gdn_a_log, gdn_dt_bias, gdn_conv_w, ret_norm_w, gdn_norm_w, w_branch, w_out,
           final_norm_w):
    B, S, D = x_prompt.shape
    BS, SS, _ = x_sample.shape
    L = ada_w.shape[0]
    P = cache_fox_k.shape[3]
    DK_RET, DV_RET = state_ret.shape[-2:]
    DK_GDN, DV_GDN = state_gdn.shape[-2:]

    rows = -(-(B + BS) // 8) * 8
    c_all = jnp.concatenate([c_prompt, c_sample, jnp.zeros((rows - B - BS, D), F32)], axis=0)
    mod = _adaln(c_all, ada_w, ada_b)
    mod_p = mod[:, :, :B].reshape(L, 3, B, 1, D)
    mod_s = mod[:, :, B:B + BS].reshape(L, 3, BS, 1, D)

    w_main, w_small = _prep_w_in(w_in, D)
    w_branch_b = w_branch.astype(BF16)
    w_out_b = w_out.astype(BF16)
    norm_w3 = norm_w.reshape(L, 1, D)
    final_w = final_norm_w.reshape(1, D)
    ret_nw = ret_norm_w.reshape(L, H_RET, 1, DV_RET)
    gdn_nw = gdn_norm_w.reshape(L, 1, DV_GDN)
    fb_rows = _lane_row(fox_f_bias, 0)
    a_rows = _lane_row(gdn_a_log, H_FOX)
    dt_rows = _lane_row(gdn_dt_bias, H_FOX)

    tm_p = min(S, 2048)
    tmm_p = min(S, 512)
    t_ret = min(S, 256)
    t_gdn = min(S, 512)
    t_prep = min(S, 256)
    t_flash = min(S, 1024)

    tabs_p = _ret_tables(t_ret, 0, S, DK_RET)
    tabs_s = _ret_tables(SS, P, SS, DK_RET)

    fcc = _cumsum_rows(cache_fox_logf.reshape(-1, LANES), P // LANES).reshape(L, BS, H_FOX, 1, P)
    tot = jnp.zeros((L, BS, 1, LANES), F32).at[:, :, 0, :H_FOX].set(fcc[:, :, :, 0, P - 1])

    zeros_ret = jnp.zeros((B, H_RET, DK_RET, DV_RET), F32)
    zeros_gdn = jnp.zeros((B, H_GDN, DK_GDN, DV_GDN), F32)
    zeros_conv = jnp.zeros((B, CONV_W - 1, gdn_conv_w.shape[-1]), F32)

    xp, xs = x_prompt, x_sample
    cache_kt = jnp.swapaxes(cache_fox_k, 3, 4)
    cache_vt = jnp.swapaxes(cache_fox_v, 3, 4)
    pk = jnp.zeros((L, B, H_FOX, cache_fox_k.shape[-1], S), F32)
    pv = jnp.zeros_like(pk)
    p_lf, p_ret, p_gdn, p_conv = [], [], [], []
    s_k, s_v, s_lf, s_ret, s_gdn, s_conv = [], [], [], [], [], []
    for l in range(L):
        final = l == L - 1
        proj, small = _inproj(xp, mod_p, l, norm_w3, w_main, w_small, 1, tm_p)
        o_ret, st = _ret(proj, (COL_RQK, COL_RV, COL_RZ), tabs_p, ret_nw[l], zeros_ret, t_ret, BF16)
        p_ret.append(st)
        o_gdn, st, cv = _gdn(proj, small, (COL_GQKV // 3, COL_GZ), gdn_conv_w[l], zeros_conv,
                             a_rows[l], dt_rows[l], gdn_nw[l], zeros_gdn, t_gdn, BF16)
        p_gdn.append(st)
        p_conv.append(cv)
        qa, ka, va, pk, pv, lf = _foxprep(proj, small, COL_FQKV // 3, fb_rows[l], l, pk, pv, L, t_prep)
        p_lf.append(lf)
        o_fox = _flash(qa, ka, va, proj, COL_FZ * 4, t_flash)
        xp = _merge(xp, mod_p, l, o_ret, o_fox, o_gdn, proj, w_branch_b, w_out_b, final_w,
                    1, tmm_p, final)
        proj, small = _inproj(xs, mod_s, l, norm_w3, w_main, w_small, BS, SS)
        o_ret, st = _ret(proj, (COL_RQK, COL_RV, COL_RZ), tabs_s, ret_nw[l], state_ret[l], SS, F32)
        s_ret.append(st)
        o_gdn, st, cv = _gdn(proj, small, (COL_GQKV // 3, COL_GZ), gdn_conv_w[l], state_gdn_conv[l],
                             a_rows[l], dt_rows[l], gdn_nw[l], state_gdn[l], SS, F32)
        s_gdn.append(st)
        s_conv.append(cv)
        o_fox, kk, vv, lf = _foxsamp(proj, small,
                                     (COL_FQKV * 4, (COL_FQKV + 1) * 4, (COL_FQKV + 2) * 4, COL_FZ * 4),
                                     fb_rows[l], tot[l], fcc[l], cache_kt, cache_vt, l)
        s_k.append(kk)
        s_v.append(vv)
        s_lf.append(lf)
        xs = _merge(xs, mod_s, l, o_ret, o_fox, o_gdn, proj, w_branch_b, w_out_b, final_w,
                    BS, SS, final)

    logf_out = lambda lfs: jnp.stack(lfs)[..., :H_FOX].transpose(0, 1, 3, 2)
    return (xp, xs,
            jnp.swapaxes(pk, 3, 4), jnp.swapaxes(pv, 3, 4), logf_out(p_lf), jnp.stack(p_ret), jnp.stack(p_gdn), jnp.stack(p_conv),
            jnp.stack(s_k), jnp.stack(s_v), logf_out(s_lf), jnp.stack(s_ret), jnp.stack(s_gdn),
            jnp.stack(s_conv))
```

```python
import functools

import numpy as np
import jax
import jax.numpy as jnp
from jax import lax
from jax.experimental import pallas as pl
from jax.experimental.pallas import tpu as pltpu

F32 = jnp.float32
BF16 = jnp.bfloat16

N_BRANCH = 3
H_RET = 4
H_FOX = 8
H_GDN = 4
CONV_W = 4
CHUNK = 64
NORM_EPS = 1e-6
ROPE_BASE = 10000.0
LOG2E = 1.4426950408889634
LANES = 128
GDN_CHUNK = 128
VMEM_LIMIT = 56 * 1024 * 1024


def _cparams(sem):
    return pltpu.CompilerParams(dimension_semantics=sem, vmem_limit_bytes=VMEM_LIMIT)


def _sigmoid(x):
    return 0.5 * jnp.tanh(0.5 * x) + 0.5


def _silu(x):
    return x * _sigmoid(x)


def _softplus(x):
    return jnp.maximum(x, 0.0) + jnp.log(1.0 + jnp.exp(-jnp.abs(x)))


def _dot(a, b):
    return jnp.dot(a.astype(BF16), b.astype(BF16), preferred_element_type=F32)


def _dot_nt(a, b):
    return lax.dot_general(a.astype(BF16), b.astype(BF16), (((1,), (1,)), ((), ())),
                           preferred_element_type=F32)


def _split3(x):
    hi = x.astype(BF16)
    r = x - hi.astype(F32)
    mid = r.astype(BF16)
    lo = (r - mid.astype(F32)).astype(BF16)
    return hi, mid, lo


def _dot_sel_l(m, x):
    hi, mid, lo = _split3(x)
    d = lambda y: jnp.dot(m, y, preferred_element_type=F32)
    return d(hi) + (d(mid) + d(lo))


def _dot_sel_r(x, m):
    hi, mid, lo = _split3(x)
    d = lambda y: jnp.dot(y, m, preferred_element_type=F32)
    return d(hi) + (d(mid) + d(lo))


def _iota2(shape, dim):
    return lax.broadcasted_iota(jnp.int32, shape, dim)


def _adaln_kernel(c_ref, w_ref, b_ref, o_ref):
    s = _silu(c_ref[...])
    o_ref[...] = _dot(s, w_ref[...]) + b_ref[...]


def _adaln(c_all, ada_w, ada_b):
    L, D, _ = ada_w.shape
    R = c_all.shape[0]
    return pl.pallas_call(
        _adaln_kernel,
        grid=(L, 3),
        in_specs=[pl.BlockSpec((R, D), lambda l, k: (0, 0)),
                  pl.BlockSpec((None, D, D), lambda l, k: (l, 0, k)),
                  pl.BlockSpec((None, None, 1, D), lambda l, k: (l, k, 0, 0))],
        out_specs=pl.BlockSpec((None, None, R, D), lambda l, k: (l, k, 0, 0)),
        out_shape=jax.ShapeDtypeStruct((L, 3, R, D), F32),
        compiler_params=_cparams(("parallel", "parallel")),
        name="adaln",
    )(c_all, ada_w, ada_b.reshape(L, 3, 1, D))


PROJ_TN = 512
PROJ_DTYPE = BF16


def _inproj_kernel(x_ref, nw_ref, sc_ref, sh_ref, w_ref, ws_ref, o_ref, os_ref, h_ref):
    bb, tm, D = x_ref.shape

    @pl.when(pl.program_id(2) == 0)
    def _():
        x = x_ref[...]
        ms = jnp.mean(x * x, axis=-1, keepdims=True)
        xn = (x * lax.rsqrt(ms + NORM_EPS)) * nw_ref[...]
        h = xn * (1.0 + sc_ref[...]) + sh_ref[...]
        hb = h.astype(BF16).reshape(bb * tm, D)
        h_ref[...] = hb
        os_ref[...] = jnp.dot(hb, ws_ref[...], preferred_element_type=F32).reshape(bb, tm, LANES)

    o_ref[...] = jnp.dot(h_ref[...], w_ref[...],
                         preferred_element_type=F32).reshape(bb, tm, PROJ_TN).astype(o_ref.dtype)


def _inproj(x, mod, l, norm_w, w_main, w_small, bb, tm):
    B, S, D = x.shape
    NP = w_main.shape[-1]
    nj = NP // PROJ_TN
    return pl.pallas_call(
        _inproj_kernel,
        grid=(B // bb, S // tm, nj),
        in_specs=[pl.BlockSpec((bb, tm, D), lambda b, i, j: (b, i, 0)),
                  pl.BlockSpec((None, 1, D), lambda b, i, j: (l, 0, 0)),
                  pl.BlockSpec((None, None, bb, 1, D), lambda b, i, j: (l, 1, b, 0, 0)),
                  pl.BlockSpec((None, None, bb, 1, D), lambda b, i, j: (l, 0, b, 0, 0)),
                  pl.BlockSpec((None, D, PROJ_TN), lambda b, i, j: (l, 0, j)),
                  pl.BlockSpec((None, D, LANES), lambda b, i, j: (l, 0, 0))],
        out_specs=[pl.BlockSpec((bb, tm, PROJ_TN), lambda b, i, j: (b, i, j)),
                   pl.BlockSpec((bb, tm, LANES), lambda b, i, j: (b, i, 0))],
        out_shape=[jax.ShapeDtypeStruct((B, S, NP), PROJ_DTYPE),
                   jax.ShapeDtypeStruct((B, S, LANES), F32)],
        scratch_shapes=[pltpu.VMEM((bb * tm, D), BF16)],
        compiler_params=_cparams(("parallel", "parallel", "arbitrary")),
        name="inproj",
    )(x, norm_w, mod, mod, w_main, w_small)


def _merge_kernel(x_ref, g_ref, o1_ref, o2_ref, o3_ref, mg_ref, wb_ref, wo_ref, fw_ref, y_ref,
                  *, final):
    bb, tm, D = x_ref.shape
    M = bb * tm
    acc = None
    for n, o_ref in enumerate((o1_ref, o2_ref, o3_ref)):
        br = jnp.dot(o_ref[...].reshape(M, o_ref.shape[-1]).astype(BF16), wb_ref[n],
                     preferred_element_type=F32)
        gate = _sigmoid(mg_ref[:, :, n * D:(n + 1) * D].astype(F32).reshape(M, D))
        term = br * gate
        acc = term if acc is None else acc + term
    out = jnp.dot(acc.astype(BF16), wo_ref[...], preferred_element_type=F32)
    xn = x_ref[...] + g_ref[...] * out.reshape(bb, tm, D)
    if final:
        ms = jnp.mean(xn * xn, axis=-1, keepdims=True)
        xn = (xn * lax.rsqrt(ms + NORM_EPS)) * fw_ref[...]
    y_ref[...] = xn


def _merge(x, mod, l, o_ret, o_fox, o_gdn, proj, w_branch, w_out, final_w, bb, tm, final):
    B, S, D = x.shape
    W = o_ret.shape[-1]
    return pl.pallas_call(
        functools.partial(_merge_kernel, final=final),
        grid=(B // bb, S // tm),
        in_specs=[pl.BlockSpec((bb, tm, D), lambda b, i: (b, i, 0)),
                  pl.BlockSpec((None, None, bb, 1, D), lambda b, i: (l, 2, b, 0, 0)),
                  pl.BlockSpec((bb, tm, W), lambda b, i: (b, i, 0)),
                  pl.BlockSpec((bb, tm, W), lambda b, i: (b, i, 0)),
                  pl.BlockSpec((bb, tm, W), lambda b, i: (b, i, 0)),
                  pl.BlockSpec((bb, tm, N_BRANCH * D), lambda b, i: (b, i, 0)),
                  pl.BlockSpec((None, N_BRANCH, W, D), lambda b, i: (l, 0, 0, 0)),
                  pl.BlockSpec((None, D, D), lambda b, i: (l, 0, 0)),
                  pl.BlockSpec((1, D), lambda b, i: (0, 0))],
        out_specs=pl.BlockSpec((bb, tm, D), lambda b, i: (b, i, 0)),
        out_shape=jax.ShapeDtypeStruct((B, S, D), F32),
        compiler_params=_cparams(("parallel", "parallel")),
        name="merge",
    )(x, mod, o_ret, o_fox, o_gdn, proj, w_branch, w_out, final_w)


def _ret_kernel(qk_ref, v_ref, z_ref, cos_ref, sin_ref, m_ref, dq_ref, dk_ref, dc_ref, nw_ref,
                s0_ref, o_ref, so_ref, s_ref):
    T = qk_ref.shape[0]
    H = s_ref.shape[0]
    DK = s0_ref.shape[1]
    t = pl.program_id(1)

    @pl.when(t == 0)
    def _():
        zero = jnp.zeros((DK, LANES), F32)
        for h in range(H):
            s0 = s0_ref[h]
            s_ref[h] = jnp.concatenate([s0, zero] if h % 2 == 0 else [zero, s0], axis=0)

    cos = cos_ref[...]
    sin = sin_ref[...]
    lane = _iota2((T, LANES), 1)
    first = (lane % DK) < (DK // 2)
    low = lane < DK

    def rot(x):
        sw = jnp.where(first, pltpu.roll(x, LANES - DK // 2, 1), pltpu.roll(x, DK // 2, 1))
        return x * cos + sw * sin

    nkq = H * DK
    hs = range(H)
    qcs = [rot(qk_ref[:, p * LANES:(p + 1) * LANES].astype(F32)) for p in range(H // 2)]
    kcs = [rot(qk_ref[:, nkq + p * LANES:nkq + (p + 1) * LANES].astype(F32)) * (DK ** -0.5)
           for p in range(H // 2)]
    kcbs = [kc.astype(BF16) for kc in kcs]
    sels = [low if h % 2 == 0 else jnp.logical_not(low) for h in hs]
    qms = [jnp.where(sels[h], qcs[h // 2], 0.0) for h in hs]
    vhs = [v_ref[:, h * LANES:(h + 1) * LANES].astype(BF16) for h in hs]
    ss = [s_ref[h] for h in hs]
    ams = [_dot_nt(qms[h], kcbs[h // 2]) * m_ref[h] for h in hs]
    kts = [(jnp.where(sels[h], kcs[h // 2], 0.0) * dk_ref[h]).T for h in hs]
    os_ = [_dot(ams[h], vhs[h]) + _dot(qms[h] * dq_ref[h], ss[h]) for h in hs]
    for h in hs:
        s_ref[h] = ss[h] * dc_ref[h] + _dot(kts[h], vhs[h])
    for h in hs:
        o = os_[h]
        ms = jnp.mean(o * o, axis=-1, keepdims=True)
        on = (o * lax.rsqrt(ms + NORM_EPS)) * nw_ref[h]
        z = z_ref[:, h * LANES:(h + 1) * LANES].astype(F32)
        o_ref[:, h * LANES:(h + 1) * LANES] = (on * _silu(z)).astype(o_ref.dtype)

    @pl.when(t == pl.num_programs(1) - 1)
    def _():
        for h in range(H):
            so_ref[h] = s_ref[h][(h % 2) * DK:(h % 2 + 1) * DK, :]


def _ret_tables(T, pos0, S, DK):
    half = DK // 2
    inv_freq = ROPE_BASE ** (-jnp.arange(half, dtype=F32) / half)
    pos = pos0 + jnp.arange(S)
    ang = pos.astype(F32)[:, None] * inv_freq[None, :]
    cos, sin = jnp.cos(ang), jnp.sin(ang)
    reps = LANES // DK
    cos_t = jnp.tile(jnp.concatenate([cos, cos], axis=1), (1, reps))
    sin_t = jnp.tile(jnp.concatenate([-sin, sin], axis=1), (1, reps))
    log_g = jnp.log1p(-jnp.exp2(-5.0 - jnp.arange(H_RET, dtype=F32)))
    i = jnp.arange(T)
    dist = jnp.abs(i[:, None] - i[None, :]).astype(F32)
    allowed = (i[None, :] // CHUNK) <= (i[:, None] // CHUNK)
    m = jnp.where(allowed[None], jnp.exp(dist[None] * log_g[:, None, None]), 0.0)
    fi = i.astype(F32)
    dq = jnp.exp((fi[None, :] + 1.0) * log_g[:, None])
    dk = jnp.exp((T - 1.0 - fi)[None, :] * log_g[:, None])
    dc = jnp.exp(T * log_g)
    bc = lambda a: jnp.broadcast_to(a[..., None], a.shape + (LANES,))
    return cos_t, sin_t, m, bc(dq), bc(dk), bc(dc[:, None])


def _ret(proj, cols, tabs, ret_nw, s0, T, out_dtype):
    B, S, _ = proj.shape
    H, DK, DV = s0.shape[1:]
    W = H * DV
    cqk, cv, cz = cols
    cos_t, sin_t, m, dq, dk, dc = tabs
    return pl.pallas_call(
        _ret_kernel,
        grid=(B, S // T),
        in_specs=[pl.BlockSpec((None, T, W), lambda b, t: (b, t, cqk)),
                  pl.BlockSpec((None, T, W), lambda b, t: (b, t, cv)),
                  pl.BlockSpec((None, T, W), lambda b, t: (b, t, cz)),
                  pl.BlockSpec((T, LANES), lambda b, t: (t, 0)),
                  pl.BlockSpec((T, LANES), lambda b, t: (t, 0)),
                  pl.BlockSpec((H, T, T), lambda b, t: (0, 0, 0)),
                  pl.BlockSpec((H, T, LANES), lambda b, t: (0, 0, 0)),
                  pl.BlockSpec((H, T, LANES), lambda b, t: (0, 0, 0)),
                  pl.BlockSpec((H, 1, LANES), lambda b, t: (0, 0, 0)),
                  pl.BlockSpec((H, 1, DV), lambda b, t: (0, 0, 0)),
                  pl.BlockSpec((None, H, DK, DV), lambda b, t: (b, 0, 0, 0))],
        out_specs=[pl.BlockSpec((None, T, W), lambda b, t: (b, t, 0)),
                   pl.BlockSpec((None, H, DK, DV), lambda b, t: (b, 0, 0, 0))],
        out_shape=[jax.ShapeDtypeStruct((B, S, W), out_dtype),
                   jax.ShapeDtypeStruct((B, H, DK, DV), F32)],
        scratch_shapes=[pltpu.VMEM((H, LANES, DV), F32)],
        compiler_params=_cparams(("parallel", "arbitrary")),
        name="ret",
    )(proj, proj, proj, cos_t, sin_t, m, dq, dk, dc, ret_nw, s0)


INV_BASE = 8


def _unit_lower_inverses(mats, eye, ri, ci):
    n = mats[0].shape[0]
    b = INV_BASE
    d = lambda x, y: jnp.dot(x, y, preferred_element_type=F32)
    same = (ri // b) == (ci // b)
    ads = [jnp.where(same, a, 0.0) for a in mats]
    xs = [eye - ad for ad in ads]
    ps = [ad.astype(BF16) for ad in ads]
    k = 2
    while k < b:
        ps = [d(p, p).astype(BF16) for p in ps]
        xs = [x + d(x.astype(BF16), p) for x, p in zip(xs, ps)]
        k *= 2
    while b < n:
        off = ((ri // (2 * b)) == (ci // (2 * b))) & ((ri // b) != (ci // b))
        xbs = [x.astype(BF16) for x in xs]
        ts = [d(jnp.where(off, a, 0.0).astype(BF16), xb) for a, xb in zip(mats, xbs)]
        xs = [x - d(xb, t.astype(BF16)) for x, xb, t in zip(xs, xbs, ts)]
        b *= 2
    return xs


def _gdn_kernel(u_ref, z_ref, sm_ref, cw_ref, cb_ref, al_ref, dt_ref, nw_ref, s0_ref,
                o_ref, so_ref, co_ref, ext_ref, s_ref):
    T, C3 = u_ref.shape
    H = s_ref.shape[0]
    DK = s_ref.shape[1]
    W = H * DK
    C = GDN_CHUNK
    TP = -(-T // C) * C
    t = pl.program_id(1)
    last = pl.num_programs(1) - 1
    base = 8
    lo = base - (CONV_W - 1)

    @pl.when(t == 0)
    def _():
        s_ref[...] = s0_ref[...]
        ext_ref[lo:base, :] = cb_ref[...]

    ext_ref[base:base + T, :] = u_ref[...].astype(F32)
    y = ext_ref[lo:lo + T, :] * cw_ref[0:1, :]
    for j in range(1, CONV_W):
        y = y + ext_ref[lo + j:lo + j + T, :] * cw_ref[j:j + 1, :]
    tail = ext_ref[lo + T:base + T, :]
    ext_ref[lo:base, :] = tail

    @pl.when(t == last)
    def _():
        co_ref[...] = tail

    qkv = _silu(y)
    sm = sm_ref[...]
    g_all = -jnp.exp(al_ref[...]) * _softplus(sm + dt_ref[...])
    b_all = _sigmoid(sm)
    z_all = z_ref[...].astype(F32)
    if TP > T:
        pad = lambda a: jnp.concatenate([a, jnp.zeros((TP - T, a.shape[1]), F32)], axis=0)
        qkv, g_all, b_all, z_all = pad(qkv), pad(g_all), pad(b_all), pad(z_all)

    ri = _iota2((C, C), 0)
    ci = _iota2((C, C), 1)
    tri = ri >= ci
    strict = ri > ci
    ltri = jnp.where(tri, 1.0, 0.0).astype(BF16)
    eye = jnp.where(ri == ci, 1.0, 0.0)

    nc = TP // C
    items = [(c, h) for c in range(nc) for h in range(H)]
    gcums = [_dot_sel_l(ltri, g_all[c * C:(c + 1) * C, :]) for c in range(nc)]

    qs, ks, vs, gcs, bcs, decs, kbs = [], [], [], [], [], [], []
    for c, h in items:
        r0 = c * C
        q = qkv[r0:r0 + C, h * DK:(h + 1) * DK]
        k = qkv[r0:r0 + C, W + h * DK:W + (h + 1) * DK]
        qs.append(q * lax.rsqrt(jnp.sum(q * q, axis=-1, keepdims=True) + NORM_EPS) * (DK ** -0.5))
        k = k * lax.rsqrt(jnp.sum(k * k, axis=-1, keepdims=True) + NORM_EPS)
        ks.append(k)
        kbs.append(k.astype(BF16))
        vs.append(qkv[r0:r0 + C, 2 * W + h * DK:2 * W + (h + 1) * DK])
        gc = gcums[c][:, 8 + h:9 + h]
        gcs.append(gc)
        bcs.append(b_all[r0:r0 + C, 12 + h:13 + h])
        gm = jnp.broadcast_to(gc, (C, C))
        decs.append(jnp.exp(jnp.where(tri, gm - gm.T, -jnp.inf)))
    kks = [_dot_nt(kb, kb) for kb in kbs]
    amats = [jnp.where(strict, kk * dec, 0.0) * bc for kk, dec, bc in zip(kks, decs, bcs)]
    tinvs = _unit_lower_inverses(amats, eye, ri, ci)
    egs = [jnp.exp(gc) for gc in gcs]
    rhss = [jnp.concatenate([k * (bc * eg), v * bc], axis=1)
            for k, v, bc, eg in zip(ks, vs, bcs, egs)]
    wus = [_dot(tinv, rhs) for tinv, rhs in zip(tinvs, rhss)]
    qks = [(_dot_nt(q, kb) * dec).astype(BF16) for q, kb, dec in zip(qs, kbs, decs)]
    qes = [(q * eg).astype(BF16) for q, eg in zip(qs, egs)]
    glasts = [gcums[c][C - 1:C, 8 + h:9 + h] for c, h in items]
    kdts = [(k * jnp.exp(gl - gc)).T.astype(BF16) for k, gl, gc in zip(ks, glasts, gcs)]
    sdec = [jnp.exp(gl) for gl in glasts]

    for c in range(nc):
        idx = [c * H + h for h in range(H)]
        ss = [s_ref[h] for h in range(H)]
        sbs = [s.astype(BF16) for s in ss]
        us = [wus[i][:, DK:] - _dot(wus[i][:, :DK], sb) for i, sb in zip(idx, sbs)]
        ubs = [u.astype(BF16) for u in us]
        os_ = [jnp.dot(qes[i], sb, preferred_element_type=F32)
               + jnp.dot(qks[i], ub, preferred_element_type=F32)
               for i, sb, ub in zip(idx, sbs, ubs)]
        for h, i in enumerate(idx):
            s_ref[h] = ss[h] * sdec[i] + jnp.dot(kdts[i], ubs[h], preferred_element_type=F32)
        for h, o in enumerate(os_):
            ms = jnp.mean(o * o, axis=-1, keepdims=True)
            on = (o * lax.rsqrt(ms + NORM_EPS)) * nw_ref[...]
            z = z_all[c * C:(c + 1) * C, h * DK:(h + 1) * DK]
            res = (on * _silu(z)).astype(o_ref.dtype)
            if TP > T:
                o_ref[:, h * DK:(h + 1) * DK] = res[:T, :]
            else:
                o_ref[c * C:(c + 1) * C, h * DK:(h + 1) * DK] = res

    @pl.when(t == last)
    def _():
        so_ref[...] = s_ref[...]


def _gdn(proj, small, cols, conv_w, conv_buf, a_row, dt_row, gdn_nw, s0, T, out_dtype):
    B, S, _ = proj.shape
    H, DK, DV = s0.shape[1:]
    W = H * DK
    C3 = conv_w.shape[-1]
    cu, cz = cols
    return pl.pallas_call(
        _gdn_kernel,
        grid=(B, S // T),
        in_specs=[pl.BlockSpec((None, T, C3), lambda b, t: (b, t, cu)),
                  pl.BlockSpec((None, T, W), lambda b, t: (b, t, cz)),
                  pl.BlockSpec((None, T, LANES), lambda b, t: (b, t, 0)),
                  pl.BlockSpec((CONV_W, C3), lambda b, t: (0, 0)),
                  pl.BlockSpec((None, CONV_W - 1, C3), lambda b, t: (b, 0, 0)),
                  pl.BlockSpec((1, LANES), lambda b, t: (0, 0)),
                  pl.BlockSpec((1, LANES), lambda b, t: (0, 0)),
                  pl.BlockSpec((1, DV), lambda b, t: (0, 0)),
                  pl.BlockSpec((None, H, DK, DV), lambda b, t: (b, 0, 0, 0))],
        out_specs=[pl.BlockSpec((None, T, W), lambda b, t: (b, t, 0)),
                   pl.BlockSpec((None, H, DK, DV), lambda b, t: (b, 0, 0, 0)),
                   pl.BlockSpec((None, CONV_W - 1, C3), lambda b, t: (b, 0, 0))],
        out_shape=[jax.ShapeDtypeStruct((B, S, W), out_dtype),
                   jax.ShapeDtypeStruct((B, H, DK, DV), F32),
                   jax.ShapeDtypeStruct((B, CONV_W - 1, C3), F32)],
        scratch_shapes=[pltpu.VMEM((T + 8, C3), F32),
                        pltpu.VMEM((H, DK, DV), F32)],
        compiler_params=_cparams(("parallel", "arbitrary")),
        name="gdn",
    )(proj, proj, small, conv_w, conv_buf, a_row, dt_row, gdn_nw, s0)


def _logf_from_small(sm, fb):
    lane = _iota2(sm.shape, 1)
    return jnp.where(lane < H_FOX, -_softplus(-(sm + fb)), 0.0)


def _foxprep_kernel(qkv_ref, sm_ref, fb_ref, qa_ref, ka_ref, vt_ref, ko_ref, vo_ref, lf_ref,
                    carry_ref):
    T = sm_ref.shape[0]
    H = qa_ref.shape[0]
    HD = ko_ref.shape[1]
    W = H * HD
    t = pl.program_id(1)

    @pl.when(t == 0)
    def _():
        carry_ref[...] = jnp.zeros_like(carry_ref)

    logf = _logf_from_small(sm_ref[...], fb_ref[...])
    lf_ref[...] = logf
    ltri = jnp.where(_iota2((T, T), 0) >= _iota2((T, T), 1), 1.0, 0.0).astype(BF16)
    fc = _dot_sel_l(ltri, logf) + carry_ref[0:1, :]
    carry_ref[0:1, :] = fc[T - 1:T, :]
    f_hi, f_mid, f_lo = [p.astype(F32) for p in _split3(fc * LOG2E)]

    lane = _iota2((T, LANES), 1)
    low = lane < HD
    one_q = jnp.where((lane >= HD + 3) & (lane < HD + 6), 1.0, 0.0)
    one_k = jnp.where((lane >= HD) & (lane < HD + 3), 1.0, 0.0)
    ones_rows = jnp.where(_iota2((LANES - HD, T), 0) == 0, 1.0, 0.0)
    scale = (HD ** -0.5) * LOG2E
    for p in range(H // 2):
        qc = qkv_ref[:, p * LANES:(p + 1) * LANES].astype(F32) * scale
        kc = qkv_ref[:, W + p * LANES:W + (p + 1) * LANES].astype(F32)
        kct = kc.T
        vct = qkv_ref[:, 2 * W + p * LANES:2 * W + (p + 1) * LANES].astype(F32).T
        for half in range(2):
            h = 2 * p + half
            if half == 1:
                qh, kh = [pltpu.roll(a, HD, 1) for a in (qc, kc)]
            else:
                qh, kh = qc, kc
            vht = vct[half * HD:(half + 1) * HD, :]
            c_hi, c_mid, c_lo = [a[:, h:h + 1] for a in (f_hi, f_mid, f_lo)]
            ex_q = jnp.where(lane == HD, c_hi,
                             jnp.where(lane == HD + 1, c_mid,
                                       jnp.where(lane == HD + 2, c_lo, one_q)))
            ex_k = jnp.where(lane == HD + 3, -c_hi,
                             jnp.where(lane == HD + 4, -c_mid,
                                       jnp.where(lane == HD + 5, -c_lo, one_k)))
            qa_ref[h] = jnp.where(low, qh, ex_q).astype(BF16)
            ka_ref[h] = jnp.where(low, kh, ex_k).astype(BF16)
            vt_ref[h] = jnp.concatenate([vht, ones_rows], axis=0).astype(BF16)
            ko_ref[h] = kct[half * HD:(half + 1) * HD, :]
            vo_ref[h] = vht


def _foxprep(proj, small, col, fb_row, l, k_prev, v_prev, L, T):
    B, S, _ = proj.shape
    H = H_FOX
    HD = 512 // H
    aug = jax.ShapeDtypeStruct((B, H, S, LANES), BF16)
    augt = jax.ShapeDtypeStruct((B, H, LANES, S), BF16)
    kv = jax.ShapeDtypeStruct((L, B, H, HD, S), F32)
    aug_spec = pl.BlockSpec((None, H, T, LANES), lambda b, t: (b, 0, t, 0))
    augt_spec = pl.BlockSpec((None, H, LANES, T), lambda b, t: (b, 0, 0, t))
    kv_spec = pl.BlockSpec((None, None, H, HD, T), lambda b, t: (l, b, 0, 0, t))
    in_specs = [pl.BlockSpec((None, T, 3 * H * HD), lambda b, t: (b, t, col)),
                pl.BlockSpec((None, T, LANES), lambda b, t: (b, t, 0)),
                pl.BlockSpec((1, LANES), lambda b, t: (0, 0)),
                pl.BlockSpec(memory_space=pl.ANY), pl.BlockSpec(memory_space=pl.ANY)]
    args = [proj, small, fb_row, k_prev, v_prev]
    aliases = {3: 3, 4: 4}
    kernel = lambda a, b, c, _k, _v, *rest: _foxprep_kernel(a, b, c, *rest)
    return pl.pallas_call(
        kernel,
        grid=(B, S // T),
        in_specs=in_specs,
        out_specs=[aug_spec, aug_spec, augt_spec, kv_spec, kv_spec,
                   pl.BlockSpec((None, T, LANES), lambda b, t: (b, t, 0))],
        out_shape=[aug, aug, augt, kv, kv, jax.ShapeDtypeStruct((B, S, LANES), F32)],
        scratch_shapes=[pltpu.VMEM((8, LANES), F32)],
        input_output_aliases=aliases,
        compiler_params=_cparams(("parallel", "arbitrary")),
        name="foxprep",
    )(*args)


FLASH_QS = 256
FLASH_HEADS = 4
FLASH_AHEAD1 = 6
FLASH_S_SLOTS = 16
FLASH_P_SLOTS = 16


def _flash_kernel(qi_ref, kj_ref, qa_ref, ka_ref, vt_ref, z_ref, o_ref,
                  s_scr, p_scr, m_scr, acc_scr):
    TQ = qa_ref.shape[1]
    TK = ka_ref.shape[1]
    QS = FLASH_QS
    HD = LANES // 2
    nqs = TQ // QS
    HPS = qa_ref.shape[0]
    st = pl.program_id(2)
    i = qi_ref[st]
    j = kj_ref[st]
    nt = (((1,), (1,)), ((), ()))

    @pl.when(j == 0)
    def _():
        m_scr[...] = jnp.full(m_scr.shape, -jnp.inf, F32)
        acc_scr[...] = jnp.zeros_like(acc_scr)

    def unit(n):
        return n // HPS, n % HPS

    def scores(n, nk, diag):
        qs, hh = unit(n)
        slot = n % s_scr.shape[0]
        q = qa_ref[hh, qs * QS:(qs + 1) * QS, :]
        s = lax.dot_general(ka_ref[hh, 0:nk, :], q, nt, preferred_element_type=F32)
        if diag:
            keep = _iota2((QS, QS), 0) <= _iota2((QS, QS), 1)
            last = jnp.where(keep, s[nk - QS:nk, :], -jnp.inf)
            s_scr[slot, nk - QS:nk, :] = last
            m_cur = jnp.max(last, axis=0, keepdims=True)
            if nk > QS:
                s_scr[slot, 0:nk - QS, :] = s[0:nk - QS, :]
                m_cur = jnp.maximum(m_cur, jnp.max(s[0:nk - QS, :], axis=0, keepdims=True))
            return m_cur
        s_scr[slot, 0:nk, :] = s
        return jnp.max(s, axis=0, keepdims=True)

    def softmax(n, nk, m_cur):
        qs, hh = unit(n)
        m_prev = m_scr[hh, qs, 0:1, :]
        m_new = jnp.maximum(m_prev, m_cur)
        p_scr[n % p_scr.shape[0], 0:nk, :] = jnp.exp2(
            s_scr[n % s_scr.shape[0], 0:nk, :] - m_new).astype(BF16)
        m_scr[hh, qs, 0:1, :] = m_new
        return jnp.exp2(m_prev - m_new)

    def pv(n, nk, alpha):
        qs, hh = unit(n)
        o = jnp.dot(vt_ref[hh, :, 0:nk], p_scr[n % p_scr.shape[0], 0:nk, :],
                    preferred_element_type=F32)
        acc_scr[hh, qs] = acc_scr[hh, qs] * alpha + o

    ahead = FLASH_AHEAD1 - 1
    nu = HPS * nqs

    @pl.when(j < i)
    def _():
        m_curs = [scores(n, TK, False) for n in range(ahead)]
        for n in range(nu):
            alpha = softmax(n, TK, m_curs[n])
            if n + ahead < nu:
                m_curs.append(scores(n + ahead, TK, False))
            pv(n, TK, alpha)

    @pl.when(j == i)
    def _():
        nks = [(n // HPS + 1) * QS for n in range(nu)]
        m_curs = [scores(n, nks[n], True) for n in range(ahead)]
        for n in range(nu):
            alpha = softmax(n, nks[n], m_curs[n])
            if n + ahead < nu:
                m_curs.append(scores(n + ahead, nks[n + ahead], True))
            pv(n, nks[n], alpha)
        lane = _iota2((QS, LANES), 1)
        for qs in range(nqs):
            for pr in range(HPS // 2):
                a0 = acc_scr[2 * pr, qs]
                a1 = acc_scr[2 * pr + 1, qs]
                o0 = (a0 / a0[HD:HD + 1, :]).T
                o1 = (a1 / a1[HD:HD + 1, :]).T
                o = jnp.where(lane < HD, o0, pltpu.roll(o1, HD, 1))
                rows, cols = slice(qs * QS, (qs + 1) * QS), slice(pr * LANES, (pr + 1) * LANES)
                z = z_ref[rows, cols].astype(F32)
                o_ref[rows, cols] = (o * _silu(z)).astype(o_ref.dtype)


def _flash(qa, ka, vt, proj, zcol, tq):
    B, H, S, _ = qa.shape
    n = S // tq
    steps = [(i, j) for i in range(n) for j in range(i + 1)]
    qi = jnp.asarray([s[0] for s in steps], jnp.int32)
    kj = jnp.asarray([s[1] for s in steps], jnp.int32)
    nqs = tq // FLASH_QS
    hps = FLASH_HEADS
    ow = hps * LANES // 2
    grid_spec = pltpu.PrefetchScalarGridSpec(
        num_scalar_prefetch=2,
        grid=(B, H // hps, len(steps)),
        in_specs=[pl.BlockSpec((None, hps, tq, LANES), lambda b, p, s, qi, kj: (b, p, qi[s], 0)),
                  pl.BlockSpec((None, hps, tq, LANES), lambda b, p, s, qi, kj: (b, p, kj[s], 0)),
                  pl.BlockSpec((None, hps, LANES, tq), lambda b, p, s, qi, kj: (b, p, 0, kj[s])),
                  pl.BlockSpec((None, tq, ow), lambda b, p, s, qi, kj: (b, qi[s], zcol // (ow // LANES) + p))],
        out_specs=pl.BlockSpec((None, tq, ow), lambda b, p, s, qi, kj: (b, qi[s], p)),
        scratch_shapes=[pltpu.VMEM((FLASH_S_SLOTS, tq, FLASH_QS), F32),
                        pltpu.VMEM((FLASH_P_SLOTS, tq, FLASH_QS), BF16),
                        pltpu.VMEM((hps, nqs, 8, FLASH_QS), F32),
                        pltpu.VMEM((hps, nqs, LANES, FLASH_QS), F32)])
    return pl.pallas_call(
        _flash_kernel,
        grid_spec=grid_spec,
        out_shape=jax.ShapeDtypeStruct((B, S, H * LANES // 2), BF16),
        compiler_params=_cparams(("parallel", "parallel", "arbitrary")),
        name="flash",
    )(qi, kj, qa, ka, vt, proj)


def _cumsum_kernel(x_ref, o_ref, *, group):
    R = x_ref.shape[0]
    x = x_ref[...]
    up = jnp.where(_iota2((LANES, LANES), 0) <= _iota2((LANES, LANES), 1), 1.0, 0.0).astype(BF16)
    w = _dot_sel_r(x, up)
    tot = jnp.broadcast_to(w[:, LANES - 1:LANES], (R, LANES))
    ri = _iota2((R, R), 0)
    ci = _iota2((R, R), 1)
    prev = jnp.where((ci < ri) & ((ci // group) == (ri // group)), 1.0, 0.0).astype(BF16)
    o_ref[...] = w + _dot_sel_l(prev, tot)


def _cumsum_rows(x, group):
    R = x.shape[0]
    rb = min(R, 256)
    return pl.pallas_call(
        functools.partial(_cumsum_kernel, group=group),
        grid=(R // rb,),
        in_specs=[pl.BlockSpec((rb, LANES), lambda r: (r, 0))],
        out_specs=pl.BlockSpec((rb, LANES), lambda r: (r, 0)),
        out_shape=jax.ShapeDtypeStruct((R, LANES), F32),
        compiler_params=_cparams(("parallel",)),
        name="cumsum",
    )(x)


def _foxsamp_kernel(q_ref, k_ref, v_ref, z_ref, sm_ref, fb_ref, tot_ref, fcc_ref, ck_ref, cv_ref,
                    o_ref, ko_ref, vo_ref, lf_ref):
    S = q_ref.shape[0]
    HD = ck_ref.shape[1]
    p = pl.program_id(1)
    logf = _logf_from_small(sm_ref[...], fb_ref[...])
    lf_ref[...] = logf
    ltri = jnp.where(_iota2((S, S), 0) >= _iota2((S, S), 1), 1.0, 0.0).astype(BF16)
    fc = _dot_sel_l(ltri, logf) + tot_ref[...]
    fct = jnp.concatenate([fc, jnp.zeros((LANES - S, LANES), F32)], axis=0).T
    lane = _iota2((S, LANES), 1)
    row = _iota2((LANES, LANES), 0)
    causal = _iota2((S, S), 0) >= _iota2((S, S), 1)
    scale = HD ** -0.5
    for hh in range(2):
        h = 2 * p + hh
        fq = jnp.sum(jnp.where(lane == h, fc, 0.0), axis=1, keepdims=True)
        fkn = jnp.sum(jnp.where(row == h, fct, 0.0), axis=0, keepdims=True)[:, :S]
        q = q_ref[:, hh * HD:(hh + 1) * HD].astype(F32) * scale
        kn = k_ref[:, hh * HD:(hh + 1) * HD].astype(F32)
        vn = v_ref[:, hh * HD:(hh + 1) * HD].astype(F32)
        ko_ref[hh] = kn
        vo_ref[hh] = vn
        sc = _dot(q, ck_ref[hh]) + fq - fcc_ref[hh]
        sn = jnp.where(causal, _dot_nt(q, kn) + fq - fkn, -jnp.inf)
        m = jnp.maximum(jnp.max(sc, axis=1, keepdims=True), jnp.max(sn, axis=1, keepdims=True))
        pc = jnp.exp(sc - m)
        pn = jnp.exp(sn - m)
        l = jnp.sum(pc, axis=1, keepdims=True) + jnp.sum(pn, axis=1, keepdims=True)
        o = (_dot_nt(pc, cv_ref[hh]) + _dot(pn, vn)) / l
        o_ref[:, hh * HD:(hh + 1) * HD] = o * _silu(z_ref[:, hh * HD:(hh + 1) * HD].astype(F32))


def _foxsamp(proj, small, cols, fb_row, tot, fcc, cache_k, cache_v, l):
    B, S, _ = proj.shape
    _, _, H, HD, P = cache_k.shape
    cq, ck, cv, cz = cols
    pair = lambda c: pl.BlockSpec((None, S, LANES), lambda b, p: (b, 0, c + p))
    kv_out = jax.ShapeDtypeStruct((B, H, S, HD), F32)
    return pl.pallas_call(
        _foxsamp_kernel,
        grid=(B, H // 2),
        in_specs=[pair(cq), pair(ck), pair(cv), pair(cz),
                  pl.BlockSpec((None, S, LANES), lambda b, p: (b, 0, 0)),
                  pl.BlockSpec((1, LANES), lambda b, p: (0, 0)),
                  pl.BlockSpec((None, 1, LANES), lambda b, p: (b, 0, 0)),
                  pl.BlockSpec((None, 2, 1, P), lambda b, p: (b, p, 0, 0)),
                  pl.BlockSpec((None, None, 2, HD, P), lambda b, p: (l, b, p, 0, 0)),
                  pl.BlockSpec((None, None, 2, HD, P), lambda b, p: (l, b, p, 0, 0))],
        out_specs=[pl.BlockSpec((None, S, LANES), lambda b, p: (b, 0, p)),
                   pl.BlockSpec((None, 2, S, HD), lambda b, p: (b, p, 0, 0)),
                   pl.BlockSpec((None, 2, S, HD), lambda b, p: (b, p, 0, 0)),
                   pl.BlockSpec((None, S, LANES), lambda b, p: (b, 0, 0))],
        out_shape=[jax.ShapeDtypeStruct((B, S, H * HD), F32), kv_out, kv_out,
                   jax.ShapeDtypeStruct((B, S, LANES), F32)],
        compiler_params=_cparams(("parallel", "arbitrary")),
        name="foxsamp",
    )(proj, proj, proj, proj, small, fb_row, tot, fcc, cache_k, cache_v)


COL_MG = 0
COL_GQKV = 6
COL_FQKV = 9
COL_RQK = 12
COL_RV = 13
COL_RZ = 14
COL_FZ = 15
COL_GZ = 16


def _prep_w_in(w_in, D):
    BW = D // 2
    hk = BW // 2
    splits = (hk, hk, BW, BW, BW, BW, BW, H_FOX, BW, BW, BW, BW, H_GDN, H_GDN, BW, N_BRANCH * D)
    offs = np.cumsum((0,) + splits)
    seg = lambda i: w_in[..., offs[i]:offs[i + 1]]
    rq, rk, rv, rz, fq, fk, fv, ff, fz, gq, gk, gv, ga, gb, gz, mg = [seg(i) for i in range(16)]
    main = jnp.concatenate([mg, gq, gk, gv, fq, fk, fv, rq, rk, rv, rz, fz, gz], axis=-1)
    pad = jnp.zeros(w_in.shape[:-1] + (LANES - H_FOX - 2 * H_GDN,), w_in.dtype)
    small = jnp.concatenate([ff, ga, gb, pad], axis=-1)
    return main.astype(BF16), small.astype(BF16)


def _lane_row(vals, off):
    L, n = vals.shape
    return jnp.zeros((L, 1, LANES), F32).at[:, 0, off:off + n].set(vals)


def kernel(x_prompt, x_sample, c_prompt, c_sample, cache_fox_k, cache_fox_v, cache_fox_logf,
           state_ret, state_gdn, state_gdn_conv, norm_w, ada_w, ada_b, w_in, fox_f_bias,
           gdn_a_log, gdn_dt_bias, gdn_conv_w, ret_norm_w, gdn_norm_w, w_branch, w_out,
           final_norm_w):
    B, S, D = x_prompt.shape
    BS, SS, _ = x_sample.shape
    L = ada_w.shape[0]
    P = cache_fox_k.shape[3]
    DK_RET, DV_RET = state_ret.shape[-2:]
    DK_GDN, DV_GDN = state_gdn.shape[-2:]

    rows = -(-(B + BS) // 8) * 8
    c_all = jnp.concatenate([c_prompt, c_sample, jnp.zeros((rows - B - BS, D), F32)], axis=0)
    mod = _adaln(c_all, ada_w, ada_b)
    mod_p = mod[:, :, :B].reshape(L, 3, B, 1, D)
    mod_s = mod[:, :, B:B + BS].reshape(L, 3, BS, 1, D)

    w_main, w_small = _prep_w_in(w_in, D)
    w_branch_b = w_branch.astype(BF16)
    w_out_b = w_out.astype(BF16)
    norm_w3 = norm_w.reshape(L, 1, D)
    final_w = final_norm_w.reshape(1, D)
    ret_nw = ret_norm_w.reshape(L, H_RET, 1, DV_RET)
    gdn_nw = gdn_norm_w.reshape(L, 1, DV_GDN)
    fb_rows = _lane_row(fox_f_bias, 0)
    a_rows = _lane_row(gdn_a_log, H_FOX)
    dt_rows = _lane_row(gdn_dt_bias, H_FOX)

    tm_p = min(S, 2048)
    tmm_p = min(S, 512)
    t_ret = min(S, 256)
    t_gdn = min(S, 512)
    t_prep = min(S, 256)
    t_flash = min(S, 1024)

    tabs_p = _ret_tables(t_ret, 0, S, DK_RET)
    tabs_s = _ret_tables(SS, P, SS, DK_RET)

    fcc = _cumsum_rows(cache_fox_logf.reshape(-1, LANES), P // LANES).reshape(L, BS, H_FOX, 1, P)
    tot = jnp.zeros((L, BS, 1, LANES), F32).at[:, :, 0, :H_FOX].set(fcc[:, :, :, 0, P - 1])

    zeros_ret = jnp.zeros((B, H_RET, DK_RET, DV_RET), F32)
    zeros_gdn = jnp.zeros((B, H_GDN, DK_GDN, DV_GDN), F32)
    zeros_conv = jnp.zeros((B, CONV_W - 1, gdn_conv_w.shape[-1]), F32)

    xp, xs = x_prompt, x_sample
    cache_kt = jnp.swapaxes(cache_fox_k, 3, 4)
    cache_vt = jnp.swapaxes(cache_fox_v, 3, 4)
    pk = jnp.zeros((L, B, H_FOX, cache_fox_k.shape[-1], S), F32)
    pv = jnp.zeros_like(pk)
    p_lf, p_ret, p_gdn, p_conv = [], [], [], []
    s_k, s_v, s_lf, s_ret, s_gdn, s_conv = [], [], [], [], [], []
    for l in range(L):
        final = l == L - 1
        proj, small = _inproj(xp, mod_p, l, norm_w3, w_main, w_small, 1, tm_p)
        o_ret, st = _ret(proj, (COL_RQK, COL_RV, COL_RZ), tabs_p, ret_nw[l], zeros_ret, t_ret, BF16)
        p_ret.append(st)
        o_gdn, st, cv = _gdn(proj, small, (COL_GQKV // 3, COL_GZ), gdn_conv_w[l], zeros_conv,
                             a_rows[l], dt_rows[l], gdn_nw[l], zeros_gdn, t_gdn, BF16)
        p_gdn.append(st)
        p_conv.append(cv)
        qa, ka, va, pk, pv, lf = _foxprep(proj, small, COL_FQKV // 3, fb_rows[l], l, pk, pv, L, t_prep)
        p_lf.append(lf)
        o_fox = _flash(qa, ka, va, proj, COL_FZ * 4, t_flash)
        xp = _merge(xp, mod_p, l, o_ret, o_fox, o_gdn, proj, w_branch_b, w_out_b, final_w,
                    1, tmm_p, final)
        proj, small = _inproj(xs, mod_s, l, norm_w3, w_main, w_small, BS, SS)
        o_ret, st = _ret(proj, (COL_RQK, COL_RV, COL_RZ), tabs_s, ret_nw[l], state_ret[l], SS, F32)
        s_ret.append(st)
        o_gdn, st, cv = _gdn(proj, small, (COL_GQKV // 3, COL_GZ), gdn_conv_w[l], state_gdn_conv[l],
                             a_rows[l], dt_rows[l], gdn_nw[l], state_gdn[l], SS, F32)
        s_gdn.append(st)
        s_conv.append(cv)
        o_fox, kk, vv, lf = _foxsamp(proj, small,
                                     (COL_FQKV * 4, (COL_FQKV + 1) * 4, (COL_FQKV + 2) * 4, COL_FZ * 4),
                                     fb_rows[l], tot[l], fcc[l], cache_kt, cache_vt, l)
        s_k.append(kk)
        s_v.append(vv)
        s_lf.append(lf)
        xs = _merge(xs, mod_s, l, o_ret, o_fox, o_gdn, proj, w_branch_b, w_out_b, final_w,
                    BS, SS, final)

    logf_out = lambda lfs: jnp.stack(lfs)[..., :H_FOX].transpose(0, 1, 3, 2)
    return (xp, xs,
            jnp.swapaxes(pk, 3, 4), jnp.swapaxes(pv, 3, 4), logf_out(p_lf), jnp.stack(p_ret), jnp.stack(p_gdn), jnp.stack(p_conv),
            jnp.stack(s_k), jnp.stack(s_v), logf_out(s_lf), jnp.stack(s_ret), jnp.stack(s_gdn),
            jnp.stack(s_conv))
```

```python
import functools

import numpy as np
import jax
import jax.numpy as jnp
from jax import lax
from jax.experimental import pallas as pl
from jax.experimental.pallas import tpu as pltpu

F32 = jnp.float32
BF16 = jnp.bfloat16

N_BRANCH = 3
H_RET = 4
H_FOX = 8
H_GDN = 4
CONV_W = 4
CHUNK = 64
NORM_EPS = 1e-6
ROPE_BASE = 10000.0
LOG2E = 1.4426950408889634
LANES = 128
GDN_CHUNK = 128
VMEM_LIMIT = 56 * 1024 * 1024


def _cparams(sem):
    return pltpu.CompilerParams(dimension_semantics=sem, vmem_limit_bytes=VMEM_LIMIT)


def _sigmoid(x):
    return 0.5 * jnp.tanh(0.5 * x) + 0.5


def _silu(x):
    return x * _sigmoid(x)


def _softplus(x):
    return jnp.maximum(x, 0.0) + jnp.log(1.0 + jnp.exp(-jnp.abs(x)))


def _dot(a, b):
    return jnp.dot(a.astype(BF16), b.astype(BF16), preferred_element_type=F32)


def _dot_nt(a, b):
    return lax.dot_general(a.astype(BF16), b.astype(BF16), (((1,), (1,)), ((), ())),
                           preferred_element_type=F32)


def _split3(x):
    hi = x.astype(BF16)
    r = x - hi.astype(F32)
    mid = r.astype(BF16)
    lo = (r - mid.astype(F32)).astype(BF16)
    return hi, mid, lo


def _dot_sel_l(m, x):
    hi, mid, lo = _split3(x)
    d = lambda y: jnp.dot(m, y, preferred_element_type=F32)
    return d(hi) + (d(mid) + d(lo))


def _dot_sel_r(x, m):
    hi, mid, lo = _split3(x)
    d = lambda y: jnp.dot(y, m, preferred_element_type=F32)
    return d(hi) + (d(mid) + d(lo))


def _iota2(shape, dim):
    return lax.broadcasted_iota(jnp.int32, shape, dim)


def _adaln_kernel(c_ref, w_ref, b_ref, o_ref):
    s = _silu(c_ref[...])
    o_ref[...] = _dot(s, w_ref[...]) + b_ref[...]


def _adaln(c_all, ada_w, ada_b):
    L, D, _ = ada_w.shape
    R = c_all.shape[0]
    return pl.pallas_call(
        _adaln_kernel,
        grid=(L, 3),
        in_specs=[pl.BlockSpec((R, D), lambda l, k: (0, 0)),
                  pl.BlockSpec((None, D, D), lambda l, k: (l, 0, k)),
                  pl.BlockSpec((None, None, 1, D), lambda l, k: (l, k, 0, 0))],
        out_specs=pl.BlockSpec((None, None, R, D), lambda l, k: (l, k, 0, 0)),
        out_shape=jax.ShapeDtypeStruct((L, 3, R, D), F32),
        compiler_params=_cparams(("parallel", "parallel")),
        name="adaln",
    )(c_all, ada_w, ada_b.reshape(L, 3, 1, D))


PROJ_TN = 512
PROJ_DTYPE = BF16


def _inproj_kernel(x_ref, nw_ref, sc_ref, sh_ref, w_ref, ws_ref, o_ref, os_ref, h_ref):
    bb, tm, D = x_ref.shape

    @pl.when(pl.program_id(2) == 0)
    def _():
        x = x_ref[...]
        ms = jnp.mean(x * x, axis=-1, keepdims=True)
        xn = (x * lax.rsqrt(ms + NORM_EPS)) * nw_ref[...]
        h = xn * (1.0 + sc_ref[...]) + sh_ref[...]
        hb = h.astype(BF16).reshape(bb * tm, D)
        h_ref[...] = hb
        os_ref[...] = jnp.dot(hb, ws_ref[...], preferred_element_type=F32).reshape(bb, tm, LANES)

    o_ref[...] = jnp.dot(h_ref[...], w_ref[...],
                         preferred_element_type=F32).reshape(bb, tm, PROJ_TN).astype(o_ref.dtype)


def _inproj(x, mod, l, norm_w, w_main, w_small, bb, tm):
    B, S, D = x.shape
    NP = w_main.shape[-1]
    nj = NP // PROJ_TN
    return pl.pallas_call(
        _inproj_kernel,
        grid=(B // bb, S // tm, nj),
        in_specs=[pl.BlockSpec((bb, tm, D), lambda b, i, j: (b, i, 0)),
                  pl.BlockSpec((None, 1, D), lambda b, i, j: (l, 0, 0)),
                  pl.BlockSpec((None, None, bb, 1, D), lambda b, i, j: (l, 1, b, 0, 0)),
                  pl.BlockSpec((None, None, bb, 1, D), lambda b, i, j: (l, 0, b, 0, 0)),
                  pl.BlockSpec((None, D, PROJ_TN), lambda b, i, j: (l, 0, j)),
                  pl.BlockSpec((None, D, LANES), lambda b, i, j: (l, 0, 0))],
        out_specs=[pl.BlockSpec((bb, tm, PROJ_TN), lambda b, i, j: (b, i, j)),
                   pl.BlockSpec((bb, tm, LANES), lambda b, i, j: (b, i, 0))],
        out_shape=[jax.ShapeDtypeStruct((B, S, NP), PROJ_DTYPE),
                   jax.ShapeDtypeStruct((B, S, LANES), F32)],
        scratch_shapes=[pltpu.VMEM((bb * tm, D), BF16)],
        compiler_params=_cparams(("parallel", "parallel", "arbitrary")),
        name="inproj",
    )(x, norm_w, mod, mod, w_main, w_small)


def _merge_kernel(x_ref, g_ref, o1_ref, o2_ref, o3_ref, mg_ref, wb_ref, wo_ref, fw_ref, y_ref,
                  *, final):
    bb, tm, D = x_ref.shape
    M = bb * tm
    acc = None
    for n, o_ref in enumerate((o1_ref, o2_ref, o3_ref)):
        br = jnp.dot(o_ref[...].reshape(M, o_ref.shape[-1]).astype(BF16), wb_ref[n],
                     preferred_element_type=F32)
        gate = _sigmoid(mg_ref[:, :, n * D:(n + 1) * D].astype(F32).reshape(M, D))
        term = br * gate
        acc = term if acc is None else acc + term
    out = jnp.dot(acc.astype(BF16), wo_ref[...], preferred_element_type=F32)
    xn = x_ref[...] + g_ref[...] * out.reshape(bb, tm, D)
    if final:
        ms = jnp.mean(xn * xn, axis=-1, keepdims=True)
        xn = (xn * lax.rsqrt(ms + NORM_EPS)) * fw_ref[...]
    y_ref[...] = xn


def _merge(x, mod, l, o_ret, o_fox, o_gdn, proj, w_branch, w_out, final_w, bb, tm, final):
    B, S, D = x.shape
    W = o_ret.shape[-1]
    return pl.pallas_call(
        functools.partial(_merge_kernel, final=final),
        grid=(B // bb, S // tm),
        in_specs=[pl.BlockSpec((bb, tm, D), lambda b, i: (b, i, 0)),
                  pl.BlockSpec((None, None, bb, 1, D), lambda b, i: (l, 2, b, 0, 0)),
                  pl.BlockSpec((bb, tm, W), lambda b, i: (b, i, 0)),
                  pl.BlockSpec((bb, tm, W), lambda b, i: (b, i, 0)),
                  pl.BlockSpec((bb, tm, W), lambda b, i: (b, i, 0)),
                  pl.BlockSpec((bb, tm, N_BRANCH * D), lambda b, i: (b, i, 0)),
                  pl.BlockSpec((None, N_BRANCH, W, D), lambda b, i: (l, 0, 0, 0)),
                  pl.BlockSpec((None, D, D), lambda b, i: (l, 0, 0)),
                  pl.BlockSpec((1, D), lambda b, i: (0, 0))],
        out_specs=pl.BlockSpec((bb, tm, D), lambda b, i: (b, i, 0)),
        out_shape=jax.ShapeDtypeStruct((B, S, D), F32),
        compiler_params=_cparams(("parallel", "parallel")),
        name="merge",
    )(x, mod, o_ret, o_fox, o_gdn, proj, w_branch, w_out, final_w)


def _ret_kernel(qk_ref, v_ref, z_ref, cos_ref, sin_ref, m_ref, dq_ref, dk_ref, dc_ref, nw_ref,
                s0_ref, o_ref, so_ref, s_ref):
    T = qk_ref.shape[0]
    H = s_ref.shape[0]
    DK = s0_ref.shape[1]
    t = pl.program_id(1)

    @pl.when(t == 0)
    def _():
        zero = jnp.zeros((DK, LANES), F32)
        for h in range(H):
            s0 = s0_ref[h]
            s_ref[h] = jnp.concatenate([s0, zero] if h % 2 == 0 else [zero, s0], axis=0)

    cos = cos_ref[...]
    sin = sin_ref[...]
    lane = _iota2((T, LANES), 1)
    first = (lane % DK) < (DK // 2)
    low = lane < DK

    def rot(x):
        sw = jnp.where(first, pltpu.roll(x, LANES - DK // 2, 1), pltpu.roll(x, DK // 2, 1))
        return x * cos + sw * sin

    nkq = H * DK
    hs = range(H)
    qcs = [rot(qk_ref[:, p * LANES:(p + 1) * LANES].astype(F32)) for p in range(H // 2)]
    kcs = [rot(qk_ref[:, nkq + p * LANES:nkq + (p + 1) * LANES].astype(F32)) * (DK ** -0.5)
           for p in range(H // 2)]
    kcbs = [kc.astype(BF16) for kc in kcs]
    sels = [low if h % 2 == 0 else jnp.logical_not(low) for h in hs]
    qms = [jnp.where(sels[h], qcs[h // 2], 0.0) for h in hs]
    vhs = [v_ref[:, h * LANES:(h + 1) * LANES].astype(BF16) for h in hs]
    ss = [s_ref[h] for h in hs]
    ams = [_dot_nt(qms[h], kcbs[h // 2]) * m_ref[h] for h in hs]
    kts = [(jnp.where(sels[h], kcs[h // 2], 0.0) * dk_ref[h]).T for h in hs]
    os_ = [_dot(ams[h], vhs[h]) + _dot(qms[h] * dq_ref[h], ss[h]) for h in hs]
    for h in hs:
        s_ref[h] = ss[h] * dc_ref[h] + _dot(kts[h], vhs[h])
    for h in hs:
        o = os_[h]
        ms = jnp.mean(o * o, axis=-1, keepdims=True)
        on = (o * lax.rsqrt(ms + NORM_EPS)) * nw_ref[h]
        z = z_ref[:, h * LANES:(h + 1) * LANES].astype(F32)
        o_ref[:, h * LANES:(h + 1) * LANES] = (on * _silu(z)).astype(o_ref.dtype)

    @pl.when(t == pl.num_programs(1) - 1)
    def _():
        for h in range(H):
            so_ref[h] = s_ref[h][(h % 2) * DK:(h % 2 + 1) * DK, :]


def _ret_tables(T, pos0, S, DK):
    half = DK // 2
    inv_freq = ROPE_BASE ** (-jnp.arange(half, dtype=F32) / half)
    pos = pos0 + jnp.arange(S)
    ang = pos.astype(F32)[:, None] * inv_freq[None, :]
    cos, sin = jnp.cos(ang), jnp.sin(ang)
    reps = LANES // DK
    cos_t = jnp.tile(jnp.concatenate([cos, cos], axis=1), (1, reps))
    sin_t = jnp.tile(jnp.concatenate([-sin, sin], axis=1), (1, reps))
    log_g = jnp.log1p(-jnp.exp2(-5.0 - jnp.arange(H_RET, dtype=F32)))
    i = jnp.arange(T)
    dist = jnp.abs(i[:, None] - i[None, :]).astype(F32)
    allowed = (i[None, :] // CHUNK) <= (i[:, None] // CHUNK)
    m = jnp.where(allowed[None], jnp.exp(dist[None] * log_g[:, None, None]), 0.0)
    fi = i.astype(F32)
    dq = jnp.exp((fi[None, :] + 1.0) * log_g[:, None])
    dk = jnp.exp((T - 1.0 - fi)[None, :] * log_g[:, None])
    dc = jnp.exp(T * log_g)
    bc = lambda a: jnp.broadcast_to(a[..., None], a.shape + (LANES,))
    return cos_t, sin_t, m, bc(dq), bc(dk), bc(dc[:, None])


def _ret(proj, cols, tabs, ret_nw, s0, T, out_dtype):
    B, S, _ = proj.shape
    H, DK, DV = s0.shape[1:]
    W = H * DV
    cqk, cv, cz = cols
    cos_t, sin_t, m, dq, dk, dc = tabs
    return pl.pallas_call(
        _ret_kernel,
        grid=(B, S // T),
        in_specs=[pl.BlockSpec((None, T, W), lambda b, t: (b, t, cqk)),
                  pl.BlockSpec((None, T, W), lambda b, t: (b, t, cv)),
                  pl.BlockSpec((None, T, W), lambda b, t: (b, t, cz)),
                  pl.BlockSpec((T, LANES), lambda b, t: (t, 0)),
                  pl.BlockSpec((T, LANES), lambda b, t: (t, 0)),
                  pl.BlockSpec((H, T, T), lambda b, t: (0, 0, 0)),
                  pl.BlockSpec((H, T, LANES), lambda b, t: (0, 0, 0)),
                  pl.BlockSpec((H, T, LANES), lambda b, t: (0, 0, 0)),
                  pl.BlockSpec((H, 1, LANES), lambda b, t: (0, 0, 0)),
                  pl.BlockSpec((H, 1, DV), lambda b, t: (0, 0, 0)),
                  pl.BlockSpec((None, H, DK, DV), lambda b, t: (b, 0, 0, 0))],
        out_specs=[pl.BlockSpec((None, T, W), lambda b, t: (b, t, 0)),
                   pl.BlockSpec((None, H, DK, DV), lambda b, t: (b, 0, 0, 0))],
        out_shape=[jax.ShapeDtypeStruct((B, S, W), out_dtype),
                   jax.ShapeDtypeStruct((B, H, DK, DV), F32)],
        scratch_shapes=[pltpu.VMEM((H, LANES, DV), F32)],
        compiler_params=_cparams(("parallel", "arbitrary")),
        name="ret",
    )(proj, proj, proj, cos_t, sin_t, m, dq, dk, dc, ret_nw, s0)


INV_BASE = 8


def _unit_lower_inverses(mats, eye, ri, ci):
    n = mats[0].shape[0]
    b = INV_BASE
    d = lambda x, y: jnp.dot(x, y, preferred_element_type=F32)
    same = (ri // b) == (ci // b)
    ads = [jnp.where(same, a, 0.0) for a in mats]
    xs = [eye - ad for ad in ads]
    ps = [ad.astype(BF16) for ad in ads]
    k = 2
    while k < b:
        ps = [d(p, p).astype(BF16) for p in ps]
        xs = [x + d(x.astype(BF16), p) for x, p in zip(xs, ps)]
        k *= 2
    while b < n:
        off = ((ri // (2 * b)) == (ci // (2 * b))) & ((ri // b) != (ci // b))
        xbs = [x.astype(BF16) for x in xs]
        ts = [d(jnp.where(off, a, 0.0).astype(BF16), xb) for a, xb in zip(mats, xbs)]
        xs = [x - d(xb, t.astype(BF16)) for x, xb, t in zip(xs, xbs, ts)]
        b *= 2
    return xs


def _gdn_kernel(u_ref, z_ref, sm_ref, cw_ref, cb_ref, al_ref, dt_ref, nw_ref, s0_ref,
                o_ref, so_ref, co_ref, ext_ref, s_ref):
    T, C3 = u_ref.shape
    H = s_ref.shape[0]
    DK = s_ref.shape[1]
    W = H * DK
    C = GDN_CHUNK
    TP = -(-T // C) * C
    t = pl.program_id(1)
    last = pl.num_programs(1) - 1
    base = 8
    lo = base - (CONV_W - 1)

    @pl.when(t == 0)
    def _():
        s_ref[...] = s0_ref[...]
        ext_ref[lo:base, :] = cb_ref[...]

    ext_ref[base:base + T, :] = u_ref[...].astype(F32)
    y = ext_ref[lo:lo + T, :] * cw_ref[0:1, :]
    for j in range(1, CONV_W):
        y = y + ext_ref[lo + j:lo + j + T, :] * cw_ref[j:j + 1, :]
    tail = ext_ref[lo + T:base + T, :]
    ext_ref[lo:base, :] = tail

    @pl.when(t == last)
    def _():
        co_ref[...] = tail

    qkv = _silu(y)
    sm = sm_ref[...]
    g_all = -jnp.exp(al_ref[...]) * _softplus(sm + dt_ref[...])
    b_all = _sigmoid(sm)
    z_all = z_ref[...].astype(F32)
    if TP > T:
        pad = lambda a: jnp.concatenate([a, jnp.zeros((TP - T, a.shape[1]), F32)], axis=0)
        qkv, g_all, b_all, z_all = pad(qkv), pad(g_all), pad(b_all), pad(z_all)

    ri = _iota2((C, C), 0)
    ci = _iota2((C, C), 1)
    tri = ri >= ci
    strict = ri > ci
    ltri = jnp.where(tri, 1.0, 0.0).astype(BF16)
    eye = jnp.where(ri == ci, 1.0, 0.0)

    nc = TP // C
    items = [(c, h) for c in range(nc) for h in range(H)]
    gcums = [_dot_sel_l(ltri, g_all[c * C:(c + 1) * C, :]) for c in range(nc)]

    qs, ks, vs, gcs, bcs, decs, kbs = [], [], [], [], [], [], []
    for c, h in items:
        r0 = c * C
        q = qkv[r0:r0 + C, h * DK:(h + 1) * DK]
        k = qkv[r0:r0 + C, W + h * DK:W + (h + 1) * DK]
        qs.append(q * lax.rsqrt(jnp.sum(q * q, axis=-1, keepdims=True) + NORM_EPS) * (DK ** -0.5))
        k = k * lax.rsqrt(jnp.sum(k * k, axis=-1, keepdims=True) + NORM_EPS)
        ks.append(k)
        kbs.append(k.astype(BF16))
        vs.append(qkv[r0:r0 + C, 2 * W + h * DK:2 * W + (h + 1) * DK])
        gc = gcums[c][:, 8 + h:9 + h]
        gcs.append(gc)
        bcs.append(b_all[r0:r0 + C, 12 + h:13 + h])
        gm = jnp.broadcast_to(gc, (C, C))
        decs.append(jnp.exp(jnp.where(tri, gm - gm.T, -jnp.inf)))
    kks = [_dot_nt(kb, kb) for kb in kbs]
    amats = [jnp.where(strict, kk * dec, 0.0) * bc for kk, dec, bc in zip(kks, decs, bcs)]
    tinvs = _unit_lower_inverses(amats, eye, ri, ci)
    egs = [jnp.exp(gc) for gc in gcs]
    rhss = [jnp.concatenate([k * (bc * eg), v * bc], axis=1)
            for k, v, bc, eg in zip(ks, vs, bcs, egs)]
    wus = [_dot(tinv, rhs) for tinv, rhs in zip(tinvs, rhss)]
    qks = [(_dot_nt(q, kb) * dec).astype(BF16) for q, kb, dec in zip(qs, kbs, decs)]
    qes = [(q * eg).astype(BF16) for q, eg in zip(qs, egs)]
    glasts = [gcums[c][C - 1:C, 8 + h:9 + h] for c, h in items]
    kdts = [(k * jnp.exp(gl - gc)).T.astype(BF16) for k, gl, gc in zip(ks, glasts, gcs)]
    sdec = [jnp.exp(gl) for gl in glasts]

    for c in range(nc):
        idx = [c * H + h for h in range(H)]
        ss = [s_ref[h] for h in range(H)]
        sbs = [s.astype(BF16) for s in ss]
        us = [wus[i][:, DK:] - _dot(wus[i][:, :DK], sb) for i, sb in zip(idx, sbs)]
        ubs = [u.astype(BF16) for u in us]
        os_ = [jnp.dot(qes[i], sb, preferred_element_type=F32)
               + jnp.dot(qks[i], ub, preferred_element_type=F32)
               for i, sb, ub in zip(idx, sbs, ubs)]
        for h, i in enumerate(idx):
            s_ref[h] = ss[h] * sdec[i] + jnp.dot(kdts[i], ubs[h], preferred_element_type=F32)
        for h, o in enumerate(os_):
            ms = jnp.mean(o * o, axis=-1, keepdims=True)
            on = (o * lax.rsqrt(ms + NORM_EPS)) * nw_ref[...]
            z = z_all[c * C:(c + 1) * C, h * DK:(h + 1) * DK]
            res = (on * _silu(z)).astype(o_ref.dtype)
            if TP > T:
                o_ref[:, h * DK:(h + 1) * DK] = res[:T, :]
            else:
                o_ref[c * C:(c + 1) * C, h * DK:(h + 1) * DK] = res

    @pl.when(t == last)
    def _():
        so_ref[...] = s_ref[...]


def _gdn(proj, small, cols, conv_w, conv_buf, a_row, dt_row, gdn_nw, s0, T, out_dtype):
    B, S, _ = proj.shape
    H, DK, DV = s0.shape[1:]
    W = H * DK
    C3 = conv_w.shape[-1]
    cu, cz = cols
    return pl.pallas_call(
        _gdn_kernel,
        grid=(B, S // T),
        in_specs=[pl.BlockSpec((None, T, C3), lambda b, t: (b, t, cu)),
                  pl.BlockSpec((None, T, W), lambda b, t: (b, t, cz)),
                  pl.BlockSpec((None, T, LANES), lambda b, t: (b, t, 0)),
                  pl.BlockSpec((CONV_W, C3), lambda b, t: (0, 0)),
                  pl.BlockSpec((None, CONV_W - 1, C3), lambda b, t: (b, 0, 0)),
                  pl.BlockSpec((1, LANES), lambda b, t: (0, 0)),
                  pl.BlockSpec((1, LANES), lambda b, t: (0, 0)),
                  pl.BlockSpec((1, DV), lambda b, t: (0, 0)),
                  pl.BlockSpec((None, H, DK, DV), lambda b, t: (b, 0, 0, 0))],
        out_specs=[pl.BlockSpec((None, T, W), lambda b, t: (b, t, 0)),
                   pl.BlockSpec((None, H, DK, DV), lambda b, t: (b, 0, 0, 0)),
                   pl.BlockSpec((None, CONV_W - 1, C3), lambda b, t: (b, 0, 0))],
        out_shape=[jax.ShapeDtypeStruct((B, S, W), out_dtype),
                   jax.ShapeDtypeStruct((B, H, DK, DV), F32),
                   jax.ShapeDtypeStruct((B, CONV_W - 1, C3), F32)],
        scratch_shapes=[pltpu.VMEM((T + 8, C3), F32),
                        pltpu.VMEM((H, DK, DV), F32)],
        compiler_params=_cparams(("parallel", "arbitrary")),
        name="gdn",
    )(proj, proj, small, conv_w, conv_buf, a_row, dt_row, gdn_nw, s0)


def _logf_from_small(sm, fb):
    lane = _iota2(sm.shape, 1)
    return jnp.where(lane < H_FOX, -_softplus(-(sm + fb)), 0.0)


def _foxprep_kernel(qkv_ref, sm_ref, fb_ref, qa_ref, ka_ref, vt_ref, ko_ref, vo_ref, lf_ref,
                    carry_ref):
    T = sm_ref.shape[0]
    H = qa_ref.shape[0]
    HD = ko_ref.shape[1]
    W = H * HD
    t = pl.program_id(1)

    @pl.when(t == 0)
    def _():
        carry_ref[...] = jnp.zeros_like(carry_ref)

    logf = _logf_from_small(sm_ref[...], fb_ref[...])
    lf_ref[...] = logf
    ltri = jnp.where(_iota2((T, T), 0) >= _iota2((T, T), 1), 1.0, 0.0).astype(BF16)
    fc = _dot_sel_l(ltri, logf) + carry_ref[0:1, :]
    carry_ref[0:1, :] = fc[T - 1:T, :]
    f_hi, f_mid, f_lo = [p.astype(F32) for p in _split3(fc * LOG2E)]

    lane = _iota2((T, LANES), 1)
    low = lane < HD
    one_q = jnp.where((lane >= HD + 3) & (lane < HD + 6), 1.0, 0.0)
    one_k = jnp.where((lane >= HD) & (lane < HD + 3), 1.0, 0.0)
    ones_rows = jnp.where(_iota2((FLASH_VROWS - HD, T), 0) == 0, 1.0, 0.0)
    scale = (HD ** -0.5) * LOG2E
    for p in range(H // 2):
        qc = qkv_ref[:, p * LANES:(p + 1) * LANES].astype(F32) * scale
        kc = qkv_ref[:, W + p * LANES:W + (p + 1) * LANES].astype(F32)
        kct = kc.T
        vct = qkv_ref[:, 2 * W + p * LANES:2 * W + (p + 1) * LANES].astype(F32).T
        for half in range(2):
            h = 2 * p + half
            if half == 1:
                qh, kh = [pltpu.roll(a, HD, 1) for a in (qc, kc)]
            else:
                qh, kh = qc, kc
            vht = vct[half * HD:(half + 1) * HD, :]
            c_hi, c_mid, c_lo = [a[:, h:h + 1] for a in (f_hi, f_mid, f_lo)]
            ex_q = jnp.where(lane == HD, c_hi,
                             jnp.where(lane == HD + 1, c_mid,
                                       jnp.where(lane == HD + 2, c_lo, one_q)))
            ex_k = jnp.where(lane == HD + 3, -c_hi,
                             jnp.where(lane == HD + 4, -c_mid,
                                       jnp.where(lane == HD + 5, -c_lo, one_k)))
            qa_ref[h] = jnp.where(low, qh, ex_q).astype(BF16)
            ka_ref[h] = jnp.where(low, kh, ex_k).astype(BF16)
            vt_ref[h] = jnp.concatenate([vht, ones_rows], axis=0).astype(BF16)
            ko_ref[h] = kct[half * HD:(half + 1) * HD, :]
            vo_ref[h] = vht


def _foxprep(proj, small, col, fb_row, l, k_prev, v_prev, L, T):
    B, S, _ = proj.shape
    H = H_FOX
    HD = 512 // H
    aug = jax.ShapeDtypeStruct((B, H, S, LANES), BF16)
    augt = jax.ShapeDtypeStruct((B, H, FLASH_VROWS, S), BF16)
    kv = jax.ShapeDtypeStruct((L, B, H, HD, S), F32)
    aug_spec = pl.BlockSpec((None, H, T, LANES), lambda b, t: (b, 0, t, 0))
    augt_spec = pl.BlockSpec((None, H, FLASH_VROWS, T), lambda b, t: (b, 0, 0, t))
    kv_spec = pl.BlockSpec((None, None, H, HD, T), lambda b, t: (l, b, 0, 0, t))
    in_specs = [pl.BlockSpec((None, T, 3 * H * HD), lambda b, t: (b, t, col)),
                pl.BlockSpec((None, T, LANES), lambda b, t: (b, t, 0)),
                pl.BlockSpec((1, LANES), lambda b, t: (0, 0)),
                pl.BlockSpec(memory_space=pl.ANY), pl.BlockSpec(memory_space=pl.ANY)]
    args = [proj, small, fb_row, k_prev, v_prev]
    aliases = {3: 3, 4: 4}
    kernel = lambda a, b, c, _k, _v, *rest: _foxprep_kernel(a, b, c, *rest)
    return pl.pallas_call(
        kernel,
        grid=(B, S // T),
        in_specs=in_specs,
        out_specs=[aug_spec, aug_spec, augt_spec, kv_spec, kv_spec,
                   pl.BlockSpec((None, T, LANES), lambda b, t: (b, t, 0))],
        out_shape=[aug, aug, augt, kv, kv, jax.ShapeDtypeStruct((B, S, LANES), F32)],
        scratch_shapes=[pltpu.VMEM((8, LANES), F32)],
        input_output_aliases=aliases,
        compiler_params=_cparams(("parallel", "arbitrary")),
        name="foxprep",
    )(*args)


FLASH_VROWS = 80
FLASH_QS = 256
FLASH_HEADS = 2
FLASH_AHEAD1 = 6
FLASH_S_SLOTS = 6
FLASH_P_SLOTS = 2


def _flash_kernel(qi_ref, kj_ref, qa_ref, ka_ref, vt_ref, z_ref, o_ref,
                  s_scr, p_scr, m_scr, acc_scr):
    TQ = qa_ref.shape[1]
    TK = ka_ref.shape[1]
    QS = FLASH_QS
    HD = LANES // 2
    nqs = TQ // QS
    HPS = qa_ref.shape[0]
    st = pl.program_id(2)
    i = qi_ref[st]
    j = kj_ref[st]
    nt = (((1,), (1,)), ((), ()))

    @pl.when(j == 0)
    def _():
        m_scr[...] = jnp.full(m_scr.shape, -jnp.inf, F32)
        acc_scr[...] = jnp.zeros_like(acc_scr)

    def unit(n):
        return n // HPS, n % HPS

    def scores(n, nk, diag):
        qs, hh = unit(n)
        slot = n % s_scr.shape[0]
        q = qa_ref[hh, qs * QS:(qs + 1) * QS, :]
        s = lax.dot_general(ka_ref[hh, 0:nk, :], q, nt, preferred_element_type=F32)
        if diag:
            keep = _iota2((QS, QS), 0) <= _iota2((QS, QS), 1)
            last = jnp.where(keep, s[nk - QS:nk, :], -jnp.inf)
            s_scr[slot, nk - QS:nk, :] = last
            m_cur = jnp.max(last, axis=0, keepdims=True)
            if nk > QS:
                s_scr[slot, 0:nk - QS, :] = s[0:nk - QS, :]
                m_cur = jnp.maximum(m_cur, jnp.max(s[0:nk - QS, :], axis=0, keepdims=True))
            return m_cur
        s_scr[slot, 0:nk, :] = s
        return jnp.max(s, axis=0, keepdims=True)

    def softmax(n, nk, m_cur):
        qs, hh = unit(n)
        m_prev = m_scr[hh, qs, 0:1, :]
        m_new = jnp.maximum(m_prev, m_cur)
        p_scr[n % p_scr.shape[0], 0:nk, :] = jnp.exp2(
            s_scr[n % s_scr.shape[0], 0:nk, :] - m_new).astype(BF16)
        m_scr[hh, qs, 0:1, :] = m_new
        return jnp.exp2(m_prev - m_new)

    def pv(n, nk, alpha):
        qs, hh = unit(n)
        o = jnp.dot(vt_ref[hh, :, 0:nk], p_scr[n % p_scr.shape[0], 0:nk, :],
                    preferred_element_type=F32)
        acc_scr[hh, qs] = acc_scr[hh, qs] * alpha + o

    ahead = FLASH_AHEAD1 - 1
    nu = HPS * nqs

    @pl.when(j < i)
    def _():
        m_curs = [scores(n, TK, False) for n in range(ahead)]
        for n in range(nu):
            alpha = softmax(n, TK, m_curs[n])
            if n + ahead < nu:
                m_curs.append(scores(n + ahead, TK, False))
            pv(n, TK, alpha)

    @pl.when(j == i)
    def _():
        nks = [(n // HPS + 1) * QS for n in range(nu)]
        m_curs = [scores(n, nks[n], True) for n in range(ahead)]
        for n in range(nu):
            alpha = softmax(n, nks[n], m_curs[n])
            if n + ahead < nu:
                m_curs.append(scores(n + ahead, nks[n + ahead], True))
            pv(n, nks[n], alpha)
        for qs in range(nqs):
            for pr in range(HPS // 2):
                a0 = acc_scr[2 * pr, qs]
                a1 = acc_scr[2 * pr + 1, qs]
                o = jnp.concatenate([a0[0:HD, :] / a0[HD:HD + 1, :],
                                     a1[0:HD, :] / a1[HD:HD + 1, :]], axis=0).T
                rows, cols = slice(qs * QS, (qs + 1) * QS), slice(pr * LANES, (pr + 1) * LANES)
                z = z_ref[rows, cols].astype(F32)
                o_ref[rows, cols] = (o * _silu(z)).astype(o_ref.dtype)


def _flash(qa, ka, vt, proj, zcol, tq):
    B, H, S, _ = qa.shape
    n = S // tq
    steps = [(i, j) for i in range(n) for j in range(i + 1)]
    qi = jnp.asarray([s[0] for s in steps], jnp.int32)
    kj = jnp.asarray([s[1] for s in steps], jnp.int32)
    nqs = tq // FLASH_QS
    hps = FLASH_HEADS
    ow = hps * LANES // 2
    grid_spec = pltpu.PrefetchScalarGridSpec(
        num_scalar_prefetch=2,
        grid=(B, H // hps, len(steps)),
        in_specs=[pl.BlockSpec((None, hps, tq, LANES), lambda b, p, s, qi, kj: (b, p, qi[s], 0)),
                  pl.BlockSpec((None, hps, tq, LANES), lambda b, p, s, qi, kj: (b, p, kj[s], 0)),
                  pl.BlockSpec((None, hps, FLASH_VROWS, tq), lambda b, p, s, qi, kj: (b, p, 0, kj[s])),
                  pl.BlockSpec((None, tq, ow), lambda b, p, s, qi, kj: (b, qi[s], zcol // (ow // LANES) + p))],
        out_specs=pl.BlockSpec((None, tq, ow), lambda b, p, s, qi, kj: (b, qi[s], p)),
        scratch_shapes=[pltpu.VMEM((FLASH_S_SLOTS, tq, FLASH_QS), F32),
                        pltpu.VMEM((FLASH_P_SLOTS, tq, FLASH_QS), BF16),
                        pltpu.VMEM((hps, nqs, 8, FLASH_QS), F32),
                        pltpu.VMEM((hps, nqs, FLASH_VROWS, FLASH_QS), F32)])
    return pl.pallas_call(
        _flash_kernel,
        grid_spec=grid_spec,
        out_shape=jax.ShapeDtypeStruct((B, S, H * LANES // 2), BF16),
        compiler_params=_cparams(("parallel", "parallel", "arbitrary")),
        name="flash",
    )(qi, kj, qa, ka, vt, proj)


def _cumsum_kernel(x_ref, o_ref, *, group):
    R = x_ref.shape[0]
    x = x_ref[...]
    up = jnp.where(_iota2((LANES, LANES), 0) <= _iota2((LANES, LANES), 1), 1.0, 0.0).astype(BF16)
    w = _dot_sel_r(x, up)
    tot = jnp.broadcast_to(w[:, LANES - 1:LANES], (R, LANES))
    ri = _iota2((R, R), 0)
    ci = _iota2((R, R), 1)
    prev = jnp.where((ci < ri) & ((ci // group) == (ri // group)), 1.0, 0.0).astype(BF16)
    o_ref[...] = w + _dot_sel_l(prev, tot)


def _cumsum_rows(x, group):
    R = x.shape[0]
    rb = min(R, 256)
    return pl.pallas_call(
        functools.partial(_cumsum_kernel, group=group),
        grid=(R // rb,),
        in_specs=[pl.BlockSpec((rb, LANES), lambda r: (r, 0))],
        out_specs=pl.BlockSpec((rb, LANES), lambda r: (r, 0)),
        out_shape=jax.ShapeDtypeStruct((R, LANES), F32),
        compiler_params=_cparams(("parallel",)),
        name="cumsum",
    )(x)


def _foxsamp_kernel(q_ref, k_ref, v_ref, z_ref, sm_ref, fb_ref, tot_ref, fcc_ref, ck_ref, cv_ref,
                    o_ref, ko_ref, vo_ref, lf_ref):
    S = q_ref.shape[0]
    HD = ck_ref.shape[1]
    p = pl.program_id(1)
    logf = _logf_from_small(sm_ref[...], fb_ref[...])
    lf_ref[...] = logf
    ltri = jnp.where(_iota2((S, S), 0) >= _iota2((S, S), 1), 1.0, 0.0).astype(BF16)
    fc = _dot_sel_l(ltri, logf) + tot_ref[...]
    fct = jnp.concatenate([fc, jnp.zeros((LANES - S, LANES), F32)], axis=0).T
    lane = _iota2((S, LANES), 1)
    row = _iota2((LANES, LANES), 0)
    causal = _iota2((S, S), 0) >= _iota2((S, S), 1)
    scale = HD ** -0.5
    for hh in range(2):
        h = 2 * p + hh
        fq = jnp.sum(jnp.where(lane == h, fc, 0.0), axis=1, keepdims=True)
        fkn = jnp.sum(jnp.where(row == h, fct, 0.0), axis=0, keepdims=True)[:, :S]
        q = q_ref[:, hh * HD:(hh + 1) * HD].astype(F32) * scale
        kn = k_ref[:, hh * HD:(hh + 1) * HD].astype(F32)
        vn = v_ref[:, hh * HD:(hh + 1) * HD].astype(F32)
        ko_ref[hh] = kn
        vo_ref[hh] = vn
        sc = _dot(q, ck_ref[hh]) + fq - fcc_ref[hh]
        sn = jnp.where(causal, _dot_nt(q, kn) + fq - fkn, -jnp.inf)
        m = jnp.maximum(jnp.max(sc, axis=1, keepdims=True), jnp.max(sn, axis=1, keepdims=True))
        pc = jnp.exp(sc - m)
        pn = jnp.exp(sn - m)
        l = jnp.sum(pc, axis=1, keepdims=True) + jnp.sum(pn, axis=1, keepdims=True)
        o = (_dot_nt(pc, cv_ref[hh]) + _dot(pn, vn)) / l
        o_ref[:, hh * HD:(hh + 1) * HD] = o * _silu(z_ref[:, hh * HD:(hh + 1) * HD].astype(F32))


def _foxsamp(proj, small, cols, fb_row, tot, fcc, cache_k, cache_v, l):
    B, S, _ = proj.shape
    _, _, H, HD, P = cache_k.shape
    cq, ck, cv, cz = cols
    pair = lambda c: pl.BlockSpec((None, S, LANES), lambda b, p: (b, 0, c + p))
    kv_out = jax.ShapeDtypeStruct((B, H, S, HD), F32)
    return pl.pallas_call(
        _foxsamp_kernel,
        grid=(B, H // 2),
        in_specs=[pair(cq), pair(ck), pair(cv), pair(cz),
                  pl.BlockSpec((None, S, LANES), lambda b, p: (b, 0, 0)),
                  pl.BlockSpec((1, LANES), lambda b, p: (0, 0)),
                  pl.BlockSpec((None, 1, LANES), lambda b, p: (b, 0, 0)),
                  pl.BlockSpec((None, 2, 1, P), lambda b, p: (b, p, 0, 0)),
                  pl.BlockSpec((None, None, 2, HD, P), lambda b, p: (l, b, p, 0, 0)),
                  pl.BlockSpec((None, None, 2, HD, P), lambda b, p: (l, b, p, 0, 0))],
        out_specs=[pl.BlockSpec((None, S, LANES), lambda b, p: (b, 0, p)),
                   pl.BlockSpec((None, 2, S, HD), lambda b, p: (b, p, 0, 0)),
                   pl.BlockSpec((None, 2, S, HD), lambda b, p: (b, p, 0, 0)),
                   pl.BlockSpec((None, S, LANES), lambda b, p: (b, 0, 0))],
        out_shape=[jax.ShapeDtypeStruct((B, S, H * HD), F32), kv_out, kv_out,
                   jax.ShapeDtypeStruct((B, S, LANES), F32)],
        compiler_params=_cparams(("parallel", "arbitrary")),
        name="foxsamp",
    )(proj, proj, proj, proj, small, fb_row, tot, fcc, cache_k, cache_v)


COL_MG = 0
COL_GQKV = 6
COL_FQKV = 9
COL_RQK = 12
COL_RV = 13
COL_RZ = 14
COL_FZ = 15
COL_GZ = 16


def _prep_w_in(w_in, D):
    BW = D // 2
    hk = BW // 2
    splits = (hk, hk, BW, BW, BW, BW, BW, H_FOX, BW, BW, BW, BW, H_GDN, H_GDN, BW, N_BRANCH * D)
    offs = np.cumsum((0,) + splits)
    seg = lambda i: w_in[..., offs[i]:offs[i + 1]]
    rq, rk, rv, rz, fq, fk, fv, ff, fz, gq, gk, gv, ga, gb, gz, mg = [seg(i) for i in range(16)]
    main = jnp.concatenate([mg, gq, gk, gv, fq, fk, fv, rq, rk, rv, rz, fz, gz], axis=-1)
    pad = jnp.zeros(w_in.shape[:-1] + (LANES - H_FOX - 2 * H_GDN,), w_in.dtype)
    small = jnp.concatenate([ff, ga, gb, pad], axis=-1)
    return main.astype(BF16), small.astype(BF16)


def _lane_row(vals, off):
    L, n = vals.shape
    return jnp.zeros((L, 1, LANES), F32).at[:, 0, off:off + n].set(vals)


def kernel(x_prompt, x_sample, c_prompt, c_sample, cache_fox_k, cache_fox_v, cache_fox_logf,
           state_ret, state_gdn, state_gdn_conv, norm_w, ada_w, ada_b, w_in, fox_f_bias,
           gdn_a_log, gdn_dt_bias, gdn_conv_w, ret_norm_w, gdn_norm_w, w_branch, w_out,
           final_norm_w):
    B, S, D = x_prompt.shape
    BS, SS, _ = x_sample.shape
    L = ada_w.shape[0]
    P = cache_fox_k.shape[3]
    DK_RET, DV_RET = state_ret.shape[-2:]
    DK_GDN, DV_GDN = state_gdn.shape[-2:]

    rows = -(-(B + BS) // 8) * 8
    c_all = jnp.concatenate([c_prompt, c_sample, jnp.zeros((rows - B - BS, D), F32)], axis=0)
    mod = _adaln(c_all, ada_w, ada_b)
    mod_p = mod[:, :, :B].reshape(L, 3, B, 1, D)
    mod_s = mod[:, :, B:B + BS].reshape(L, 3, BS, 1, D)

    w_main, w_small = _prep_w_in(w_in, D)
    w_branch_b = w_branch.astype(BF16)
    w_out_b = w_out.astype(BF16)
    norm_w3 = norm_w.reshape(L, 1, D)
    final_w = final_norm_w.reshape(1, D)
    ret_nw = ret_norm_w.reshape(L, H_RET, 1, DV_RET)
    gdn_nw = gdn_norm_w.reshape(L, 1, DV_GDN)
    fb_rows = _lane_row(fox_f_bias, 0)
    a_rows = _lane_row(gdn_a_log, H_FOX)
    dt_rows = _lane_row(gdn_dt_bias, H_FOX)

    tm_p = min(S, 2048)
    tmm_p = min(S, 512)
    t_ret = min(S, 256)
    t_gdn = min(S, 512)
    t_prep = min(S, 256)
    t_flash = min(S, 1024)

    tabs_p = _ret_tables(t_ret, 0, S, DK_RET)
    tabs_s = _ret_tables(SS, P, SS, DK_RET)

    fcc = _cumsum_rows(cache_fox_logf.reshape(-1, LANES), P // LANES).reshape(L, BS, H_FOX, 1, P)
    tot = jnp.zeros((L, BS, 1, LANES), F32).at[:, :, 0, :H_FOX].set(fcc[:, :, :, 0, P - 1])

    zeros_ret = jnp.zeros((B, H_RET, DK_RET, DV_RET), F32)
    zeros_gdn = jnp.zeros((B, H_GDN, DK_GDN, DV_GDN), F32)
    zeros_conv = jnp.zeros((B, CONV_W - 1, gdn_conv_w.shape[-1]), F32)

    xp, xs = x_prompt, x_sample
    cache_kt = jnp.swapaxes(cache_fox_k, 3, 4)
    cache_vt = jnp.swapaxes(cache_fox_v, 3, 4)
    pk = jnp.zeros((L, B, H_FOX, cache_fox_k.shape[-1], S), F32)
    pv = jnp.zeros_like(pk)
    p_lf, p_ret, p_gdn, p_conv = [], [], [], []
    s_k, s_v, s_lf, s_ret, s_gdn, s_conv = [], [], [], [], [], []
    for l in range(L):
        final = l == L - 1
        proj, small = _inproj(xp, mod_p, l, norm_w3, w_main, w_small, 1, tm_p)
        o_ret, st = _ret(proj, (COL_RQK, COL_RV, COL_RZ), tabs_p, ret_nw[l], zeros_ret, t_ret, BF16)
        p_ret.append(st)
        o_gdn, st, cv = _gdn(proj, small, (COL_GQKV // 3, COL_GZ), gdn_conv_w[l], zeros_conv,
                             a_rows[l], dt_rows[l], gdn_nw[l], zeros_gdn, t_gdn, BF16)
        p_gdn.append(st)
        p_conv.append(cv)
        qa, ka, va, pk, pv, lf = _foxprep(proj, small, COL_FQKV // 3, fb_rows[l], l, pk, pv, L, t_prep)
        p_lf.append(lf)
        o_fox = _flash(qa, ka, va, proj, COL_FZ * 4, t_flash)
        xp = _merge(xp, mod_p, l, o_ret, o_fox, o_gdn, proj, w_branch_b, w_out_b, final_w,
                    1, tmm_p, final)
        proj, small = _inproj(xs, mod_s, l, norm_w3, w_main, w_small, BS, SS)
        o_ret, st = _ret(proj, (COL_RQK, COL_RV, COL_RZ), tabs_s, ret_nw[l], state_ret[l], SS, F32)
        s_ret.append(st)
        o_gdn, st, cv = _gdn(proj, small, (COL_GQKV // 3, COL_GZ), gdn_conv_w[l], state_gdn_conv[l],
                             a_rows[l], dt_rows[l], gdn_nw[l], state_gdn[l], SS, F32)
        s_gdn.append(st)
        s_conv.append(cv)
        o_fox, kk, vv, lf = _foxsamp(proj, small,
                                     (COL_FQKV * 4, (COL_FQKV + 1) * 4, (COL_FQKV + 2) * 4, COL_FZ * 4),
                                     fb_rows[l], tot[l], fcc[l], cache_kt, cache_vt, l)
        s_k.append(kk)
        s_v.append(vv)
        s_lf.append(lf)
        xs = _merge(xs, mod_s, l, o_ret, o_fox, o_gdn, proj, w_branch_b, w_out_b, final_w,
                    BS, SS, final)

    logf_out = lambda lfs: jnp.stack(lfs)[..., :H_FOX].transpose(0, 1, 3, 2)
    return (xp, xs,
            jnp.swapaxes(pk, 3, 4), jnp.swapaxes(pv, 3, 4), logf_out(p_lf), jnp.stack(p_ret), jnp.stack(p_gdn), jnp.stack(p_conv),
            jnp.stack(s_k), jnp.stack(s_v), logf_out(s_lf), jnp.stack(s_ret), jnp.stack(s_gdn),
            jnp.stack(s_conv))
```

```python
import functools

import numpy as np
import jax
import jax.numpy as jnp
from jax import lax
from jax.experimental import pallas as pl
from jax.experimental.pallas import tpu as pltpu

F32 = jnp.float32
BF16 = jnp.bfloat16

N_BRANCH = 3
H_RET = 4
H_FOX = 8
H_GDN = 4
CONV_W = 4
CHUNK = 64
NORM_EPS = 1e-6
ROPE_BASE = 10000.0
LOG2E = 1.4426950408889634
LANES = 128
GDN_CHUNK = 128
VMEM_LIMIT = 56 * 1024 * 1024


def _cparams(sem):
    return pltpu.CompilerParams(dimension_semantics=sem, vmem_limit_bytes=VMEM_LIMIT)


def _sigmoid(x):
    return 0.5 * jnp.tanh(0.5 * x) + 0.5


def _silu(x):
    return x * _sigmoid(x)


def _softplus(x):
    return jnp.maximum(x, 0.0) + jnp.log(1.0 + jnp.exp(-jnp.abs(x)))


def _dot(a, b):
    return jnp.dot(a.astype(BF16), b.astype(BF16), preferred_element_type=F32)


def _dot_nt(a, b):
    return lax.dot_general(a.astype(BF16), b.astype(BF16), (((1,), (1,)), ((), ())),
                           preferred_element_type=F32)


def _split3(x):
    hi = x.astype(BF16)
    r = x - hi.astype(F32)
    mid = r.astype(BF16)
    lo = (r - mid.astype(F32)).astype(BF16)
    return hi, mid, lo


def _dot_sel_l(m, x):
    hi, mid, lo = _split3(x)
    d = lambda y: jnp.dot(m, y, preferred_element_type=F32)
    return d(hi) + (d(mid) + d(lo))


def _dot_sel_r(x, m):
    hi, mid, lo = _split3(x)
    d = lambda y: jnp.dot(y, m, preferred_element_type=F32)
    return d(hi) + (d(mid) + d(lo))


def _iota2(shape, dim):
    return lax.broadcasted_iota(jnp.int32, shape, dim)


def _adaln_kernel(c_ref, w_ref, b_ref, o_ref):
    s = _silu(c_ref[...])
    o_ref[...] = _dot(s, w_ref[...]) + b_ref[...]


def _adaln(c_all, ada_w, ada_b):
    L, D, _ = ada_w.shape
    R = c_all.shape[0]
    return pl.pallas_call(
        _adaln_kernel,
        grid=(L, 3),
        in_specs=[pl.BlockSpec((R, D), lambda l, k: (0, 0)),
                  pl.BlockSpec((None, D, D), lambda l, k: (l, 0, k)),
                  pl.BlockSpec((None, None, 1, D), lambda l, k: (l, k, 0, 0))],
        out_specs=pl.BlockSpec((None, None, R, D), lambda l, k: (l, k, 0, 0)),
        out_shape=jax.ShapeDtypeStruct((L, 3, R, D), F32),
        compiler_params=_cparams(("parallel", "parallel")),
        name="adaln",
    )(c_all, ada_w, ada_b.reshape(L, 3, 1, D))


PROJ_TN = 512
PROJ_DTYPE = BF16


def _inproj_kernel(x_ref, nw_ref, sc_ref, sh_ref, w_ref, ws_ref, o_ref, os_ref, h_ref):
    bb, tm, D = x_ref.shape

    @pl.when(pl.program_id(2) == 0)
    def _():
        x = x_ref[...]
        ms = jnp.mean(x * x, axis=-1, keepdims=True)
        xn = (x * lax.rsqrt(ms + NORM_EPS)) * nw_ref[...]
        h = xn * (1.0 + sc_ref[...]) + sh_ref[...]
        hb = h.astype(BF16).reshape(bb * tm, D)
        h_ref[...] = hb
        os_ref[...] = jnp.dot(hb, ws_ref[...], preferred_element_type=F32).reshape(bb, tm, LANES)

    o_ref[...] = jnp.dot(h_ref[...], w_ref[...],
                         preferred_element_type=F32).reshape(bb, tm, PROJ_TN).astype(o_ref.dtype)


def _inproj(x, mod, l, norm_w, w_main, w_small, bb, tm):
    B, S, D = x.shape
    NP = w_main.shape[-1]
    nj = NP // PROJ_TN
    return pl.pallas_call(
        _inproj_kernel,
        grid=(B // bb, S // tm, nj),
        in_specs=[pl.BlockSpec((bb, tm, D), lambda b, i, j: (b, i, 0)),
                  pl.BlockSpec((None, 1, D), lambda b, i, j: (l, 0, 0)),
                  pl.BlockSpec((None, None, bb, 1, D), lambda b, i, j: (l, 1, b, 0, 0)),
                  pl.BlockSpec((None, None, bb, 1, D), lambda b, i, j: (l, 0, b, 0, 0)),
                  pl.BlockSpec((None, D, PROJ_TN), lambda b, i, j: (l, 0, j)),
                  pl.BlockSpec((None, D, LANES), lambda b, i, j: (l, 0, 0))],
        out_specs=[pl.BlockSpec((bb, tm, PROJ_TN), lambda b, i, j: (b, i, j)),
                   pl.BlockSpec((bb, tm, LANES), lambda b, i, j: (b, i, 0))],
        out_shape=[jax.ShapeDtypeStruct((B, S, NP), PROJ_DTYPE),
                   jax.ShapeDtypeStruct((B, S, LANES), F32)],
        scratch_shapes=[pltpu.VMEM((bb * tm, D), BF16)],
        compiler_params=_cparams(("parallel", "parallel", "arbitrary")),
        name="inproj",
    )(x, norm_w, mod, mod, w_main, w_small)


def _merge_kernel(x_ref, g_ref, o1_ref, o2_ref, o3_ref, mg_ref, wb_ref, wo_ref, fw_ref, y_ref,
                  *, final):
    bb, tm, D = x_ref.shape
    M = bb * tm
    acc = None
    for n, o_ref in enumerate((o1_ref, o2_ref, o3_ref)):
        br = jnp.dot(o_ref[...].reshape(M, o_ref.shape[-1]).astype(BF16), wb_ref[n],
                     preferred_element_type=F32)
        gate = _sigmoid(mg_ref[:, :, n * D:(n + 1) * D].astype(F32).reshape(M, D))
        term = br * gate
        acc = term if acc is None else acc + term
    out = jnp.dot(acc.astype(BF16), wo_ref[...], preferred_element_type=F32)
    xn = x_ref[...] + g_ref[...] * out.reshape(bb, tm, D)
    if final:
        ms = jnp.mean(xn * xn, axis=-1, keepdims=True)
        xn = (xn * lax.rsqrt(ms + NORM_EPS)) * fw_ref[...]
    y_ref[...] = xn


def _merge(x, mod, l, o_ret, o_fox, o_gdn, proj, w_branch, w_out, final_w, bb, tm, final):
    B, S, D = x.shape
    W = o_ret.shape[-1]
    return pl.pallas_call(
        functools.partial(_merge_kernel, final=final),
        grid=(B // bb, S // tm),
        in_specs=[pl.BlockSpec((bb, tm, D), lambda b, i: (b, i, 0)),
                  pl.BlockSpec((None, None, bb, 1, D), lambda b, i: (l, 2, b, 0, 0)),
                  pl.BlockSpec((bb, tm, W), lambda b, i: (b, i, 0)),
                  pl.BlockSpec((bb, tm, W), lambda b, i: (b, i, 0)),
                  pl.BlockSpec((bb, tm, W), lambda b, i: (b, i, 0)),
                  pl.BlockSpec((bb, tm, N_BRANCH * D), lambda b, i: (b, i, 0)),
                  pl.BlockSpec((None, N_BRANCH, W, D), lambda b, i: (l, 0, 0, 0)),
                  pl.BlockSpec((None, D, D), lambda b, i: (l, 0, 0)),
                  pl.BlockSpec((1, D), lambda b, i: (0, 0))],
        out_specs=pl.BlockSpec((bb, tm, D), lambda b, i: (b, i, 0)),
        out_shape=jax.ShapeDtypeStruct((B, S, D), F32),
        compiler_params=_cparams(("parallel", "parallel")),
        name="merge",
    )(x, mod, o_ret, o_fox, o_gdn, proj, w_branch, w_out, final_w)


def _ret_kernel(qk_ref, v_ref, z_ref, cos_ref, sin_ref, m_ref, dq_ref, dk_ref, dc_ref, nw_ref,
                s0_ref, o_ref, so_ref, s_ref):
    T = qk_ref.shape[0]
    H = s_ref.shape[0]
    DK = s0_ref.shape[1]
    t = pl.program_id(1)

    @pl.when(t == 0)
    def _():
        zero = jnp.zeros((DK, LANES), F32)
        for h in range(H):
            s0 = s0_ref[h]
            s_ref[h] = jnp.concatenate([s0, zero] if h % 2 == 0 else [zero, s0], axis=0)

    cos = cos_ref[...]
    sin = sin_ref[...]
    lane = _iota2((T, LANES), 1)
    first = (lane % DK) < (DK // 2)
    low = lane < DK

    def rot(x):
        sw = jnp.where(first, pltpu.roll(x, LANES - DK // 2, 1), pltpu.roll(x, DK // 2, 1))
        return x * cos + sw * sin

    nkq = H * DK
    hs = range(H)
    qcs = [rot(qk_ref[:, p * LANES:(p + 1) * LANES].astype(F32)) for p in range(H // 2)]
    kcs = [rot(qk_ref[:, nkq + p * LANES:nkq + (p + 1) * LANES].astype(F32)) * (DK ** -0.5)
           for p in range(H // 2)]
    kcbs = [kc.astype(BF16) for kc in kcs]
    sels = [low if h % 2 == 0 else jnp.logical_not(low) for h in hs]
    qms = [jnp.where(sels[h], qcs[h // 2], 0.0) for h in hs]
    vhs = [v_ref[:, h * LANES:(h + 1) * LANES].astype(BF16) for h in hs]
    ss = [s_ref[h] for h in hs]
    ams = [_dot_nt(qms[h], kcbs[h // 2]) * m_ref[h] for h in hs]
    kts = [(jnp.where(sels[h], kcs[h // 2], 0.0) * dk_ref[h]).T for h in hs]
    os_ = [_dot(ams[h], vhs[h]) + _dot(qms[h] * dq_ref[h], ss[h]) for h in hs]
    for h in hs:
        s_ref[h] = ss[h] * dc_ref[h] + _dot(kts[h], vhs[h])
    for h in hs:
        o = os_[h]
        ms = jnp.mean(o * o, axis=-1, keepdims=True)
        on = (o * lax.rsqrt(ms + NORM_EPS)) * nw_ref[h]
        z = z_ref[:, h * LANES:(h + 1) * LANES].astype(F32)
        o_ref[:, h * LANES:(h + 1) * LANES] = (on * _silu(z)).astype(o_ref.dtype)

    @pl.when(t == pl.num_programs(1) - 1)
    def _():
        for h in range(H):
            so_ref[h] = s_ref[h][(h % 2) * DK:(h % 2 + 1) * DK, :]


def _ret_tables(T, pos0, S, DK):
    half = DK // 2
    inv_freq = ROPE_BASE ** (-jnp.arange(half, dtype=F32) / half)
    pos = pos0 + jnp.arange(S)
    ang = pos.astype(F32)[:, None] * inv_freq[None, :]
    cos, sin = jnp.cos(ang), jnp.sin(ang)
    reps = LANES // DK
    cos_t = jnp.tile(jnp.concatenate([cos, cos], axis=1), (1, reps))
    sin_t = jnp.tile(jnp.concatenate([-sin, sin], axis=1), (1, reps))
    log_g = jnp.log1p(-jnp.exp2(-5.0 - jnp.arange(H_RET, dtype=F32)))
    i = jnp.arange(T)
    dist = jnp.abs(i[:, None] - i[None, :]).astype(F32)
    allowed = (i[None, :] // CHUNK) <= (i[:, None] // CHUNK)
    m = jnp.where(allowed[None], jnp.exp(dist[None] * log_g[:, None, None]), 0.0)
    fi = i.astype(F32)
    dq = jnp.exp((fi[None, :] + 1.0) * log_g[:, None])
    dk = jnp.exp((T - 1.0 - fi)[None, :] * log_g[:, None])
    dc = jnp.exp(T * log_g)
    bc = lambda a: jnp.broadcast_to(a[..., None], a.shape + (LANES,))
    return cos_t, sin_t, m, bc(dq), bc(dk), bc(dc[:, None])


def _ret(proj, cols, tabs, ret_nw, s0, T, out_dtype):
    B, S, _ = proj.shape
    H, DK, DV = s0.shape[1:]
    W = H * DV
    cqk, cv, cz = cols
    cos_t, sin_t, m, dq, dk, dc = tabs
    return pl.pallas_call(
        _ret_kernel,
        grid=(B, S // T),
        in_specs=[pl.BlockSpec((None, T, W), lambda b, t: (b, t, cqk)),
                  pl.BlockSpec((None, T, W), lambda b, t: (b, t, cv)),
                  pl.BlockSpec((None, T, W), lambda b, t: (b, t, cz)),
                  pl.BlockSpec((T, LANES), lambda b, t: (t, 0)),
                  pl.BlockSpec((T, LANES), lambda b, t: (t, 0)),
                  pl.BlockSpec((H, T, T), lambda b, t: (0, 0, 0)),
                  pl.BlockSpec((H, T, LANES), lambda b, t: (0, 0, 0)),
                  pl.BlockSpec((H, T, LANES), lambda b, t: (0, 0, 0)),
                  pl.BlockSpec((H, 1, LANES), lambda b, t: (0, 0, 0)),
                  pl.BlockSpec((H, 1, DV), lambda b, t: (0, 0, 0)),
                  pl.BlockSpec((None, H, DK, DV), lambda b, t: (b, 0, 0, 0))],
        out_specs=[pl.BlockSpec((None, T, W), lambda b, t: (b, t, 0)),
                   pl.BlockSpec((None, H, DK, DV), lambda b, t: (b, 0, 0, 0))],
        out_shape=[jax.ShapeDtypeStruct((B, S, W), out_dtype),
                   jax.ShapeDtypeStruct((B, H, DK, DV), F32)],
        scratch_shapes=[pltpu.VMEM((H, LANES, DV), F32)],
        compiler_params=_cparams(("parallel", "arbitrary")),
        name="ret",
    )(proj, proj, proj, cos_t, sin_t, m, dq, dk, dc, ret_nw, s0)


INV_BASE = 8


def _unit_lower_inverses(mats, eye, ri, ci):
    n = mats[0].shape[0]
    b = INV_BASE
    d = lambda x, y: jnp.dot(x, y, preferred_element_type=F32)
    same = (ri // b) == (ci // b)
    ads = [jnp.where(same, a, 0.0) for a in mats]
    xs = [eye - ad for ad in ads]
    ps = [ad.astype(BF16) for ad in ads]
    k = 2
    while k < b:
        ps = [d(p, p).astype(BF16) for p in ps]
        xs = [x + d(x.astype(BF16), p) for x, p in zip(xs, ps)]
        k *= 2
    while b < n:
        off = ((ri // (2 * b)) == (ci // (2 * b))) & ((ri // b) != (ci // b))
        xbs = [x.astype(BF16) for x in xs]
        ts = [d(jnp.where(off, a, 0.0).astype(BF16), xb) for a, xb in zip(mats, xbs)]
        xs = [x - d(xb, t.astype(BF16)) for x, xb, t in zip(xs, xbs, ts)]
        b *= 2
    return xs


def _gdn_kernel(u_ref, z_ref, sm_ref, cw_ref, cb_ref, al_ref, dt_ref, nw_ref, s0_ref,
                o_ref, so_ref, co_ref, ext_ref, s_ref):
    T, C3 = u_ref.shape
    H = s_ref.shape[0]
    DK = s_ref.shape[1]
    W = H * DK
    C = GDN_CHUNK
    TP = -(-T // C) * C
    t = pl.program_id(1)
    last = pl.num_programs(1) - 1
    base = 8
    lo = base - (CONV_W - 1)

    @pl.when(t == 0)
    def _():
        s_ref[...] = s0_ref[...]
        ext_ref[lo:base, :] = cb_ref[...]

    ext_ref[base:base + T, :] = u_ref[...].astype(F32)
    y = ext_ref[lo:lo + T, :] * cw_ref[0:1, :]
    for j in range(1, CONV_W):
        y = y + ext_ref[lo + j:lo + j + T, :] * cw_ref[j:j + 1, :]
    tail = ext_ref[lo + T:base + T, :]
    ext_ref[lo:base, :] = tail

    @pl.when(t == last)
    def _():
        co_ref[...] = tail

    qkv = _silu(y)
    sm = sm_ref[...]
    g_all = -jnp.exp(al_ref[...]) * _softplus(sm + dt_ref[...])
    b_all = _sigmoid(sm)
    z_all = z_ref[...].astype(F32)
    if TP > T:
        pad = lambda a: jnp.concatenate([a, jnp.zeros((TP - T, a.shape[1]), F32)], axis=0)
        qkv, g_all, b_all, z_all = pad(qkv), pad(g_all), pad(b_all), pad(z_all)

    ri = _iota2((C, C), 0)
    ci = _iota2((C, C), 1)
    tri = ri >= ci
    strict = ri > ci
    ltri = jnp.where(tri, 1.0, 0.0).astype(BF16)
    eye = jnp.where(ri == ci, 1.0, 0.0)

    nc = TP // C
    items = [(c, h) for c in range(nc) for h in range(H)]
    gcums = [_dot_sel_l(ltri, g_all[c * C:(c + 1) * C, :]) for c in range(nc)]

    qs, ks, vs, gcs, bcs, decs, kbs = [], [], [], [], [], [], []
    for c, h in items:
        r0 = c * C
        q = qkv[r0:r0 + C, h * DK:(h + 1) * DK]
        k = qkv[r0:r0 + C, W + h * DK:W + (h + 1) * DK]
        qs.append(q * lax.rsqrt(jnp.sum(q * q, axis=-1, keepdims=True) + NORM_EPS) * (DK ** -0.5))
        k = k * lax.rsqrt(jnp.sum(k * k, axis=-1, keepdims=True) + NORM_EPS)
        ks.append(k)
        kbs.append(k.astype(BF16))
        vs.append(qkv[r0:r0 + C, 2 * W + h * DK:2 * W + (h + 1) * DK])
        gc = gcums[c][:, 8 + h:9 + h]
        gcs.append(gc)
        bcs.append(b_all[r0:r0 + C, 12 + h:13 + h])
        gm = jnp.broadcast_to(gc, (C, C))
        decs.append(jnp.exp(jnp.where(tri, gm - gm.T, -jnp.inf)))
    kks = [_dot_nt(kb, kb) for kb in kbs]
    amats = [jnp.where(strict, kk * dec, 0.0) * bc for kk, dec, bc in zip(kks, decs, bcs)]
    tinvs = _unit_lower_inverses(amats, eye, ri, ci)
    egs = [jnp.exp(gc) for gc in gcs]
    rhss = [jnp.concatenate([k * (bc * eg), v * bc], axis=1)
            for k, v, bc, eg in zip(ks, vs, bcs, egs)]
    wus = [_dot(tinv, rhs) for tinv, rhs in zip(tinvs, rhss)]
    qks = [(_dot_nt(q, kb) * dec).astype(BF16) for q, kb, dec in zip(qs, kbs, decs)]
    qes = [(q * eg).astype(BF16) for q, eg in zip(qs, egs)]
    glasts = [gcums[c][C - 1:C, 8 + h:9 + h] for c, h in items]
    kdts = [(k * jnp.exp(gl - gc)).T.astype(BF16) for k, gl, gc in zip(ks, glasts, gcs)]
    sdec = [jnp.exp(gl) for gl in glasts]

    for c in range(nc):
        idx = [c * H + h for h in range(H)]
        ss = [s_ref[h] for h in range(H)]
        sbs = [s.astype(BF16) for s in ss]
        us = [wus[i][:, DK:] - _dot(wus[i][:, :DK], sb) for i, sb in zip(idx, sbs)]
        ubs = [u.astype(BF16) for u in us]
        os_ = [jnp.dot(qes[i], sb, preferred_element_type=F32)
               + jnp.dot(qks[i], ub, preferred_element_type=F32)
               for i, sb, ub in zip(idx, sbs, ubs)]
        for h, i in enumerate(idx):
            s_ref[h] = ss[h] * sdec[i] + jnp.dot(kdts[i], ubs[h], preferred_element_type=F32)
        for h, o in enumerate(os_):
            ms = jnp.mean(o * o, axis=-1, keepdims=True)
            on = (o * lax.rsqrt(ms + NORM_EPS)) * nw_ref[...]
            z = z_all[c * C:(c + 1) * C, h * DK:(h + 1) * DK]
            res = (on * _silu(z)).astype(o_ref.dtype)
            if TP > T:
                o_ref[:, h * DK:(h + 1) * DK] = res[:T, :]
            else:
                o_ref[c * C:(c + 1) * C, h * DK:(h + 1) * DK] = res

    @pl.when(t == last)
    def _():
        so_ref[...] = s_ref[...]


def _gdn(proj, small, cols, conv_w, conv_buf, a_row, dt_row, gdn_nw, s0, T, out_dtype):
    B, S, _ = proj.shape
    H, DK, DV = s0.shape[1:]
    W = H * DK
    C3 = conv_w.shape[-1]
    cu, cz = cols
    return pl.pallas_call(
        _gdn_kernel,
        grid=(B, S // T),
        in_specs=[pl.BlockSpec((None, T, C3), lambda b, t: (b, t, cu)),
                  pl.BlockSpec((None, T, W), lambda b, t: (b, t, cz)),
                  pl.BlockSpec((None, T, LANES), lambda b, t: (b, t, 0)),
                  pl.BlockSpec((CONV_W, C3), lambda b, t: (0, 0)),
                  pl.BlockSpec((None, CONV_W - 1, C3), lambda b, t: (b, 0, 0)),
                  pl.BlockSpec((1, LANES), lambda b, t: (0, 0)),
                  pl.BlockSpec((1, LANES), lambda b, t: (0, 0)),
                  pl.BlockSpec((1, DV), lambda b, t: (0, 0)),
                  pl.BlockSpec((None, H, DK, DV), lambda b, t: (b, 0, 0, 0))],
        out_specs=[pl.BlockSpec((None, T, W), lambda b, t: (b, t, 0)),
                   pl.BlockSpec((None, H, DK, DV), lambda b, t: (b, 0, 0, 0)),
                   pl.BlockSpec((None, CONV_W - 1, C3), lambda b, t: (b, 0, 0))],
        out_shape=[jax.ShapeDtypeStruct((B, S, W), out_dtype),
                   jax.ShapeDtypeStruct((B, H, DK, DV), F32),
                   jax.ShapeDtypeStruct((B, CONV_W - 1, C3), F32)],
        scratch_shapes=[pltpu.VMEM((T + 8, C3), F32),
                        pltpu.VMEM((H, DK, DV), F32)],
        compiler_params=_cparams(("parallel", "arbitrary")),
        name="gdn",
    )(proj, proj, small, conv_w, conv_buf, a_row, dt_row, gdn_nw, s0)


def _logf_from_small(sm, fb):
    lane = _iota2(sm.shape, 1)
    return jnp.where(lane < H_FOX, -_softplus(-(sm + fb)), 0.0)


def _foxprep_kernel(qkv_ref, sm_ref, fb_ref, qa_ref, ka_ref, vt_ref, ko_ref, vo_ref, lf_ref,
                    carry_ref):
    T = sm_ref.shape[0]
    H = qa_ref.shape[0]
    HD = ko_ref.shape[1]
    W = H * HD
    t = pl.program_id(1)

    @pl.when(t == 0)
    def _():
        carry_ref[...] = jnp.zeros_like(carry_ref)

    logf = _logf_from_small(sm_ref[...], fb_ref[...])
    lf_ref[...] = logf
    ltri = jnp.where(_iota2((T, T), 0) >= _iota2((T, T), 1), 1.0, 0.0).astype(BF16)
    fc = _dot_sel_l(ltri, logf) + carry_ref[0:1, :]
    carry_ref[0:1, :] = fc[T - 1:T, :]
    pieces = jnp.concatenate(_split3(fc * LOG2E), axis=1)
    r = _iota2((3 * LANES, 2 * LANES), 0)
    c = _iota2((3 * LANES, 2 * LANES), 1)
    rh, rj = r % LANES, r // LANES
    place = jnp.where((rh < H) & (c == HD + 3 * rh + rj), 1.0,
                      jnp.where((rh < H) & (c == LANES + HD + 3 * H + 3 * rh + rj), -1.0, 0.0))
    ebias = jnp.dot(pieces, place.astype(BF16), preferred_element_type=F32)
    e_q, e_k = ebias[:, :LANES], ebias[:, LANES:]

    lane = _iota2((T, LANES), 1)
    low = lane < HD
    ones_rows = jnp.where(_iota2((FLASH_VROWS - HD, T), 0) == 0, 1.0, 0.0)
    scale = (HD ** -0.5) * LOG2E
    for p in range(H // 2):
        qc = qkv_ref[:, p * LANES:(p + 1) * LANES].astype(F32) * scale
        kc = qkv_ref[:, W + p * LANES:W + (p + 1) * LANES].astype(F32)
        kct = kc.T
        vct = qkv_ref[:, 2 * W + p * LANES:2 * W + (p + 1) * LANES].astype(F32).T
        for half in range(2):
            h = 2 * p + half
            if half == 1:
                qh, kh = [pltpu.roll(a, HD, 1) for a in (qc, kc)]
            else:
                qh, kh = qc, kc
            vht = vct[half * HD:(half + 1) * HD, :]
            own_k = (lane >= HD + 3 * h) & (lane < HD + 3 * h + 3)
            own_q = (lane >= HD + 3 * H + 3 * h) & (lane < HD + 3 * H + 3 * h + 3)
            qa_ref[h] = jnp.where(low, qh, jnp.where(own_q, 1.0, e_q)).astype(BF16)
            ka_ref[h] = jnp.where(low, kh, jnp.where(own_k, 1.0, e_k)).astype(BF16)
            vt_ref[h] = jnp.concatenate([vht, ones_rows], axis=0).astype(BF16)
            ko_ref[h] = kct[half * HD:(half + 1) * HD, :]
            vo_ref[h] = vht


def _foxprep(proj, small, col, fb_row, l, k_prev, v_prev, L, T):
    B, S, _ = proj.shape
    H = H_FOX
    HD = 512 // H
    aug = jax.ShapeDtypeStruct((B, H, S, LANES), BF16)
    augt = jax.ShapeDtypeStruct((B, H, FLASH_VROWS, S), BF16)
    kv = jax.ShapeDtypeStruct((L, B, H, HD, S), F32)
    aug_spec = pl.BlockSpec((None, H, T, LANES), lambda b, t: (b, 0, t, 0))
    augt_spec = pl.BlockSpec((None, H, FLASH_VROWS, T), lambda b, t: (b, 0, 0, t))
    kv_spec = pl.BlockSpec((None, None, H, HD, T), lambda b, t: (l, b, 0, 0, t))
    in_specs = [pl.BlockSpec((None, T, 3 * H * HD), lambda b, t: (b, t, col)),
                pl.BlockSpec((None, T, LANES), lambda b, t: (b, t, 0)),
                pl.BlockSpec((1, LANES), lambda b, t: (0, 0)),
                pl.BlockSpec(memory_space=pl.ANY), pl.BlockSpec(memory_space=pl.ANY)]
    args = [proj, small, fb_row, k_prev, v_prev]
    aliases = {3: 3, 4: 4}
    kernel = lambda a, b, c, _k, _v, *rest: _foxprep_kernel(a, b, c, *rest)
    return pl.pallas_call(
        kernel,
        grid=(B, S // T),
        in_specs=in_specs,
        out_specs=[aug_spec, aug_spec, augt_spec, kv_spec, kv_spec,
                   pl.BlockSpec((None, T, LANES), lambda b, t: (b, t, 0))],
        out_shape=[aug, aug, augt, kv, kv, jax.ShapeDtypeStruct((B, S, LANES), F32)],
        scratch_shapes=[pltpu.VMEM((8, LANES), F32)],
        input_output_aliases=aliases,
        compiler_params=_cparams(("parallel", "arbitrary")),
        name="foxprep",
    )(*args)


FLASH_VROWS = 80
FLASH_QS = 256
FLASH_HEADS = 2
FLASH_AHEAD1 = 6
FLASH_S_SLOTS = 6
FLASH_P_SLOTS = 2


def _flash_kernel(qi_ref, kj_ref, qa_ref, ka_ref, vt_ref, z_ref, o_ref,
                  s_scr, p_scr, m_scr, acc_scr):
    TQ = qa_ref.shape[1]
    TK = ka_ref.shape[1]
    QS = FLASH_QS
    HD = LANES // 2
    nqs = TQ // QS
    HPS = qa_ref.shape[0]
    st = pl.program_id(2)
    i = qi_ref[st]
    j = kj_ref[st]
    nt = (((1,), (1,)), ((), ()))

    @pl.when(j == 0)
    def _():
        m_scr[...] = jnp.full(m_scr.shape, -jnp.inf, F32)
        acc_scr[...] = jnp.zeros_like(acc_scr)

    def unit(n):
        return n // HPS, n % HPS

    def scores(n, nk, diag):
        qs, hh = unit(n)
        slot = n % s_scr.shape[0]
        q = qa_ref[hh, qs * QS:(qs + 1) * QS, :]
        s = lax.dot_general(ka_ref[hh, 0:nk, :], q, nt, preferred_element_type=F32)
        if diag:
            keep = _iota2((QS, QS), 0) <= _iota2((QS, QS), 1)
            last = jnp.where(keep, s[nk - QS:nk, :], -jnp.inf)
            s_scr[slot, nk - QS:nk, :] = last
            m_cur = jnp.max(last, axis=0, keepdims=True)
            if nk > QS:
                s_scr[slot, 0:nk - QS, :] = s[0:nk - QS, :]
                m_cur = jnp.maximum(m_cur, jnp.max(s[0:nk - QS, :], axis=0, keepdims=True))
            return m_cur
        s_scr[slot, 0:nk, :] = s
        return jnp.max(s, axis=0, keepdims=True)

    def softmax(n, nk, m_cur):
        qs, hh = unit(n)
        m_prev = m_scr[hh, qs, 0:1, :]
        m_new = jnp.maximum(m_prev, m_cur)
        p_scr[n % p_scr.shape[0], 0:nk, :] = jnp.exp2(
            s_scr[n % s_scr.shape[0], 0:nk, :] - m_new).astype(BF16)
        m_scr[hh, qs, 0:1, :] = m_new
        return jnp.exp2(m_prev - m_new)

    def pv(n, nk, alpha):
        qs, hh = unit(n)
        o = jnp.dot(vt_ref[hh, :, 0:nk], p_scr[n % p_scr.shape[0], 0:nk, :],
                    preferred_element_type=F32)
        acc_scr[hh, qs] = acc_scr[hh, qs] * alpha + o

    ahead = FLASH_AHEAD1 - 1
    nu = HPS * nqs

    @pl.when(j < i)
    def _():
        m_curs = [scores(n, TK, False) for n in range(ahead)]
        for n in range(nu):
            alpha = softmax(n, TK, m_curs[n])
            if n + ahead < nu:
                m_curs.append(scores(n + ahead, TK, False))
            pv(n, TK, alpha)

    @pl.when(j == i)
    def _():
        nks = [(n // HPS + 1) * QS for n in range(nu)]
        m_curs = [scores(n, nks[n], True) for n in range(ahead)]
        for n in range(nu):
            alpha = softmax(n, nks[n], m_curs[n])
            if n + ahead < nu:
                m_curs.append(scores(n + ahead, nks[n + ahead], True))
            pv(n, nks[n], alpha)
        for qs in range(nqs):
            for pr in range(HPS // 2):
                a0 = acc_scr[2 * pr, qs]
                a1 = acc_scr[2 * pr + 1, qs]
                o = jnp.concatenate([a0[0:HD, :] / a0[HD:HD + 1, :],
                                     a1[0:HD, :] / a1[HD:HD + 1, :]], axis=0).T
                rows, cols = slice(qs * QS, (qs + 1) * QS), slice(pr * LANES, (pr + 1) * LANES)
                z = z_ref[rows, cols].astype(F32)
                o_ref[rows, cols] = (o * _silu(z)).astype(o_ref.dtype)


def _flash(qa, ka, vt, proj, zcol, tq):
    B, H, S, _ = qa.shape
    n = S // tq
    steps = [(i, j) for i in range(n) for j in range(i + 1)]
    qi = jnp.asarray([s[0] for s in steps], jnp.int32)
    kj = jnp.asarray([s[1] for s in steps], jnp.int32)
    nqs = tq // FLASH_QS
    hps = FLASH_HEADS
    ow = hps * LANES // 2
    grid_spec = pltpu.PrefetchScalarGridSpec(
        num_scalar_prefetch=2,
        grid=(B, H // hps, len(steps)),
        in_specs=[pl.BlockSpec((None, hps, tq, LANES), lambda b, p, s, qi, kj: (b, p, qi[s], 0)),
                  pl.BlockSpec((None, hps, tq, LANES), lambda b, p, s, qi, kj: (b, p, kj[s], 0)),
                  pl.BlockSpec((None, hps, FLASH_VROWS, tq), lambda b, p, s, qi, kj: (b, p, 0, kj[s])),
                  pl.BlockSpec((None, tq, ow), lambda b, p, s, qi, kj: (b, qi[s], zcol // (ow // LANES) + p))],
        out_specs=pl.BlockSpec((None, tq, ow), lambda b, p, s, qi, kj: (b, qi[s], p)),
        scratch_shapes=[pltpu.VMEM((FLASH_S_SLOTS, tq, FLASH_QS), F32),
                        pltpu.VMEM((FLASH_P_SLOTS, tq, FLASH_QS), BF16),
                        pltpu.VMEM((hps, nqs, 8, FLASH_QS), F32),
                        pltpu.VMEM((hps, nqs, FLASH_VROWS, FLASH_QS), F32)])
    return pl.pallas_call(
        _flash_kernel,
        grid_spec=grid_spec,
        out_shape=jax.ShapeDtypeStruct((B, S, H * LANES // 2), BF16),
        compiler_params=_cparams(("parallel", "parallel", "arbitrary")),
        name="flash",
    )(qi, kj, qa, ka, vt, proj)


def _cumsum_kernel(x_ref, o_ref, *, group):
    R = x_ref.shape[0]
    x = x_ref[...]
    up = jnp.where(_iota2((LANES, LANES), 0) <= _iota2((LANES, LANES), 1), 1.0, 0.0).astype(BF16)
    w = _dot_sel_r(x, up)
    tot = jnp.broadcast_to(w[:, LANES - 1:LANES], (R, LANES))
    ri = _iota2((R, R), 0)
    ci = _iota2((R, R), 1)
    prev = jnp.where((ci < ri) & ((ci // group) == (ri // group)), 1.0, 0.0).astype(BF16)
    o_ref[...] = w + _dot_sel_l(prev, tot)


def _cumsum_rows(x, group):
    R = x.shape[0]
    rb = min(R, 256)
    return pl.pallas_call(
        functools.partial(_cumsum_kernel, group=group),
        grid=(R // rb,),
        in_specs=[pl.BlockSpec((rb, LANES), lambda r: (r, 0))],
        out_specs=pl.BlockSpec((rb, LANES), lambda r: (r, 0)),
        out_shape=jax.ShapeDtypeStruct((R, LANES), F32),
        compiler_params=_cparams(("parallel",)),
        name="cumsum",
    )(x)


def _foxsamp_kernel(q_ref, k_ref, v_ref, z_ref, sm_ref, fb_ref, tot_ref, fcc_ref, ck_ref, cv_ref,
                    o_ref, ko_ref, vo_ref, lf_ref):
    S = q_ref.shape[0]
    HD = ck_ref.shape[1]
    p = pl.program_id(1)
    logf = _logf_from_small(sm_ref[...], fb_ref[...])
    lf_ref[...] = logf
    ltri = jnp.where(_iota2((S, S), 0) >= _iota2((S, S), 1), 1.0, 0.0).astype(BF16)
    fc = _dot_sel_l(ltri, logf) + tot_ref[...]
    fct = jnp.concatenate([fc, jnp.zeros((LANES - S, LANES), F32)], axis=0).T
    lane = _iota2((S, LANES), 1)
    row = _iota2((LANES, LANES), 0)
    causal = _iota2((S, S), 0) >= _iota2((S, S), 1)
    scale = HD ** -0.5
    for hh in range(2):
        h = 2 * p + hh
        fq = jnp.sum(jnp.where(lane == h, fc, 0.0), axis=1, keepdims=True)
        fkn = jnp.sum(jnp.where(row == h, fct, 0.0), axis=0, keepdims=True)[:, :S]
        q = q_ref[:, hh * HD:(hh + 1) * HD].astype(F32) * scale
        kn = k_ref[:, hh * HD:(hh + 1) * HD].astype(F32)
        vn = v_ref[:, hh * HD:(hh + 1) * HD].astype(F32)
        ko_ref[hh] = kn
        vo_ref[hh] = vn
        sc = _dot(q, ck_ref[hh]) + fq - fcc_ref[hh]
        sn = jnp.where(causal, _dot_nt(q, kn) + fq - fkn, -jnp.inf)
        m = jnp.maximum(jnp.max(sc, axis=1, keepdims=True), jnp.max(sn, axis=1, keepdims=True))
        pc = jnp.exp(sc - m)
        pn = jnp.exp(sn - m)
        l = jnp.sum(pc, axis=1, keepdims=True) + jnp.sum(pn, axis=1, keepdims=True)
        o = (_dot_nt(pc, cv_ref[hh]) + _dot(pn, vn)) / l
        o_ref[:, hh * HD:(hh + 1) * HD] = o * _silu(z_ref[:, hh * HD:(hh + 1) * HD].astype(F32))


def _foxsamp(proj, small, cols, fb_row, tot, fcc, cache_k, cache_v, l):
    B, S, _ = proj.shape
    _, _, H, HD, P = cache_k.shape
    cq, ck, cv, cz = cols
    pair = lambda c: pl.BlockSpec((None, S, LANES), lambda b, p: (b, 0, c + p))
    kv_out = jax.ShapeDtypeStruct((B, H, S, HD), F32)
    return pl.pallas_call(
        _foxsamp_kernel,
        grid=(B, H // 2),
        in_specs=[pair(cq), pair(ck), pair(cv), pair(cz),
                  pl.BlockSpec((None, S, LANES), lambda b, p: (b, 0, 0)),
                  pl.BlockSpec((1, LANES), lambda b, p: (0, 0)),
                  pl.BlockSpec((None, 1, LANES), lambda b, p: (b, 0, 0)),
                  pl.BlockSpec((None, 2, 1, P), lambda b, p: (b, p, 0, 0)),
                  pl.BlockSpec((None, None, 2, HD, P), lambda b, p: (l, b, p, 0, 0)),
                  pl.BlockSpec((None, None, 2, HD, P), lambda b, p: (l, b, p, 0, 0))],
        out_specs=[pl.BlockSpec((None, S, LANES), lambda b, p: (b, 0, p)),
                   pl.BlockSpec((None, 2, S, HD), lambda b, p: (b, p, 0, 0)),
                   pl.BlockSpec((None, 2, S, HD), lambda b, p: (b, p, 0, 0)),
                   pl.BlockSpec((None, S, LANES), lambda b, p: (b, 0, 0))],
        out_shape=[jax.ShapeDtypeStruct((B, S, H * HD), F32), kv_out, kv_out,
                   jax.ShapeDtypeStruct((B, S, LANES), F32)],
        compiler_params=_cparams(("parallel", "arbitrary")),
        name="foxsamp",
    )(proj, proj, proj, proj, small, fb_row, tot, fcc, cache_k, cache_v)


COL_MG = 0
COL_GQKV = 6
COL_FQKV = 9
COL_RQK = 12
COL_RV = 13
COL_RZ = 14
COL_FZ = 15
COL_GZ = 16


def _prep_w_in(w_in, D):
    BW = D // 2
    hk = BW // 2
    splits = (hk, hk, BW, BW, BW, BW, BW, H_FOX, BW, BW, BW, BW, H_GDN, H_GDN, BW, N_BRANCH * D)
    offs = np.cumsum((0,) + splits)
    seg = lambda i: w_in[..., offs[i]:offs[i + 1]]
    rq, rk, rv, rz, fq, fk, fv, ff, fz, gq, gk, gv, ga, gb, gz, mg = [seg(i) for i in range(16)]
    main = jnp.concatenate([mg, gq, gk, gv, fq, fk, fv, rq, rk, rv, rz, fz, gz], axis=-1)
    pad = jnp.zeros(w_in.shape[:-1] + (LANES - H_FOX - 2 * H_GDN,), w_in.dtype)
    small = jnp.concatenate([ff, ga, gb, pad], axis=-1)
    return main.astype(BF16), small.astype(BF16)


def _lane_row(vals, off):
    L, n = vals.shape
    return jnp.zeros((L, 1, LANES), F32).at[:, 0, off:off + n].set(vals)


def kernel(x_prompt, x_sample, c_prompt, c_sample, cache_fox_k, cache_fox_v, cache_fox_logf,
           state_ret, state_gdn, state_gdn_conv, norm_w, ada_w, ada_b, w_in, fox_f_bias,
           gdn_a_log, gdn_dt_bias, gdn_conv_w, ret_norm_w, gdn_norm_w, w_branch, w_out,
           final_norm_w):
    B, S, D = x_prompt.shape
    BS, SS, _ = x_sample.shape
    L = ada_w.shape[0]
    P = cache_fox_k.shape[3]
    DK_RET, DV_RET = state_ret.shape[-2:]
    DK_GDN, DV_GDN = state_gdn.shape[-2:]

    rows = -(-(B + BS) // 8) * 8
    c_all = jnp.concatenate([c_prompt, c_sample, jnp.zeros((rows - B - BS, D), F32)], axis=0)
    mod = _adaln(c_all, ada_w, ada_b)
    mod_p = mod[:, :, :B].reshape(L, 3, B, 1, D)
    mod_s = mod[:, :, B:B + BS].reshape(L, 3, BS, 1, D)

    w_main, w_small = _prep_w_in(w_in, D)
    w_branch_b = w_branch.astype(BF16)
    w_out_b = w_out.astype(BF16)
    norm_w3 = norm_w.reshape(L, 1, D)
    final_w = final_norm_w.reshape(1, D)
    ret_nw = ret_norm_w.reshape(L, H_RET, 1, DV_RET)
    gdn_nw = gdn_norm_w.reshape(L, 1, DV_GDN)
    fb_rows = _lane_row(fox_f_bias, 0)
    a_rows = _lane_row(gdn_a_log, H_FOX)
    dt_rows = _lane_row(gdn_dt_bias, H_FOX)

    tm_p = min(S, 2048)
    tmm_p = min(S, 512)
    t_ret = min(S, 256)
    t_gdn = min(S, 512)
    t_prep = min(S, 256)
    t_flash = min(S, 1024)

    tabs_p = _ret_tables(t_ret, 0, S, DK_RET)
    tabs_s = _ret_tables(SS, P, SS, DK_RET)

    fcc = _cumsum_rows(cache_fox_logf.reshape(-1, LANES), P // LANES).reshape(L, BS, H_FOX, 1, P)
    tot = jnp.zeros((L, BS, 1, LANES), F32).at[:, :, 0, :H_FOX].set(fcc[:, :, :, 0, P - 1])

    zeros_ret = jnp.zeros((B, H_RET, DK_RET, DV_RET), F32)
    zeros_gdn = jnp.zeros((B, H_GDN, DK_GDN, DV_GDN), F32)
    zeros_conv = jnp.zeros((B, CONV_W - 1, gdn_conv_w.shape[-1]), F32)

    xp, xs = x_prompt, x_sample
    cache_kt = jnp.swapaxes(cache_fox_k, 3, 4)
    cache_vt = jnp.swapaxes(cache_fox_v, 3, 4)
    pk = jnp.zeros((L, B, H_FOX, cache_fox_k.shape[-1], S), F32)
    pv = jnp.zeros_like(pk)
    p_lf, p_ret, p_gdn, p_conv = [], [], [], []
    s_k, s_v, s_lf, s_ret, s_gdn, s_conv = [], [], [], [], [], []
    for l in range(L):
        final = l == L - 1
        proj, small = _inproj(xp, mod_p, l, norm_w3, w_main, w_small, 1, tm_p)
        o_ret, st = _ret(proj, (COL_RQK, COL_RV, COL_RZ), tabs_p, ret_nw[l], zeros_ret, t_ret, BF16)
        p_ret.append(st)
        o_gdn, st, cv = _gdn(proj, small, (COL_GQKV // 3, COL_GZ), gdn_conv_w[l], zeros_conv,
                             a_rows[l], dt_rows[l], gdn_nw[l], zeros_gdn, t_gdn, BF16)
        p_gdn.append(st)
        p_conv.append(cv)
        qa, ka, va, pk, pv, lf = _foxprep(proj, small, COL_FQKV // 3, fb_rows[l], l, pk, pv, L, t_prep)
        p_lf.append(lf)
        o_fox = _flash(qa, ka, va, proj, COL_FZ * 4, t_flash)
        xp = _merge(xp, mod_p, l, o_ret, o_fox, o_gdn, proj, w_branch_b, w_out_b, final_w,
                    1, tmm_p, final)
        proj, small = _inproj(xs, mod_s, l, norm_w3, w_main, w_small, BS, SS)
        o_ret, st = _ret(proj, (COL_RQK, COL_RV, COL_RZ), tabs_s, ret_nw[l], state_ret[l], SS, F32)
        s_ret.append(st)
        o_gdn, st, cv = _gdn(proj, small, (COL_GQKV // 3, COL_GZ), gdn_conv_w[l], state_gdn_conv[l],
                             a_rows[l], dt_rows[l], gdn_nw[l], state_gdn[l], SS, F32)
        s_gdn.append(st)
        s_conv.append(cv)
        o_fox, kk, vv, lf = _foxsamp(proj, small,
                                     (COL_FQKV * 4, (COL_FQKV + 1) * 4, (COL_FQKV + 2) * 4, COL_FZ * 4),
                                     fb_rows[l], tot[l], fcc[l], cache_kt, cache_vt, l)
        s_k.append(kk)
        s_v.append(vv)
        s_lf.append(lf)
        xs = _merge(xs, mod_s, l, o_ret, o_fox, o_gdn, proj, w_branch_b, w_out_b, final_w,
                    BS, SS, final)

    logf_out = lambda lfs: jnp.stack(lfs)[..., :H_FOX].transpose(0, 1, 3, 2)
    return (xp, xs,
            jnp.swapaxes(pk, 3, 4), jnp.swapaxes(pv, 3, 4), logf_out(p_lf), jnp.stack(p_ret), jnp.stack(p_gdn), jnp.stack(p_conv),
            jnp.stack(s_k), jnp.stack(s_v), logf_out(s_lf), jnp.stack(s_ret), jnp.stack(s_gdn),
            jnp.stack(s_conv))
```

```python
import functools

import numpy as np
import jax
import jax.numpy as jnp
from jax import lax
from jax.experimental import pallas as pl
from jax.experimental.pallas import tpu as pltpu

F32 = jnp.float32
BF16 = jnp.bfloat16

N_BRANCH = 3
H_RET = 4
H_FOX = 8
H_GDN = 4
CONV_W = 4
CHUNK = 64
NORM_EPS = 1e-6
ROPE_BASE = 10000.0
LOG2E = 1.4426950408889634
LANES = 128
GDN_CHUNK = 128
VMEM_LIMIT = 56 * 1024 * 1024


def _cparams(sem):
    return pltpu.CompilerParams(dimension_semantics=sem, vmem_limit_bytes=VMEM_LIMIT)


def _sigmoid(x):
    return 0.5 * jnp.tanh(0.5 * x) + 0.5


def _silu(x):
    return x * _sigmoid(x)


def _softplus(x):
    return jnp.maximum(x, 0.0) + jnp.log(1.0 + jnp.exp(-jnp.abs(x)))


def _dot(a, b):
    return jnp.dot(a.astype(BF16), b.astype(BF16), preferred_element_type=F32)


def _dot_nt(a, b):
    return lax.dot_general(a.astype(BF16), b.astype(BF16), (((1,), (1,)), ((), ())),
                           preferred_element_type=F32)


def _split3(x):
    hi = x.astype(BF16)
    r = x - hi.astype(F32)
    mid = r.astype(BF16)
    lo = (r - mid.astype(F32)).astype(BF16)
    return hi, mid, lo


def _dot_sel_l(m, x):
    hi, mid, lo = _split3(x)
    d = lambda y: jnp.dot(m, y, preferred_element_type=F32)
    return d(hi) + (d(mid) + d(lo))


def _dot_sel_r(x, m):
    hi, mid, lo = _split3(x)
    d = lambda y: jnp.dot(y, m, preferred_element_type=F32)
    return d(hi) + (d(mid) + d(lo))


def _iota2(shape, dim):
    return lax.broadcasted_iota(jnp.int32, shape, dim)


def _adaln_kernel(c_ref, w_ref, b_ref, o_ref):
    s = _silu(c_ref[...])
    o_ref[...] = _dot(s, w_ref[...]) + b_ref[...]


def _adaln(c_all, ada_w, ada_b):
    L, D, _ = ada_w.shape
    R = c_all.shape[0]
    return pl.pallas_call(
        _adaln_kernel,
        grid=(L, 3),
        in_specs=[pl.BlockSpec((R, D), lambda l, k: (0, 0)),
                  pl.BlockSpec((None, D, D), lambda l, k: (l, 0, k)),
                  pl.BlockSpec((None, None, 1, D), lambda l, k: (l, k, 0, 0))],
        out_specs=pl.BlockSpec((None, None, R, D), lambda l, k: (l, k, 0, 0)),
        out_shape=jax.ShapeDtypeStruct((L, 3, R, D), F32),
        compiler_params=_cparams(("parallel", "parallel")),
        name="adaln",
    )(c_all, ada_w, ada_b.reshape(L, 3, 1, D))


PROJ_TN = 512
PROJ_DTYPE = BF16


def _inproj_kernel(x_ref, nw_ref, sc_ref, sh_ref, w_ref, ws_ref, o_ref, os_ref, h_ref):
    bb, tm, D = x_ref.shape

    @pl.when(pl.program_id(2) == 0)
    def _():
        x = x_ref[...]
        ms = jnp.mean(x * x, axis=-1, keepdims=True)
        xn = (x * lax.rsqrt(ms + NORM_EPS)) * nw_ref[...]
        h = xn * (1.0 + sc_ref[...]) + sh_ref[...]
        hb = h.astype(BF16).reshape(bb * tm, D)
        h_ref[...] = hb
        os_ref[...] = jnp.dot(hb, ws_ref[...], preferred_element_type=F32).reshape(bb, tm, LANES)

    o_ref[...] = jnp.dot(h_ref[...], w_ref[...],
                         preferred_element_type=F32).reshape(bb, tm, PROJ_TN).astype(o_ref.dtype)


def _inproj(x, mod, l, norm_w, w_main, w_small, bb, tm):
    B, S, D = x.shape
    NP = w_main.shape[-1]
    nj = NP // PROJ_TN
    return pl.pallas_call(
        _inproj_kernel,
        grid=(B // bb, S // tm, nj),
        in_specs=[pl.BlockSpec((bb, tm, D), lambda b, i, j: (b, i, 0)),
                  pl.BlockSpec((None, 1, D), lambda b, i, j: (l, 0, 0)),
                  pl.BlockSpec((None, None, bb, 1, D), lambda b, i, j: (l, 1, b, 0, 0)),
                  pl.BlockSpec((None, None, bb, 1, D), lambda b, i, j: (l, 0, b, 0, 0)),
                  pl.BlockSpec((None, D, PROJ_TN), lambda b, i, j: (l, 0, j)),
                  pl.BlockSpec((None, D, LANES), lambda b, i, j: (l, 0, 0))],
        out_specs=[pl.BlockSpec((bb, tm, PROJ_TN), lambda b, i, j: (b, i, j)),
                   pl.BlockSpec((bb, tm, LANES), lambda b, i, j: (b, i, 0))],
        out_shape=[jax.ShapeDtypeStruct((B, S, NP), PROJ_DTYPE),
                   jax.ShapeDtypeStruct((B, S, LANES), F32)],
        scratch_shapes=[pltpu.VMEM((bb * tm, D), BF16)],
        compiler_params=_cparams(("parallel", "parallel", "arbitrary")),
        name="inproj",
    )(x, norm_w, mod, mod, w_main, w_small)


def _merge_kernel(x_ref, g_ref, o1_ref, o2_ref, o3_ref, mg_ref, wb_ref, wo_ref, fw_ref, y_ref,
                  *, final):
    bb, tm, D = x_ref.shape
    M = bb * tm
    acc = None
    for n, o_ref in enumerate((o1_ref, o2_ref, o3_ref)):
        br = jnp.dot(o_ref[...].reshape(M, o_ref.shape[-1]).astype(BF16), wb_ref[n],
                     preferred_element_type=F32)
        gate = _sigmoid(mg_ref[:, :, n * D:(n + 1) * D].astype(F32).reshape(M, D))
        term = br * gate
        acc = term if acc is None else acc + term
    out = jnp.dot(acc.astype(BF16), wo_ref[...], preferred_element_type=F32)
    xn = x_ref[...] + g_ref[...] * out.reshape(bb, tm, D)
    if final:
        ms = jnp.mean(xn * xn, axis=-1, keepdims=True)
        xn = (xn * lax.rsqrt(ms + NORM_EPS)) * fw_ref[...]
    y_ref[...] = xn


def _merge(x, mod, l, o_ret, o_fox, o_gdn, proj, w_branch, w_out, final_w, bb, tm, final):
    B, S, D = x.shape
    W = o_ret.shape[-1]
    return pl.pallas_call(
        functools.partial(_merge_kernel, final=final),
        grid=(B // bb, S // tm),
        in_specs=[pl.BlockSpec((bb, tm, D), lambda b, i: (b, i, 0)),
                  pl.BlockSpec((None, None, bb, 1, D), lambda b, i: (l, 2, b, 0, 0)),
                  pl.BlockSpec((bb, tm, W), lambda b, i: (b, i, 0)),
                  pl.BlockSpec((bb, tm, W), lambda b, i: (b, i, 0)),
                  pl.BlockSpec((bb, tm, W), lambda b, i: (b, i, 0)),
                  pl.BlockSpec((bb, tm, N_BRANCH * D), lambda b, i: (b, i, 0)),
                  pl.BlockSpec((None, N_BRANCH, W, D), lambda b, i: (l, 0, 0, 0)),
                  pl.BlockSpec((None, D, D), lambda b, i: (l, 0, 0)),
                  pl.BlockSpec((1, D), lambda b, i: (0, 0))],
        out_specs=pl.BlockSpec((bb, tm, D), lambda b, i: (b, i, 0)),
        out_shape=jax.ShapeDtypeStruct((B, S, D), F32),
        compiler_params=_cparams(("parallel", "parallel")),
        name="merge",
    )(x, mod, o_ret, o_fox, o_gdn, proj, w_branch, w_out, final_w)


def _ret_kernel(qk_ref, v_ref, z_ref, cos_ref, sin_ref, m_ref, dq_ref, dk_ref, dc_ref, nw_ref,
                s0_ref, o_ref, so_ref, s_ref):
    T = qk_ref.shape[0]
    H = s_ref.shape[0]
    DK = s0_ref.shape[1]
    t = pl.program_id(1)

    @pl.when(t == 0)
    def _():
        zero = jnp.zeros((DK, LANES), F32)
        for h in range(H):
            s0 = s0_ref[h]
            s_ref[h] = jnp.concatenate([s0, zero] if h % 2 == 0 else [zero, s0], axis=0)

    cos = cos_ref[...]
    sin = sin_ref[...]
    lane = _iota2((T, LANES), 1)
    first = (lane % DK) < (DK // 2)
    low = lane < DK

    def rot(x):
        sw = jnp.where(first, pltpu.roll(x, LANES - DK // 2, 1), pltpu.roll(x, DK // 2, 1))
        return x * cos + sw * sin

    nkq = H * DK
    hs = range(H)
    qcs = [rot(qk_ref[:, p * LANES:(p + 1) * LANES].astype(F32)) for p in range(H // 2)]
    kcs = [rot(qk_ref[:, nkq + p * LANES:nkq + (p + 1) * LANES].astype(F32)) * (DK ** -0.5)
           for p in range(H // 2)]
    kcbs = [kc.astype(BF16) for kc in kcs]
    sels = [low if h % 2 == 0 else jnp.logical_not(low) for h in hs]
    qms = [jnp.where(sels[h], qcs[h // 2], 0.0) for h in hs]
    vhs = [v_ref[:, h * LANES:(h + 1) * LANES].astype(BF16) for h in hs]
    ss = [s_ref[h] for h in hs]
    ams = [_dot_nt(qms[h], kcbs[h // 2]) * m_ref[h] for h in hs]
    kts = [(jnp.where(sels[h], kcs[h // 2], 0.0) * dk_ref[h]).T for h in hs]
    os_ = [_dot(ams[h], vhs[h]) + _dot(qms[h] * dq_ref[h], ss[h]) for h in hs]
    for h in hs:
        s_ref[h] = ss[h] * dc_ref[h] + _dot(kts[h], vhs[h])
    for h in hs:
        o = os_[h]
        ms = jnp.mean(o * o, axis=-1, keepdims=True)
        on = (o * lax.rsqrt(ms + NORM_EPS)) * nw_ref[h]
        z = z_ref[:, h * LANES:(h + 1) * LANES].astype(F32)
        o_ref[:, h * LANES:(h + 1) * LANES] = (on * _silu(z)).astype(o_ref.dtype)

    @pl.when(t == pl.num_programs(1) - 1)
    def _():
        for h in range(H):
            so_ref[h] = s_ref[h][(h % 2) * DK:(h % 2 + 1) * DK, :]


def _ret_tables(T, pos0, S, DK):
    half = DK // 2
    inv_freq = ROPE_BASE ** (-jnp.arange(half, dtype=F32) / half)
    pos = pos0 + jnp.arange(S)
    ang = pos.astype(F32)[:, None] * inv_freq[None, :]
    cos, sin = jnp.cos(ang), jnp.sin(ang)
    reps = LANES // DK
    cos_t = jnp.tile(jnp.concatenate([cos, cos], axis=1), (1, reps))
    sin_t = jnp.tile(jnp.concatenate([-sin, sin], axis=1), (1, reps))
    log_g = jnp.log1p(-jnp.exp2(-5.0 - jnp.arange(H_RET, dtype=F32)))
    i = jnp.arange(T)
    dist = jnp.abs(i[:, None] - i[None, :]).astype(F32)
    allowed = (i[None, :] // CHUNK) <= (i[:, None] // CHUNK)
    m = jnp.where(allowed[None], jnp.exp(dist[None] * log_g[:, None, None]), 0.0)
    fi = i.astype(F32)
    dq = jnp.exp((fi[None, :] + 1.0) * log_g[:, None])
    dk = jnp.exp((T - 1.0 - fi)[None, :] * log_g[:, None])
    dc = jnp.exp(T * log_g)
    bc = lambda a: jnp.broadcast_to(a[..., None], a.shape + (LANES,))
    return cos_t, sin_t, m, bc(dq), bc(dk), bc(dc[:, None])


def _ret(proj, cols, tabs, ret_nw, s0, T, out_dtype):
    B, S, _ = proj.shape
    H, DK, DV = s0.shape[1:]
    W = H * DV
    cqk, cv, cz = cols
    cos_t, sin_t, m, dq, dk, dc = tabs
    return pl.pallas_call(
        _ret_kernel,
        grid=(B, S // T),
        in_specs=[pl.BlockSpec((None, T, W), lambda b, t: (b, t, cqk)),
                  pl.BlockSpec((None, T, W), lambda b, t: (b, t, cv)),
                  pl.BlockSpec((None, T, W), lambda b, t: (b, t, cz)),
                  pl.BlockSpec((T, LANES), lambda b, t: (t, 0)),
                  pl.BlockSpec((T, LANES), lambda b, t: (t, 0)),
                  pl.BlockSpec((H, T, T), lambda b, t: (0, 0, 0)),
                  pl.BlockSpec((H, T, LANES), lambda b, t: (0, 0, 0)),
                  pl.BlockSpec((H, T, LANES), lambda b, t: (0, 0, 0)),
                  pl.BlockSpec((H, 1, LANES), lambda b, t: (0, 0, 0)),
                  pl.BlockSpec((H, 1, DV), lambda b, t: (0, 0, 0)),
                  pl.BlockSpec((None, H, DK, DV), lambda b, t: (b, 0, 0, 0))],
        out_specs=[pl.BlockSpec((None, T, W), lambda b, t: (b, t, 0)),
                   pl.BlockSpec((None, H, DK, DV), lambda b, t: (b, 0, 0, 0))],
        out_shape=[jax.ShapeDtypeStruct((B, S, W), out_dtype),
                   jax.ShapeDtypeStruct((B, H, DK, DV), F32)],
        scratch_shapes=[pltpu.VMEM((H, LANES, DV), F32)],
        compiler_params=_cparams(("parallel", "arbitrary")),
        name="ret",
    )(proj, proj, proj, cos_t, sin_t, m, dq, dk, dc, ret_nw, s0)


INV_BASE = 8


def _unit_lower_inverses(mats, eye, ri, ci):
    n = mats[0].shape[0]
    b = INV_BASE
    d = lambda x, y: jnp.dot(x, y, preferred_element_type=F32)
    same = (ri // b) == (ci // b)
    ads = [jnp.where(same, a, 0.0) for a in mats]
    xs = [eye - ad for ad in ads]
    ps = [ad.astype(BF16) for ad in ads]
    k = 2
    while k < b:
        ps = [d(p, p).astype(BF16) for p in ps]
        xs = [x + d(x.astype(BF16), p) for x, p in zip(xs, ps)]
        k *= 2
    while b < n:
        off = ((ri // (2 * b)) == (ci // (2 * b))) & ((ri // b) != (ci // b))
        xbs = [x.astype(BF16) for x in xs]
        ts = [d(jnp.where(off, a, 0.0).astype(BF16), xb) for a, xb in zip(mats, xbs)]
        xs = [x - d(xb, t.astype(BF16)) for x, xb, t in zip(xs, xbs, ts)]
        b *= 2
    return xs


def _gdn_kernel(u_ref, z_ref, sm_ref, cw_ref, cb_ref, al_ref, dt_ref, nw_ref, s0_ref,
                o_ref, so_ref, co_ref, ext_ref, s_ref):
    T, C3 = u_ref.shape
    H = s_ref.shape[0]
    DK = s_ref.shape[1]
    W = H * DK
    C = GDN_CHUNK
    TP = -(-T // C) * C
    t = pl.program_id(1)
    last = pl.num_programs(1) - 1
    base = 8
    lo = base - (CONV_W - 1)

    @pl.when(t == 0)
    def _():
        s_ref[...] = s0_ref[...]
        ext_ref[lo:base, :] = cb_ref[...]

    ext_ref[base:base + T, :] = u_ref[...].astype(F32)
    y = ext_ref[lo:lo + T, :] * cw_ref[0:1, :]
    for j in range(1, CONV_W):
        y = y + ext_ref[lo + j:lo + j + T, :] * cw_ref[j:j + 1, :]
    tail = ext_ref[lo + T:base + T, :]
    ext_ref[lo:base, :] = tail

    @pl.when(t == last)
    def _():
        co_ref[...] = tail

    qkv = _silu(y)
    sm = sm_ref[...]
    g_all = -jnp.exp(al_ref[...]) * _softplus(sm + dt_ref[...])
    b_all = _sigmoid(sm)
    z_all = z_ref[...].astype(F32)
    if TP > T:
        pad = lambda a: jnp.concatenate([a, jnp.zeros((TP - T, a.shape[1]), F32)], axis=0)
        qkv, g_all, b_all, z_all = pad(qkv), pad(g_all), pad(b_all), pad(z_all)

    ri = _iota2((C, C), 0)
    ci = _iota2((C, C), 1)
    tri = ri >= ci
    strict = ri > ci
    ltri = jnp.where(tri, 1.0, 0.0).astype(BF16)
    eye = jnp.where(ri == ci, 1.0, 0.0)

    nc = TP // C
    items = [(c, h) for c in range(nc) for h in range(H)]
    gcums = [_dot_sel_l(ltri, g_all[c * C:(c + 1) * C, :]) for c in range(nc)]

    qs, ks, vs, gcs, bcs, decs, kbs = [], [], [], [], [], [], []
    for c, h in items:
        r0 = c * C
        q = qkv[r0:r0 + C, h * DK:(h + 1) * DK]
        k = qkv[r0:r0 + C, W + h * DK:W + (h + 1) * DK]
        qs.append(q * lax.rsqrt(jnp.sum(q * q, axis=-1, keepdims=True) + NORM_EPS) * (DK ** -0.5))
        k = k * lax.rsqrt(jnp.sum(k * k, axis=-1, keepdims=True) + NORM_EPS)
        ks.append(k)
        kbs.append(k.astype(BF16))
        vs.append(qkv[r0:r0 + C, 2 * W + h * DK:2 * W + (h + 1) * DK])
        gc = gcums[c][:, 8 + h:9 + h]
        gcs.append(gc)
        bcs.append(b_all[r0:r0 + C, 12 + h:13 + h])
        gm = jnp.broadcast_to(gc, (C, C))
        decs.append(jnp.exp(jnp.where(tri, gm - gm.T, -jnp.inf)))
    kks = [_dot_nt(kb, kb) for kb in kbs]
    amats = [jnp.where(strict, kk * dec, 0.0) * bc for kk, dec, bc in zip(kks, decs, bcs)]
    tinvs = _unit_lower_inverses(amats, eye, ri, ci)
    egs = [jnp.exp(gc) for gc in gcs]
    rhss = [jnp.concatenate([k * (bc * eg), v * bc], axis=1)
            for k, v, bc, eg in zip(ks, vs, bcs, egs)]
    wus = [_dot(tinv, rhs) for tinv, rhs in zip(tinvs, rhss)]
    qks = [(_dot_nt(q, kb) * dec).astype(BF16) for q, kb, dec in zip(qs, kbs, decs)]
    qes = [(q * eg).astype(BF16) for q, eg in zip(qs, egs)]
    glasts = [gcums[c][C - 1:C, 8 + h:9 + h] for c, h in items]
    kdts = [(k * jnp.exp(gl - gc)).T.astype(BF16) for k, gl, gc in zip(ks, glasts, gcs)]
    sdec = [jnp.exp(gl) for gl in glasts]

    for c in range(nc):
        idx = [c * H + h for h in range(H)]
        ss = [s_ref[h] for h in range(H)]
        sbs = [s.astype(BF16) for s in ss]
        us = [wus[i][:, DK:] - _dot(wus[i][:, :DK], sb) for i, sb in zip(idx, sbs)]
        ubs = [u.astype(BF16) for u in us]
        os_ = [jnp.dot(qes[i], sb, preferred_element_type=F32)
               + jnp.dot(qks[i], ub, preferred_element_type=F32)
               for i, sb, ub in zip(idx, sbs, ubs)]
        for h, i in enumerate(idx):
            s_ref[h] = ss[h] * sdec[i] + jnp.dot(kdts[i], ubs[h], preferred_element_type=F32)
        for h, o in enumerate(os_):
            ms = jnp.mean(o * o, axis=-1, keepdims=True)
            on = (o * lax.rsqrt(ms + NORM_EPS)) * nw_ref[...]
            z = z_all[c * C:(c + 1) * C, h * DK:(h + 1) * DK]
            res = (on * _silu(z)).astype(o_ref.dtype)
            if TP > T:
                o_ref[:, h * DK:(h + 1) * DK] = res[:T, :]
            else:
                o_ref[c * C:(c + 1) * C, h * DK:(h + 1) * DK] = res

    @pl.when(t == last)
    def _():
        so_ref[...] = s_ref[...]


def _gdn(proj, small, cols, conv_w, conv_buf, a_row, dt_row, gdn_nw, s0, T, out_dtype):
    B, S, _ = proj.shape
    H, DK, DV = s0.shape[1:]
    W = H * DK
    C3 = conv_w.shape[-1]
    cu, cz = cols
    return pl.pallas_call(
        _gdn_kernel,
        grid=(B, S // T),
        in_specs=[pl.BlockSpec((None, T, C3), lambda b, t: (b, t, cu)),
                  pl.BlockSpec((None, T, W), lambda b, t: (b, t, cz)),
                  pl.BlockSpec((None, T, LANES), lambda b, t: (b, t, 0)),
                  pl.BlockSpec((CONV_W, C3), lambda b, t: (0, 0)),
                  pl.BlockSpec((None, CONV_W - 1, C3), lambda b, t: (b, 0, 0)),
                  pl.BlockSpec((1, LANES), lambda b, t: (0, 0)),
                  pl.BlockSpec((1, LANES), lambda b, t: (0, 0)),
                  pl.BlockSpec((1, DV), lambda b, t: (0, 0)),
                  pl.BlockSpec((None, H, DK, DV), lambda b, t: (b, 0, 0, 0))],
        out_specs=[pl.BlockSpec((None, T, W), lambda b, t: (b, t, 0)),
                   pl.BlockSpec((None, H, DK, DV), lambda b, t: (b, 0, 0, 0)),
                   pl.BlockSpec((None, CONV_W - 1, C3), lambda b, t: (b, 0, 0))],
        out_shape=[jax.ShapeDtypeStruct((B, S, W), out_dtype),
                   jax.ShapeDtypeStruct((B, H, DK, DV), F32),
                   jax.ShapeDtypeStruct((B, CONV_W - 1, C3), F32)],
        scratch_shapes=[pltpu.VMEM((T + 8, C3), F32),
                        pltpu.VMEM((H, DK, DV), F32)],
        compiler_params=_cparams(("parallel", "arbitrary")),
        name="gdn",
    )(proj, proj, small, conv_w, conv_buf, a_row, dt_row, gdn_nw, s0)


def _logf_from_small(sm, fb):
    lane = _iota2(sm.shape, 1)
    return jnp.where(lane < H_FOX, -_softplus(-(sm + fb)), 0.0)


def _foxprep_kernel(qkv_ref, sm_ref, fb_ref, qa_ref, ka_ref, vt_ref, ko_ref, vo_ref, lf_ref,
                    carry_ref):
    T = sm_ref.shape[0]
    H = qa_ref.shape[0]
    HD = ko_ref.shape[1]
    W = H * HD
    t = pl.program_id(1)

    @pl.when(t == 0)
    def _():
        carry_ref[...] = jnp.zeros_like(carry_ref)

    logf = _logf_from_small(sm_ref[...], fb_ref[...])
    lf_ref[...] = logf
    ltri = jnp.where(_iota2((T, T), 0) >= _iota2((T, T), 1), 1.0, 0.0).astype(BF16)
    fc = _dot_sel_l(ltri, logf) + carry_ref[0:1, :]
    carry_ref[0:1, :] = fc[T - 1:T, :]
    pieces = jnp.concatenate(_split3(fc * LOG2E), axis=1)
    r = _iota2((3 * LANES, 2 * LANES), 0)
    c = _iota2((3 * LANES, 2 * LANES), 1)
    rh, rj = r % LANES, r // LANES
    place = jnp.where((rh < H) & (c == HD + 3 * rh + rj), 1.0,
                      jnp.where((rh < H) & (c == LANES + HD + 3 * H + 3 * rh + rj), -1.0, 0.0))
    ebias = jnp.dot(pieces, place.astype(BF16), preferred_element_type=F32)
    e_q, e_k = ebias[:, :LANES], ebias[:, LANES:]

    lane = _iota2((T, LANES), 1)
    low = lane < HD
    ones_rows = jnp.where(_iota2((FLASH_VROWS - HD, T), 0) == 0, 1.0, 0.0)
    scale = (HD ** -0.5) * LOG2E
    for p in range(H // 2):
        qc = qkv_ref[:, p * LANES:(p + 1) * LANES].astype(F32) * scale
        kc = qkv_ref[:, W + p * LANES:W + (p + 1) * LANES].astype(F32)
        kct = kc.T
        vct = qkv_ref[:, 2 * W + p * LANES:2 * W + (p + 1) * LANES].astype(F32).T
        for half in range(2):
            h = 2 * p + half
            if half == 1:
                qh, kh = [pltpu.roll(a, HD, 1) for a in (qc, kc)]
            else:
                qh, kh = qc, kc
            vht = vct[half * HD:(half + 1) * HD, :]
            own_k = (lane >= HD + 3 * h) & (lane < HD + 3 * h + 3)
            own_q = (lane >= HD + 3 * H + 3 * h) & (lane < HD + 3 * H + 3 * h + 3)
            qa_ref[h] = jnp.where(low, qh, jnp.where(own_q, 1.0, e_q)).astype(BF16)
            ka_ref[h] = jnp.where(low, kh, jnp.where(own_k, 1.0, e_k)).astype(BF16)
            vt_ref[h] = jnp.concatenate([vht, ones_rows], axis=0).astype(BF16)
            ko_ref[h] = kct[half * HD:(half + 1) * HD, :]
            vo_ref[h] = vht


def _foxprep(proj, small, col, fb_row, l, k_prev, v_prev, L, T):
    B, S, _ = proj.shape
    H = H_FOX
    HD = 512 // H
    aug = jax.ShapeDtypeStruct((B, H, S, LANES), BF16)
    augt = jax.ShapeDtypeStruct((B, H, FLASH_VROWS, S), BF16)
    kv = jax.ShapeDtypeStruct((L, B, H, HD, S), F32)
    aug_spec = pl.BlockSpec((None, H, T, LANES), lambda b, t: (b, 0, t, 0))
    augt_spec = pl.BlockSpec((None, H, FLASH_VROWS, T), lambda b, t: (b, 0, 0, t))
    kv_spec = pl.BlockSpec((None, None, H, HD, T), lambda b, t: (l, b, 0, 0, t))
    in_specs = [pl.BlockSpec((None, T, 3 * H * HD), lambda b, t: (b, t, col)),
                pl.BlockSpec((None, T, LANES), lambda b, t: (b, t, 0)),
                pl.BlockSpec((1, LANES), lambda b, t: (0, 0)),
                pl.BlockSpec(memory_space=pl.ANY), pl.BlockSpec(memory_space=pl.ANY)]
    args = [proj, small, fb_row, k_prev, v_prev]
    aliases = {3: 3, 4: 4}
    kernel = lambda a, b, c, _k, _v, *rest: _foxprep_kernel(a, b, c, *rest)
    return pl.pallas_call(
        kernel,
        grid=(B, S // T),
        in_specs=in_specs,
        out_specs=[aug_spec, aug_spec, augt_spec, kv_spec, kv_spec,
                   pl.BlockSpec((None, T, LANES), lambda b, t: (b, t, 0))],
        out_shape=[aug, aug, augt, kv, kv, jax.ShapeDtypeStruct((B, S, LANES), F32)],
        scratch_shapes=[pltpu.VMEM((8, LANES), F32)],
        input_output_aliases=aliases,
        compiler_params=_cparams(("parallel", "arbitrary")),
        name="foxprep",
    )(*args)


FLASH_VROWS = 80
FLASH_QS = 256
FLASH_HEADS = 2
FLASH_AHEAD1 = 6
FLASH_S_SLOTS = 6
FLASH_P_SLOTS = 2


def _flash_kernel(qi_ref, kj_ref, qa_ref, ka_ref, vt_ref, z_ref, o_ref,
                  s_scr, p_scr, m_scr, acc_scr):
    TQ = qa_ref.shape[1]
    TK = ka_ref.shape[1]
    QS = FLASH_QS
    HD = LANES // 2
    nqs = TQ // QS
    HPS = qa_ref.shape[0]
    st = pl.program_id(2)
    i = qi_ref[st]
    j = kj_ref[st]
    nt = (((1,), (1,)), ((), ()))

    @pl.when(j == 0)
    def _():
        m_scr[...] = jnp.full(m_scr.shape, -jnp.inf, F32)
        acc_scr[...] = jnp.zeros_like(acc_scr)

    def unit(n):
        return n // HPS, n % HPS

    def scores(n, nk, diag):
        qs, hh = unit(n)
        slot = n % s_scr.shape[0]
        q = qa_ref[hh, qs * QS:(qs + 1) * QS, :]
        s = lax.dot_general(ka_ref[hh, 0:nk, :], q, nt, preferred_element_type=F32)
        if diag:
            keep = _iota2((QS, QS), 0) <= _iota2((QS, QS), 1)
            last = jnp.where(keep, s[nk - QS:nk, :], -jnp.inf)
            s_scr[slot, nk - QS:nk, :] = last
            m_cur = jnp.max(last, axis=0, keepdims=True)
            if nk > QS:
                s_scr[slot, 0:nk - QS, :] = s[0:nk - QS, :]
                m_cur = jnp.maximum(m_cur, jnp.max(s[0:nk - QS, :], axis=0, keepdims=True))
            return m_cur
        s_scr[slot, 0:nk, :] = s
        return jnp.max(s, axis=0, keepdims=True)

    def softmax(n, nk, m_cur):
        qs, hh = unit(n)
        m_prev = m_scr[hh, qs, 0:1, :]
        m_new = jnp.maximum(m_prev, m_cur)
        p_scr[n % p_scr.shape[0], 0:nk, :] = jnp.exp2(
            s_scr[n % s_scr.shape[0], 0:nk, :] - m_new).astype(BF16)
        m_scr[hh, qs, 0:1, :] = m_new
        return jnp.exp2(m_prev - m_new)

    def pv(n, nk, alpha):
        qs, hh = unit(n)
        o = jnp.dot(vt_ref[hh, :, 0:nk], p_scr[n % p_scr.shape[0], 0:nk, :],
                    preferred_element_type=F32)
        acc_scr[hh, qs] = acc_scr[hh, qs] * alpha + o

    ahead = FLASH_AHEAD1 - 1
    nu = HPS * nqs

    @pl.when(j < i)
    def _():
        m_curs = [scores(n, TK, False) for n in range(ahead)]
        for n in range(nu):
            alpha = softmax(n, TK, m_curs[n])
            if n + ahead < nu:
                m_curs.append(scores(n + ahead, TK, False))
            pv(n, TK, alpha)

    @pl.when(j == i)
    def _():
        nks = [(n // HPS + 1) * QS for n in range(nu)]
        m_curs = [scores(n, nks[n], True) for n in range(ahead)]
        for n in range(nu):
            alpha = softmax(n, nks[n], m_curs[n])
            if n + ahead < nu:
                m_curs.append(scores(n + ahead, nks[n + ahead], True))
            pv(n, nks[n], alpha)
        for qs in range(nqs):
            for pr in range(HPS // 2):
                a0 = acc_scr[2 * pr, qs]
                a1 = acc_scr[2 * pr + 1, qs]
                o = jnp.concatenate([a0[0:HD, :] / a0[HD:HD + 1, :],
                                     a1[0:HD, :] / a1[HD:HD + 1, :]], axis=0).T
                rows, cols = slice(qs * QS, (qs + 1) * QS), slice(pr * LANES, (pr + 1) * LANES)
                z = z_ref[rows, cols].astype(F32)
                o_ref[rows, cols] = (o * _silu(z)).astype(o_ref.dtype)


def _flash(qa, ka, vt, proj, zcol, tq):
    B, H, S, _ = qa.shape
    n = S // tq
    steps = [(i, j) for i in range(n) for j in range(i + 1)]
    qi = jnp.asarray([s[0] for s in steps], jnp.int32)
    kj = jnp.asarray([s[1] for s in steps], jnp.int32)
    nqs = tq // FLASH_QS
    hps = FLASH_HEADS
    ow = hps * LANES // 2
    grid_spec = pltpu.PrefetchScalarGridSpec(
        num_scalar_prefetch=2,
        grid=(B, H // hps, len(steps)),
        in_specs=[pl.BlockSpec((None, hps, tq, LANES), lambda b, p, s, qi, kj: (b, p, qi[s], 0)),
                  pl.BlockSpec((None, hps, tq, LANES), lambda b, p, s, qi, kj: (b, p, kj[s], 0)),
                  pl.BlockSpec((None, hps, FLASH_VROWS, tq), lambda b, p, s, qi, kj: (b, p, 0, kj[s])),
                  pl.BlockSpec((None, tq, ow), lambda b, p, s, qi, kj: (b, qi[s], zcol // (ow // LANES) + p))],
        out_specs=pl.BlockSpec((None, tq, ow), lambda b, p, s, qi, kj: (b, qi[s], p)),
        scratch_shapes=[pltpu.VMEM((FLASH_S_SLOTS, tq, FLASH_QS), F32),
                        pltpu.VMEM((FLASH_P_SLOTS, tq, FLASH_QS), BF16),
                        pltpu.VMEM((hps, nqs, 8, FLASH_QS), F32),
                        pltpu.VMEM((hps, nqs, FLASH_VROWS, FLASH_QS), F32)])
    return pl.pallas_call(
        _flash_kernel,
        grid_spec=grid_spec,
        out_shape=jax.ShapeDtypeStruct((B, S, H * LANES // 2), BF16),
        compiler_params=_cparams(("parallel", "parallel", "arbitrary")),
        name="flash",
    )(qi, kj, qa, ka, vt, proj)


FOXSAMP_HEADS = 4

def _cumsum_kernel(x_ref, o_ref, *, group):
    R = x_ref.shape[0]
    x = x_ref[...]
    up = jnp.where(_iota2((LANES, LANES), 0) <= _iota2((LANES, LANES), 1), 1.0, 0.0).astype(BF16)
    w = _dot_sel_r(x, up)
    tot = jnp.broadcast_to(w[:, LANES - 1:LANES], (R, LANES))
    ri = _iota2((R, R), 0)
    ci = _iota2((R, R), 1)
    prev = jnp.where((ci < ri) & ((ci // group) == (ri // group)), 1.0, 0.0).astype(BF16)
    o_ref[...] = w + _dot_sel_l(prev, tot)


def _cumsum_rows(x, group):
    R = x.shape[0]
    rb = min(R, 256)
    return pl.pallas_call(
        functools.partial(_cumsum_kernel, group=group),
        grid=(R // rb,),
        in_specs=[pl.BlockSpec((rb, LANES), lambda r: (r, 0))],
        out_specs=pl.BlockSpec((rb, LANES), lambda r: (r, 0)),
        out_shape=jax.ShapeDtypeStruct((R, LANES), F32),
        compiler_params=_cparams(("parallel",)),
        name="cumsum",
    )(x)


def _foxsamp_kernel(q_ref, k_ref, v_ref, z_ref, sm_ref, fb_ref, tot_ref, fcc_ref, ck_ref, cv_ref,
                    o_ref, ko_ref, vo_ref, lf_ref):
    S = q_ref.shape[0]
    HD = ck_ref.shape[1]
    p = pl.program_id(1)
    logf = _logf_from_small(sm_ref[...], fb_ref[...])
    lf_ref[...] = logf
    ltri = jnp.where(_iota2((S, S), 0) >= _iota2((S, S), 1), 1.0, 0.0).astype(BF16)
    fc = _dot_sel_l(ltri, logf) + tot_ref[...]
    fct = jnp.concatenate([fc, jnp.zeros((LANES - S, LANES), F32)], axis=0).T
    lane = _iota2((S, LANES), 1)
    row = _iota2((LANES, LANES), 0)
    causal = _iota2((S, S), 0) >= _iota2((S, S), 1)
    scale = HD ** -0.5
    HS = ck_ref.shape[0]
    for hh in range(HS):
        h = HS * p + hh
        fq = jnp.sum(jnp.where(lane == h, fc, 0.0), axis=1, keepdims=True)
        fkn = jnp.sum(jnp.where(row == h, fct, 0.0), axis=0, keepdims=True)[:, :S]
        q = q_ref[:, hh * HD:(hh + 1) * HD].astype(F32) * scale
        kn = k_ref[:, hh * HD:(hh + 1) * HD].astype(F32)
        vn = v_ref[:, hh * HD:(hh + 1) * HD].astype(F32)
        ko_ref[hh] = kn
        vo_ref[hh] = vn
        sc = _dot(q, ck_ref[hh]) + fq - fcc_ref[hh]
        sn = jnp.where(causal, _dot_nt(q, kn) + fq - fkn, -jnp.inf)
        m = jnp.maximum(jnp.max(sc, axis=1, keepdims=True), jnp.max(sn, axis=1, keepdims=True))
        pc = jnp.exp(sc - m)
        pn = jnp.exp(sn - m)
        l = jnp.sum(pc, axis=1, keepdims=True) + jnp.sum(pn, axis=1, keepdims=True)
        o = (_dot_nt(pc, cv_ref[hh]) + _dot(pn, vn)) / l
        o_ref[:, hh * HD:(hh + 1) * HD] = o * _silu(z_ref[:, hh * HD:(hh + 1) * HD].astype(F32))


def _foxsamp(proj, small, cols, fb_row, tot, fcc, cache_k, cache_v, l):
    B, S, _ = proj.shape
    _, _, H, HD, P = cache_k.shape
    cq, ck, cv, cz = cols
    hs = FOXSAMP_HEADS
    gw = hs * HD
    pair = lambda c: pl.BlockSpec((None, S, gw), lambda b, p: (b, 0, c * LANES // gw + p))
    kv_out = jax.ShapeDtypeStruct((B, H, S, HD), F32)
    return pl.pallas_call(
        _foxsamp_kernel,
        grid=(B, H // hs),
        in_specs=[pair(cq), pair(ck), pair(cv), pair(cz),
                  pl.BlockSpec((None, S, LANES), lambda b, p: (b, 0, 0)),
                  pl.BlockSpec((1, LANES), lambda b, p: (0, 0)),
                  pl.BlockSpec((None, 1, LANES), lambda b, p: (b, 0, 0)),
                  pl.BlockSpec((None, hs, 1, P), lambda b, p: (b, p, 0, 0)),
                  pl.BlockSpec((None, None, hs, HD, P), lambda b, p: (l, b, p, 0, 0)),
                  pl.BlockSpec((None, None, hs, HD, P), lambda b, p: (l, b, p, 0, 0))],
        out_specs=[pl.BlockSpec((None, S, gw), lambda b, p: (b, 0, p)),
                   pl.BlockSpec((None, hs, S, HD), lambda b, p: (b, p, 0, 0)),
                   pl.BlockSpec((None, hs, S, HD), lambda b, p: (b, p, 0, 0)),
                   pl.BlockSpec((None, S, LANES), lambda b, p: (b, 0, 0))],
        out_shape=[jax.ShapeDtypeStruct((B, S, H * HD), F32), kv_out, kv_out,
                   jax.ShapeDtypeStruct((B, S, LANES), F32)],
        compiler_params=_cparams(("parallel", "arbitrary")),
        name="foxsamp",
    )(proj, proj, proj, proj, small, fb_row, tot, fcc, cache_k, cache_v)


COL_MG = 0
COL_GQKV = 6
COL_FQKV = 9
COL_RQK = 12
COL_RV = 13
COL_RZ = 14
COL_FZ = 15
COL_GZ = 16


def _prep_w_in(w_in, D):
    BW = D // 2
    hk = BW // 2
    splits = (hk, hk, BW, BW, BW, BW, BW, H_FOX, BW, BW, BW, BW, H_GDN, H_GDN, BW, N_BRANCH * D)
    offs = np.cumsum((0,) + splits)
    seg = lambda i: w_in[..., offs[i]:offs[i + 1]]
    rq, rk, rv, rz, fq, fk, fv, ff, fz, gq, gk, gv, ga, gb, gz, mg = [seg(i) for i in range(16)]
    main = jnp.concatenate([mg, gq, gk, gv, fq, fk, fv, rq, rk, rv, rz, fz, gz], axis=-1)
    pad = jnp.zeros(w_in.shape[:-1] + (LANES - H_FOX - 2 * H_GDN,), w_in.dtype)
    small = jnp.concatenate([ff, ga, gb, pad], axis=-1)
    return main.astype(BF16), small.astype(BF16)


def _lane_row(vals, off):
    L, n = vals.shape
    return jnp.zeros((L, 1, LANES), F32).at[:, 0, off:off + n].set(vals)


def kernel(x_prompt, x_sample, c_prompt, c_sample, cache_fox_k, cache_fox_v, cache_fox_logf,
           state_ret, state_gdn, state_gdn_conv, norm_w, ada_w, ada_b, w_in, fox_f_bias,
           gdn_a_log, gdn_dt_bias, gdn_conv_w, ret_norm_w, gdn_norm_w, w_branch, w_out,
           final_norm_w):
    B, S, D = x_prompt.shape
    BS, SS, _ = x_sample.shape
    L = ada_w.shape[0]
    P = cache_fox_k.shape[3]
    DK_RET, DV_RET = state_ret.shape[-2:]
    DK_GDN, DV_GDN = state_gdn.shape[-2:]

    rows = -(-(B + BS) // 8) * 8
    c_all = jnp.concatenate([c_prompt, c_sample, jnp.zeros((rows - B - BS, D), F32)], axis=0)
    mod = _adaln(c_all, ada_w, ada_b)
    mod_p = mod[:, :, :B].reshape(L, 3, B, 1, D)
    mod_s = mod[:, :, B:B + BS].reshape(L, 3, BS, 1, D)

    w_main, w_small = _prep_w_in(w_in, D)
    w_branch_b = w_branch.astype(BF16)
    w_out_b = w_out.astype(BF16)
    norm_w3 = norm_w.reshape(L, 1, D)
    final_w = final_norm_w.reshape(1, D)
    ret_nw = ret_norm_w.reshape(L, H_RET, 1, DV_RET)
    gdn_nw = gdn_norm_w.reshape(L, 1, DV_GDN)
    fb_rows = _lane_row(fox_f_bias, 0)
    a_rows = _lane_row(gdn_a_log, H_FOX)
    dt_rows = _lane_row(gdn_dt_bias, H_FOX)

    tm_p = min(S, 2048)
    tmm_p = min(S, 1024)
    t_ret = min(S, 256)
    t_gdn = min(S, 512)
    t_prep = min(S, 256)
    t_flash = min(S, 1024)

    tabs_p = _ret_tables(t_ret, 0, S, DK_RET)
    tabs_s = _ret_tables(SS, P, SS, DK_RET)

    fcc = _cumsum_rows(cache_fox_logf.reshape(-1, LANES), P // LANES).reshape(L, BS, H_FOX, 1, P)
    tot = jnp.zeros((L, BS, 1, LANES), F32).at[:, :, 0, :H_FOX].set(fcc[:, :, :, 0, P - 1])

    zeros_ret = jnp.zeros((B, H_RET, DK_RET, DV_RET), F32)
    zeros_gdn = jnp.zeros((B, H_GDN, DK_GDN, DV_GDN), F32)
    zeros_conv = jnp.zeros((B, CONV_W - 1, gdn_conv_w.shape[-1]), F32)

    xp, xs = x_prompt, x_sample
    cache_kt = jnp.swapaxes(cache_fox_k, 3, 4)
    cache_vt = jnp.swapaxes(cache_fox_v, 3, 4)
    pk = jnp.zeros((L, B, H_FOX, cache_fox_k.shape[-1], S), F32)
    pv = jnp.zeros_like(pk)
    p_lf, p_ret, p_gdn, p_conv = [], [], [], []
    s_k, s_v, s_lf, s_ret, s_gdn, s_conv = [], [], [], [], [], []
    for l in range(L):
        final = l == L - 1
        proj, small = _inproj(xp, mod_p, l, norm_w3, w_main, w_small, 1, tm_p)
        o_ret, st = _ret(proj, (COL_RQK, COL_RV, COL_RZ), tabs_p, ret_nw[l], zeros_ret, t_ret, BF16)
        p_ret.append(st)
        o_gdn, st, cv = _gdn(proj, small, (COL_GQKV // 3, COL_GZ), gdn_conv_w[l], zeros_conv,
                             a_rows[l], dt_rows[l], gdn_nw[l], zeros_gdn, t_gdn, BF16)
        p_gdn.append(st)
        p_conv.append(cv)
        qa, ka, va, pk, pv, lf = _foxprep(proj, small, COL_FQKV // 3, fb_rows[l], l, pk, pv, L, t_prep)
        p_lf.append(lf)
        o_fox = _flash(qa, ka, va, proj, COL_FZ * 4, t_flash)
        xp = _merge(xp, mod_p, l, o_ret, o_fox, o_gdn, proj, w_branch_b, w_out_b, final_w,
                    1, tmm_p, final)
        proj, small = _inproj(xs, mod_s, l, norm_w3, w_main, w_small, BS, SS)
        o_ret, st = _ret(proj, (COL_RQK, COL_RV, COL_RZ), tabs_s, ret_nw[l], state_ret[l], SS, F32)
        s_ret.append(st)
        o_gdn, st, cv = _gdn(proj, small, (COL_GQKV // 3, COL_GZ), gdn_conv_w[l], state_gdn_conv[l],
                             a_rows[l], dt_rows[l], gdn_nw[l], state_gdn[l], SS, F32)
        s_gdn.append(st)
        s_conv.append(cv)
        o_fox, kk, vv, lf = _foxsamp(proj, small,
                                     (COL_FQKV * 4, (COL_FQKV + 1) * 4, (COL_FQKV + 2) * 4, COL_FZ * 4),
                                     fb_rows[l], tot[l], fcc[l], cache_kt, cache_vt, l)
        s_k.append(kk)
        s_v.append(vv)
        s_lf.append(lf)
        xs = _merge(xs, mod_s, l, o_ret, o_fox, o_gdn, proj, w_branch_b, w_out_b, final_w,
                    BS, SS, final)

    logf_out = lambda lfs: jnp.stack(lfs)[..., :H_FOX].transpose(0, 1, 3, 2)
    return (xp, xs,
            jnp.swapaxes(pk, 3, 4), jnp.swapaxes(pv, 3, 4), logf_out(p_lf), jnp.stack(p_ret), jnp.stack(p_gdn), jnp.stack(p_conv),
            jnp.stack(s_k), jnp.stack(s_v), logf_out(s_lf), jnp.stack(s_ret), jnp.stack(s_gdn),
            jnp.stack(s_conv))
```
